```python
import math
import jax, jax.numpy as jnp
from jax import lax
import numpy as np

D_MODEL = 4096
BATCH = 32
SEQ = 256
DEPTH = 2
DEC_BATCH = 4
DEC_SEQ = 1024
PAST_LEN = 256

GRID_W = 64
N_MIXERS = 2
N_MLA_LAYERS = (DEPTH + 1) // 2
N_DIFF_LAYERS = DEPTH // 2
MLA_HEADS = D_MODEL // 128
MLA_NOPE = 128
MLA_ROPE = 64
MLA_QK = MLA_NOPE + MLA_ROPE
MLA_V = 128
MLA_Q_LORA = 1024
MLA_KV_LORA = 512
MLA_IN = MLA_Q_LORA + MLA_KV_LORA + MLA_ROPE
DIFF_HEAD_DIM = 128
DIFF_HEADS = D_MODEL // (2 * DIFF_HEAD_DIM)
DIFF_V_DIM = 2 * DIFF_HEAD_DIM
N_GROUPS = 8
EXPERTS_PER_GROUP = 8
N_EXPERTS = N_GROUPS * EXPERTS_PER_GROUP
TOP_K = 2
EXPERT_HIDDEN = 1024
MOE_BLOCK = 128
Q_BLOCK = 128
ROPE_THETA = 10000.0
NORM_EPS = 1e-6

kernel_name = 'hybrid_mla_diffattn_hmoe_dit_step'

F32 = jnp.float32


def rms_norm(x, g):
    xf = x.astype(F32)
    y = xf * lax.rsqrt(jnp.mean(xf * xf, axis=-1, keepdims=True) + NORM_EPS)
    return (y * g.astype(F32)).astype(x.dtype)


def axial_rope_angles(n_tok, rot_dim):
    rows = n_tok // GRID_W
    n_freq = rot_dim // 4
    inv = jnp.exp(-math.log(ROPE_THETA) * jnp.arange(n_freq, dtype=F32) / n_freq)
    row_pos = jnp.repeat(jnp.arange(rows, dtype=F32), GRID_W)
    col_pos = jnp.tile(jnp.arange(GRID_W, dtype=F32), rows)
    ang = jnp.stack([row_pos[:, None] * inv, col_pos[:, None] * inv], axis=1)
    return jnp.cos(ang), jnp.sin(ang)


def apply_axial_rope(x, cos, sin):
    B, T, H, R = x.shape
    nf = R // 4
    xf = x.astype(F32).reshape(B, T, H, 2, 2, nf)
    x1, x2 = xf[..., 0, :], xf[..., 1, :]
    c = cos[None, :, None]
    s = sin[None, :, None]
    out = jnp.stack([x1 * c - x2 * s, x1 * s + x2 * c], axis=-2)
    return out.reshape(B, T, H, R).astype(x.dtype)


def rope_tail(x, cos, sin, rot_dim):
    return jnp.concatenate([x[..., :-rot_dim], apply_axial_rope(x[..., -rot_dim:], cos, sin)], axis=-1)


def _query_blocks(q):
    B, Tq = q.shape[:2]
    qb = Q_BLOCK if Tq % Q_BLOCK == 0 else Tq
    nb = Tq // qb
    return jnp.moveaxis(q.reshape((B, nb, qb) + q.shape[2:]), 1, 0)


def softmax_attention(q, k, v, scale):
    def one(q_blk):
        s = jnp.einsum('bqhd,bkhd->bhqk', q_blk, k, preferred_element_type=F32) * scale
        p = jax.nn.softmax(s, axis=-1).astype(v.dtype)
        return jnp.einsum('bhqk,bkhd->bqhd', p, v)
    o = lax.map(one, _query_blocks(q))
    B, Tq, H = q.shape[:3]
    return jnp.moveaxis(o, 0, 1).reshape(B, Tq, H, v.shape[-1])


def differential_attention(q, k, v, lam, scale):
    def one(q_blk):
        s = jnp.einsum('bqhjd,bkhjd->bhjqk', q_blk, k, preferred_element_type=F32) * scale
        p = jax.nn.softmax(s, axis=-1)
        a = (p[:, :, 0] - lam * p[:, :, 1]).astype(v.dtype)
        return jnp.einsum('bhqk,bkhd->bqhd', a, v)
    o = lax.map(one, _query_blocks(q))
    B, Tq, H = q.shape[:3]
    return jnp.moveaxis(o, 0, 1).reshape(B, Tq, H, v.shape[-1])


def mla_project(h, w_in, q_a_norm, w_q_up, kv_a_norm):
    B, T, _ = h.shape
    proj = jnp.einsum('btd,de->bte', h, w_in)
    cq, ckv, krope = jnp.split(proj, [MLA_Q_LORA, MLA_Q_LORA + MLA_KV_LORA], axis=-1)
    q = jnp.einsum('btr,re->bte', rms_norm(cq, q_a_norm), w_q_up).reshape(B, T, MLA_HEADS, MLA_QK)
    return q, rms_norm(ckv, kv_a_norm), krope


def mla_expand(ckv, krope, w_kv_up):
    B, T, _ = ckv.shape
    kv = jnp.einsum('btr,re->bte', ckv, w_kv_up).reshape(B, T, MLA_HEADS, MLA_NOPE + MLA_V)
    k_nope, v = jnp.split(kv, [MLA_NOPE], axis=-1)
    k = jnp.concatenate([k_nope, jnp.broadcast_to(krope[:, :, None, :], (B, T, MLA_HEADS, MLA_ROPE))], axis=-1)
    return k, v


def mla_context(h, w_in, q_a_norm, w_q_up, kv_a_norm, w_kv_up, q_norm, k_norm, w_out):
    B, T, _ = h.shape
    q, ckv, krope = mla_project(h, w_in, q_a_norm, w_q_up, kv_a_norm)
    k, v = mla_expand(ckv, krope, w_kv_up)
    o = softmax_attention(rms_norm(q, q_norm), rms_norm(k, k_norm), v, MLA_QK ** -0.5)
    out = jnp.einsum('bte,ed->btd', o.reshape(B, T, MLA_HEADS * MLA_V), w_out)
    return out, ckv, krope


def mla_latent(h, ckv_ctx, krope_ctx, w_in, q_a_norm, w_q_up, kv_a_norm, w_kv_up, q_norm, k_norm, w_out):
    B, T, _ = h.shape
    q, ckv, krope = mla_project(h, w_in, q_a_norm, w_q_up, kv_a_norm)
    k_lat, v_lat = mla_expand(ckv, krope, w_kv_up)
    k_ctx, v_ctx = mla_expand(ckv_ctx, krope_ctx, w_kv_up)
    cos, sin = axial_rope_angles(T, MLA_ROPE)
    q = rope_tail(rms_norm(q, q_norm), cos, sin, MLA_ROPE)
    k_lat = rope_tail(rms_norm(k_lat, k_norm), cos, sin, MLA_ROPE)
    k = jnp.concatenate([k_lat, rms_norm(k_ctx, k_norm)], axis=1)
    v = jnp.concatenate([v_lat, v_ctx], axis=1)
    o = softmax_attention(q, k, v, MLA_QK ** -0.5)
    return jnp.einsum('bte,ed->btd', o.reshape(B, T, MLA_HEADS * MLA_V), w_out)


def diff_project(h, w_qkv, q_norm, k_norm):
    B, T, _ = h.shape
    q, k, v = jnp.split(jnp.einsum('btd,de->bte', h, w_qkv), 3, axis=-1)
    q = rms_norm(q.reshape(B, T, DIFF_HEADS, 2, DIFF_HEAD_DIM), q_norm)
    k = rms_norm(k.reshape(B, T, DIFF_HEADS, 2, DIFF_HEAD_DIM), k_norm)
    return q, k, v.reshape(B, T, DIFF_HEADS, DIFF_V_DIM)


def diff_lambda(lq1, lk1, lq2, lk2, lam_init):
    e1 = jnp.exp(jnp.sum(lq1.astype(F32) * lk1.astype(F32)))
    e2 = jnp.exp(jnp.sum(lq2.astype(F32) * lk2.astype(F32)))
    return e1 - e2 + lam_init


def diff_finish(o, out_norm, lam_init, w_out):
    B, T = o.shape[:2]
    o = rms_norm(o, out_norm) * (1.0 - lam_init)
    return jnp.einsum('bte,ed->btd', o.reshape(B, T, DIFF_HEADS * DIFF_V_DIM), w_out)


def rope_subheads(x, cos, sin):
    B, T, H, J, d = x.shape
    return apply_axial_rope(x.reshape(B, T, H * J, d), cos, sin).reshape(B, T, H, J, d)


def diff_context(h, lam_init, w_qkv, q_norm, k_norm, lq1, lk1, lq2, lk2, out_norm, w_out):
    q, k, v = diff_project(h, w_qkv, q_norm, k_norm)
    lam = diff_lambda(lq1, lk1, lq2, lk2, lam_init)
    o = differential_attention(q, k, v, lam, DIFF_HEAD_DIM ** -0.5)
    return diff_finish(o, out_norm, lam_init, w_out), k, v


def diff_latent(h, k_ctx, v_ctx, lam_init, w_qkv, q_norm, k_norm, lq1, lk1, lq2, lk2, out_norm, w_out):
    T = h.shape[1]
    q, k, v = diff_project(h, w_qkv, q_norm, k_norm)
    cos, sin = axial_rope_angles(T, DIFF_HEAD_DIM)
    q = rope_subheads(q, cos, sin)
    k = jnp.concatenate([rope_subheads(k, cos, sin), k_ctx], axis=1)
    v = jnp.concatenate([v, v_ctx], axis=1)
    lam = diff_lambda(lq1, lk1, lq2, lk2, lam_init)
    o = differential_attention(q, k, v, lam, DIFF_HEAD_DIM ** -0.5)
    return diff_finish(o, out_norm, lam_init, w_out)


def hier_moe(x, w_group, b_group, w_router, b_router, w_gate, w_up, w_down):
    B, T, D = x.shape
    xf = x.reshape(B * T, D)
    n_tok = B * T
    g_logits = jnp.einsum('td,dg->tg', xf, w_group).astype(F32) + b_group.astype(F32)
    g_prob = jax.nn.softmax(g_logits, axis=-1)
    g_sel = jnp.argmax(g_logits, axis=-1).astype(jnp.int32)
    p_group = jnp.take_along_axis(g_prob, g_sel[:, None], axis=-1)[:, 0]
    e_logits = (jnp.einsum('td,de->te', xf, w_router).astype(F32) + b_router.astype(F32))
    e_logits = e_logits.reshape(n_tok, N_GROUPS, EXPERTS_PER_GROUP)
    in_group = jnp.take_along_axis(e_logits, g_sel[:, None, None], axis=1)[:, 0]
    top_vals, top_idx = lax.top_k(in_group, TOP_K)
    gates = p_group[:, None] * jax.nn.softmax(top_vals, axis=-1)
    expert_ids = g_sel[:, None] * EXPERTS_PER_GROUP + top_idx.astype(jnp.int32)

    n_assign = n_tok * TOP_K
    n_blocks = n_assign // MOE_BLOCK + N_EXPERTS
    flat_e = expert_ids.reshape(n_assign)
    flat_tok = jnp.arange(n_assign, dtype=jnp.int32) // TOP_K
    flat_g = gates.reshape(n_assign)
    order = jnp.argsort(flat_e)
    sorted_e = flat_e[order]
    counts = jnp.zeros((N_EXPERTS,), jnp.int32).at[flat_e].add(1)
    starts = jnp.cumsum(counts) - counts
    padded = (counts + MOE_BLOCK - 1) // MOE_BLOCK * MOE_BLOCK
    pad_ends = jnp.cumsum(padded)
    pad_starts = pad_ends - padded
    dest = pad_starts[sorted_e] + (jnp.arange(n_assign, dtype=jnp.int32) - starts[sorted_e])
    slot_tok = jnp.zeros((n_blocks * MOE_BLOCK,), jnp.int32).at[dest].set(flat_tok[order])
    slot_gate = jnp.zeros((n_blocks * MOE_BLOCK,), F32).at[dest].set(flat_g[order])
    block_start = jnp.arange(n_blocks, dtype=jnp.int32) * MOE_BLOCK
    block_e = jnp.minimum(jnp.searchsorted(pad_ends, block_start, side='right'), N_EXPERTS - 1).astype(jnp.int32)
    xb = xf[slot_tok].reshape(n_blocks, MOE_BLOCK, D)

    def expert_block(args):
        x_blk, e = args
        hid = jax.nn.silu(x_blk @ w_gate[e]) * (x_blk @ w_up[e])
        return hid @ w_down[e]

    yb = lax.map(expert_block, (xb, block_e)).reshape(n_blocks * MOE_BLOCK, D)
    y = jax.ops.segment_sum(yb.astype(F32) * slot_gate[:, None], slot_tok, num_segments=n_tok)
    return y.astype(x.dtype).reshape(B, T, D)


def adaln(cond, w_ada, b_ada):
    mod = jnp.einsum('bd,de->be', jax.nn.silu(cond), w_ada) + b_ada
    return [m[:, None, :] for m in jnp.split(mod, 6, axis=-1)]


def modulate(x, g, shift, scale):
    return rms_norm(x, g) * (1.0 + scale) + shift


def setup_inputs(seed: int = 0) -> dict:
    key = jax.random.key(seed)
    ks = iter(jax.random.split(key, 64))

    def nrm(shape, scale):
        return jax.random.normal(next(ks), shape, F32) * scale

    def gain(shape):
        return 1.0 + nrm(shape, 0.01)

    NM, ND = N_MLA_LAYERS, N_DIFF_LAYERS
    D = D_MODEL
    return {
        'x_prompt': nrm((BATCH, SEQ, D), 1.0),
        'x_sample': nrm((DEC_BATCH, DEC_SEQ, D), 1.0),
        'cache_mla_ckv': nrm((DEC_BATCH, NM, PAST_LEN, MLA_KV_LORA), 1.0),
        'cache_mla_krope': nrm((DEC_BATCH, NM, PAST_LEN, MLA_ROPE), 1.0),
        'cache_diff_k': nrm((DEC_BATCH, ND, PAST_LEN, DIFF_HEADS, 2, DIFF_HEAD_DIM), 1.0),
        'cache_diff_v': nrm((DEC_BATCH, ND, PAST_LEN, DIFF_HEADS, DIFF_V_DIM), 1.0),
        'c': nrm((DEC_BATCH, D), 1.0),
        'c_ctx': nrm((D,), 1.0),
        'norm1_g': gain((DEPTH, D)),
        'norm2_g': gain((DEPTH, D)),
        'ada_w': nrm((DEPTH, D, 6 * D), 0.5 * D ** -0.5),
        'ada_b': nrm((DEPTH, 6 * D), 0.01),
        'mla_w_in': nrm((NM, D, MLA_IN), D ** -0.5),
        'mla_q_a_norm': gain((NM, MLA_Q_LORA)),
        'mla_w_q_up': nrm((NM, MLA_Q_LORA, MLA_HEADS * MLA_QK), MLA_Q_LORA ** -0.5),
        'mla_kv_a_norm': gain((NM, MLA_KV_LORA)),
        'mla_w_kv_up': nrm((NM, MLA_KV_LORA, MLA_HEADS * (MLA_NOPE + MLA_V)), MLA_KV_LORA ** -0.5),
        'mla_q_norm': gain((NM, MLA_QK)),
        'mla_k_norm': gain((NM, MLA_QK)),
        'mla_w_out': nrm((NM, MLA_HEADS * MLA_V, D), (MLA_HEADS * MLA_V) ** -0.5),
        'diff_w_qkv': nrm((ND, D, 3 * DIFF_HEADS * 2 * DIFF_HEAD_DIM), D ** -0.5),
        'diff_q_norm': gain((ND, DIFF_HEAD_DIM)),
        'diff_k_norm': gain((ND, DIFF_HEAD_DIM)),
        'diff_lambda_q1': nrm((ND, DIFF_HEAD_DIM), 0.1),
        'diff_lambda_k1': nrm((ND, DIFF_HEAD_DIM), 0.1),
        'diff_lambda_q2': nrm((ND, DIFF_HEAD_DIM), 0.1),
        'diff_lambda_k2': nrm((ND, DIFF_HEAD_DIM), 0.1),
        'diff_out_norm': gain((ND, DIFF_V_DIM)),
        'diff_w_out': nrm((ND, DIFF_HEADS * DIFF_V_DIM, D), (DIFF_HEADS * DIFF_V_DIM) ** -0.5),
        'moe_w_group': nrm((DEPTH, D, N_GROUPS), D ** -0.5),
        'moe_b_group': nrm((DEPTH, N_GROUPS), 0.01),
        'moe_w_router': nrm((DEPTH, D, N_EXPERTS), D ** -0.5),
        'moe_b_router': nrm((DEPTH, N_EXPERTS), 0.01),
        'moe_w_gate': nrm((DEPTH, N_EXPERTS, D, EXPERT_HIDDEN), D ** -0.5),
        'moe_w_up': nrm((DEPTH, N_EXPERTS, D, EXPERT_HIDDEN), D ** -0.5),
        'moe_w_down': nrm((DEPTH, N_EXPERTS, EXPERT_HIDDEN, D), EXPERT_HIDDEN ** -0.5),
    }


def reference(x_prompt, x_sample, cache_mla_ckv, cache_mla_krope, cache_diff_k, cache_diff_v, c, c_ctx,
              norm1_g, norm2_g, ada_w, ada_b,
              mla_w_in, mla_q_a_norm, mla_w_q_up, mla_kv_a_norm, mla_w_kv_up, mla_q_norm, mla_k_norm, mla_w_out,
              diff_w_qkv, diff_q_norm, diff_k_norm, diff_lambda_q1, diff_lambda_k1, diff_lambda_q2, diff_lambda_k2,
              diff_out_norm, diff_w_out,
              moe_w_group, moe_b_group, moe_w_router, moe_b_router, moe_w_gate, moe_w_up, moe_w_down):
    xp = x_prompt
    xs = x_sample
    cond_ctx = c_ctx[None, :]
    new_ckv, new_krope, new_k, new_v = [], [], [], []
    for i in range(DEPTH):
        sh1p, sc1p, g1p, sh2p, sc2p, g2p = adaln(cond_ctx, ada_w[i], ada_b[i])
        sh1s, sc1s, g1s, sh2s, sc2s, g2s = adaln(c, ada_w[i], ada_b[i])
        hp = modulate(xp, norm1_g[i], sh1p, sc1p)
        hs = modulate(xs, norm1_g[i], sh1s, sc1s)
        j = i // N_MIXERS
        if i % N_MIXERS == 0:
            mla_w = (mla_w_in[j], mla_q_a_norm[j], mla_w_q_up[j], mla_kv_a_norm[j], mla_w_kv_up[j],
                     mla_q_norm[j], mla_k_norm[j], mla_w_out[j])
            op, ckv, krope = mla_context(hp, *mla_w)
            os_ = mla_latent(hs, cache_mla_ckv[:, j], cache_mla_krope[:, j], *mla_w)
            new_ckv.append(ckv)
            new_krope.append(krope)
        else:
            lam_init = 0.8 - 0.6 * math.exp(-0.3 * i)
            diff_w = (diff_w_qkv[j], diff_q_norm[j], diff_k_norm[j], diff_lambda_q1[j], diff_lambda_k1[j],
                      diff_lambda_q2[j], diff_lambda_k2[j], diff_out_norm[j], diff_w_out[j])
            op, k_c, v_c = diff_context(hp, lam_init, *diff_w)
            os_ = diff_latent(hs, cache_diff_k[:, j], cache_diff_v[:, j], lam_init, *diff_w)
            new_k.append(k_c)
            new_v.append(v_c)
        xp = xp + g1p * op
        xs = xs + g1s * os_
        moe_w = (moe_w_group[i], moe_b_group[i], moe_w_router[i], moe_b_router[i],
                 moe_w_gate[i], moe_w_up[i], moe_w_down[i])
        xp = xp + g2p * hier_moe(modulate(xp, norm2_g[i], sh2p, sc2p), *moe_w)
        xs = xs + g2s * hier_moe(modulate(xs, norm2_g[i], sh2s, sc2s), *moe_w)
    return (xp, xs, jnp.stack(new_ckv, axis=1), jnp.stack(new_krope, axis=1),
            jnp.stack(new_k, axis=1), jnp.stack(new_v, axis=1))
```

```python
import functools
import math

import jax
import jax.numpy as jnp
from jax import lax
from jax.experimental import pallas as pl
from jax.experimental.pallas import tpu as pltpu

F32 = jnp.float32
BF16 = jnp.bfloat16
I32 = jnp.int32

D_MODEL = 4096
BATCH = 32
SEQ = 256
DEPTH = 2
DEC_BATCH = 4
DEC_SEQ = 1024
PAST_LEN = 256
GRID_W = 64
MLA_HEADS = 32
MLA_NOPE = 128
MLA_ROPE = 64
MLA_QK = MLA_NOPE + MLA_ROPE
MLA_V = 128
MLA_Q_LORA = 1024
MLA_KV_LORA = 512
MLA_IN = MLA_Q_LORA + MLA_KV_LORA + MLA_ROPE
MLA_IN_PAD = MLA_Q_LORA + MLA_KV_LORA + 128
MLA_HEAD_PAD = 256
DIFF_HEAD_DIM = 128
DIFF_HEADS = 16
DIFF_V_DIM = 256
N_GROUPS = 8
EXPERTS_PER_GROUP = 8
N_EXPERTS = 64
TOP_K = 2
EXPERT_HIDDEN = 1024
ROPE_THETA = 10000.0
NORM_EPS = 1e-6

NP = BATCH * SEQ
NS = DEC_BATCH * DEC_SEQ
TT = NP + NS
NCTX = DEC_BATCH * PAST_LEN
N_COND = 8
N_ASSIGN = TT * TOP_K
MOE_ROWS = 256
N_SLOTS = N_ASSIGN + N_EXPERTS * MOE_ROWS
N_BLOCKS = N_SLOTS // MOE_ROWS
ROUTER_LANES = 128

VMEM_LIMIT = 56 * 1024 * 1024


def _cparams(sem):
    return pltpu.CompilerParams(dimension_semantics=sem, vmem_limit_bytes=VMEM_LIMIT)


def _cond_row(row0):
    return jnp.where(row0 < NP, 0, 1 + (row0 - NP) // DEC_SEQ)


def _rms(x, g):
    return x * lax.rsqrt(jnp.mean(x * x, axis=-1, keepdims=True) + NORM_EPS) * g


def _modulate(x, g, shift, scale):
    return _rms(x, g) * (1.0 + scale) + shift


def _dot(a, b):
    return jnp.dot(a, b, preferred_element_type=F32)


def _dot_nt(a, b):
    return lax.dot_general(a, b, (((1,), (1,)), ((), ())), preferred_element_type=F32)


def _rope128(x, cos, sin_signed, half):
    lane = lax.broadcasted_iota(I32, x.shape, 1)
    first_half = ((lane // half) % 2) == 0
    partner = jnp.where(first_half, pltpu.roll(x, 128 - half, 1), pltpu.roll(x, half, 1))
    return x * cos + partner * sin_signed


def _latent_tables(cos_ref, sin_ref, is_latent):
    m = is_latent.astype(F32)
    return 1.0 + m * (cos_ref[...] - 1.0), m * sin_ref[...]


def _ada_kernel(c_ref, w_ref, b_ref, o_ref):
    c = c_ref[...]
    a = (c * jax.nn.sigmoid(c)).astype(BF16)
    o_ref[...] = _dot(a, w_ref[...].astype(BF16)) + b_ref[...]


def _ada_call(cond, ada_w, ada_b):
    tn = 512
    n = 6 * D_MODEL
    return pl.pallas_call(
        _ada_kernel,
        grid=(DEPTH, n // tn),
        in_specs=[
            pl.BlockSpec((N_COND, D_MODEL), lambda l, j: (0, 0)),
            pl.BlockSpec((None, D_MODEL, tn), lambda l, j: (l, 0, j)),
            pl.BlockSpec((None, 1, tn), lambda l, j: (l, 0, j)),
        ],
        out_specs=pl.BlockSpec((None, N_COND, tn), lambda l, j: (l, 0, j)),
        out_shape=jax.ShapeDtypeStruct((DEPTH, N_COND, n), F32),
        compiler_params=_cparams(("arbitrary", "arbitrary")),
        name="ada_table",
    )(cond, ada_w, ada_b.reshape(DEPTH, 1, n))


def _mod_spec(tm, chunk, width=D_MODEL, with_j=False):
    per = D_MODEL // width
    if with_j:
        return pl.BlockSpec((None, 1, width), lambda i, j: (_cond_row(i * tm), 0, chunk * per + j))
    return pl.BlockSpec((None, 1, width), lambda i: (_cond_row(i * tm), 0, chunk))


def _mla_in_kernel(x_ref, g_ref, sh_ref, sc_ref, w_ref, qg_ref, kvg_ref, cq_ref, ckv_ref, kr_ref):
    h = _modulate(x_ref[...], g_ref[...], sh_ref[...], sc_ref[...]).astype(BF16)
    p = _dot(h, w_ref[...])
    cq_ref[...] = _rms(p[:, :MLA_Q_LORA], qg_ref[...]).astype(BF16)
    ckv_ref[...] = _rms(p[:, MLA_Q_LORA:MLA_Q_LORA + MLA_KV_LORA], kvg_ref[...])
    kr_ref[...] = p[:, MLA_Q_LORA + MLA_KV_LORA:]


def _mla_in_call(x, g, mod, w_in, q_a_norm, kv_a_norm):
    tm = 256
    return pl.pallas_call(
        _mla_in_kernel,
        grid=(TT // tm,),
        in_specs=[
            pl.BlockSpec((tm, D_MODEL), lambda i: (i, 0)),
            pl.BlockSpec((1, D_MODEL), lambda i: (0, 0)),
            _mod_spec(tm, 0),
            _mod_spec(tm, 1),
            pl.BlockSpec((D_MODEL, MLA_IN_PAD), lambda i: (0, 0)),
            pl.BlockSpec((1, MLA_Q_LORA), lambda i: (0, 0)),
            pl.BlockSpec((1, MLA_KV_LORA), lambda i: (0, 0)),
        ],
        out_specs=[
            pl.BlockSpec((tm, MLA_Q_LORA), lambda i: (i, 0)),
            pl.BlockSpec((tm, MLA_KV_LORA), lambda i: (i, 0)),
            pl.BlockSpec((tm, 128), lambda i: (i, 0)),
        ],
        out_shape=[
            jax.ShapeDtypeStruct((TT, MLA_Q_LORA), BF16),
            jax.ShapeDtypeStruct((TT, MLA_KV_LORA), F32),
            jax.ShapeDtypeStruct((TT, 128), F32),
        ],
        compiler_params=_cparams(("arbitrary",)),
        name="mla_in",
    )(x, g, mod, mod, w_in, q_a_norm, kv_a_norm)


def _q_up_kernel(c_ref, w_ref, g_ref, cos_ref, sin_ref, q_ref, *, tm, heads):
    i = pl.program_id(0)
    cos, sin = _latent_tables(cos_ref, sin_ref, i >= NP // tm)
    o = _dot(c_ref[...], w_ref[...])
    g = g_ref[...] * (MLA_QK ** -0.5)
    for hh in range(heads):
        s = o[:, hh * 256:(hh + 1) * 256]
        r = lax.rsqrt(jnp.sum(s * s, axis=-1, keepdims=True) / MLA_QK + NORM_EPS)
        lo = s[:, :128] * r * g[:, :128]
        hi = _rope128(s[:, 128:] * r * g[:, 128:], cos, sin, MLA_ROPE // 4)
        q_ref[:, hh * 256:hh * 256 + 128] = lo.astype(BF16)
        q_ref[:, hh * 256 + 128:(hh + 1) * 256] = hi.astype(BF16)


def _pos_block(i, tm):
    return jnp.clip(i - NP // tm, 0, NS // tm - 1) % (DEC_SEQ // tm)


def _q_up_call(cq, w_q, q_norm_pad, cos, sin):
    tm, heads = 512, 8
    tn = heads * MLA_HEAD_PAD
    return pl.pallas_call(
        functools.partial(_q_up_kernel, tm=tm, heads=heads),
        grid=(TT // tm, MLA_HEADS // heads),
        in_specs=[
            pl.BlockSpec((tm, MLA_Q_LORA), lambda i, j: (i, 0)),
            pl.BlockSpec((MLA_Q_LORA, tn), lambda i, j: (0, j)),
            pl.BlockSpec((1, MLA_HEAD_PAD), lambda i, j: (0, 0)),
            pl.BlockSpec((tm, 128), lambda i, j: (_pos_block(i, tm), 0)),
            pl.BlockSpec((tm, 128), lambda i, j: (_pos_block(i, tm), 0)),
        ],
        out_specs=pl.BlockSpec((tm, tn), lambda i, j: (i, j)),
        out_shape=jax.ShapeDtypeStruct((TT, MLA_HEADS * MLA_HEAD_PAD), BF16),
        compiler_params=_cparams(("arbitrary", "arbitrary")),
        name="mla_q_up",
    )(cq, w_q, q_norm_pad, cos, sin)


def _kv_up_kernel(c_ref, kr_ref, w_ref, g_ref, cos_ref, sin_ref, k_ref, v_ref, *, tm, heads):
    i = pl.program_id(0)
    cos, sin = _latent_tables(cos_ref, sin_ref, (i >= NP // tm) & (i < TT // tm))
    kv = _dot(c_ref[...].astype(BF16), w_ref[...])
    g = g_ref[...]
    kr = kr_ref[...]
    ss_rope = jnp.sum(kr * kr, axis=-1, keepdims=True)
    base = _rope128(kr * g[:, 128:], cos, sin, MLA_ROPE // 4)
    for hh in range(heads):
        kn = kv[:, hh * 256:hh * 256 + 128]
        r = lax.rsqrt((jnp.sum(kn * kn, axis=-1, keepdims=True) + ss_rope) / MLA_QK + NORM_EPS)
        k_ref[:, hh * 256:hh * 256 + 128] = (kn * r * g[:, :128]).astype(BF16)
        k_ref[:, hh * 256 + 128:(hh + 1) * 256] = (base * r).astype(BF16)
        v_ref[:, hh * 128:(hh + 1) * 128] = kv[:, hh * 256 + 128:(hh + 1) * 256].astype(BF16)


def _kv_up_call(ckv_all, kr_all, w_kv, k_norm_pad, cos, sin):
    tm, heads = 512, 8
    rows = TT + NCTX
    return pl.pallas_call(
        functools.partial(_kv_up_kernel, tm=tm, heads=heads),
        grid=(rows // tm, MLA_HEADS // heads),
        in_specs=[
            pl.BlockSpec((tm, MLA_KV_LORA), lambda i, j: (i, 0)),
            pl.BlockSpec((tm, 128), lambda i, j: (i, 0)),
            pl.BlockSpec((MLA_KV_LORA, heads * 256), lambda i, j: (0, j)),
            pl.BlockSpec((1, MLA_HEAD_PAD), lambda i, j: (0, 0)),
            pl.BlockSpec((tm, 128), lambda i, j: (_pos_block(i, tm), 0)),
            pl.BlockSpec((tm, 128), lambda i, j: (_pos_block(i, tm), 0)),
        ],
        out_specs=[
            pl.BlockSpec((tm, heads * MLA_HEAD_PAD), lambda i, j: (i, j)),
            pl.BlockSpec((tm, heads * MLA_V), lambda i, j: (i, j)),
        ],
        out_shape=[
            jax.ShapeDtypeStruct((rows, MLA_HEADS * MLA_HEAD_PAD), BF16),
            jax.ShapeDtypeStruct((rows, MLA_HEADS * MLA_V), BF16),
        ],
        compiler_params=_cparams(("arbitrary", "arbitrary")),
        name="mla_kv_up",
    )(ckv_all, kr_all, w_kv, k_norm_pad, cos, sin)


def _mla_attn_prompt_kernel(q_ref, k_ref, v_ref, o_ref, *, heads):
    for hh in range(heads):
        q = q_ref[:, hh * 256:(hh + 1) * 256]
        k = k_ref[:, hh * 256:(hh + 1) * 256]
        s = _dot_nt(q, k)
        p = jnp.exp(s - jnp.max(s, axis=-1, keepdims=True))
        l = jnp.sum(p, axis=-1, keepdims=True)
        o = _dot(p.astype(BF16), v_ref[:, hh * 128:(hh + 1) * 128]) / l
        o_ref[:, hh * 128:(hh + 1) * 128] = o.astype(BF16)


def _mla_attn_prompt_call(q, k, v):
    heads = 8
    return pl.pallas_call(
        functools.partial(_mla_attn_prompt_kernel, heads=heads),
        grid=(BATCH, MLA_HEADS // heads),
        in_specs=[
            pl.BlockSpec((SEQ, heads * 256), lambda b, j: (b, j)),
            pl.BlockSpec((SEQ, heads * 256), lambda b, j: (b, j)),
            pl.BlockSpec((SEQ, heads * 128), lambda b, j: (b, j)),
        ],
        out_specs=pl.BlockSpec((SEQ, heads * 128), lambda b, j: (b, j)),
        out_shape=jax.ShapeDtypeStruct((NP, MLA_HEADS * MLA_V), BF16),
        compiler_params=_cparams(("arbitrary", "arbitrary")),
        name="mla_attn_prompt",
    )(q, k, v)


def _mla_attn_latent_kernel(q_ref, kl_ref, kc_ref, vl_ref, vc_ref, o_ref, *, heads):
    for hh in range(heads):
        q = q_ref[:, hh * 256:(hh + 1) * 256]
        sl = _dot_nt(q, kl_ref[:, hh * 256:(hh + 1) * 256])
        sc = _dot_nt(q, kc_ref[:, hh * 256:(hh + 1) * 256])
        m = jnp.maximum(jnp.max(sl, axis=-1, keepdims=True), jnp.max(sc, axis=-1, keepdims=True))
        pl_ = jnp.exp(sl - m)
        pc = jnp.exp(sc - m)
        l = jnp.sum(pl_, axis=-1, keepdims=True) + jnp.sum(pc, axis=-1, keepdims=True)
        o = _dot(pl_.astype(BF16), vl_ref[:, hh * 128:(hh + 1) * 128])
        o = o + _dot(pc.astype(BF16), vc_ref[:, hh * 128:(hh + 1) * 128])
        o_ref[:, hh * 128:(hh + 1) * 128] = (o / l).astype(BF16)


def _mla_attn_latent_call(q, k, v):
    heads, tq = 8, 512
    nq = DEC_SEQ // tq
    qrow = lambda b, j, t: (NP // tq + b * nq + t, j)
    return pl.pallas_call(
        functools.partial(_mla_attn_latent_kernel, heads=heads),
        grid=(DEC_BATCH, MLA_HEADS // heads, nq),
        in_specs=[
            pl.BlockSpec((tq, heads * 256), qrow),
            pl.BlockSpec((DEC_SEQ, heads * 256), lambda b, j, t: (NP // DEC_SEQ + b, j)),
            pl.BlockSpec((PAST_LEN, heads * 256), lambda b, j, t: (TT // PAST_LEN + b, j)),
            pl.BlockSpec((DEC_SEQ, heads * 128), lambda b, j, t: (NP // DEC_SEQ + b, j)),
            pl.BlockSpec((PAST_LEN, heads * 128), lambda b, j, t: (TT // PAST_LEN + b, j)),
        ],
        out_specs=pl.BlockSpec((tq, heads * 128), lambda b, j, t: (b * nq + t, j)),
        out_shape=jax.ShapeDtypeStruct((NS, MLA_HEADS * MLA_V), BF16),
        compiler_params=_cparams(("arbitrary", "arbitrary", "arbitrary")),
        name="mla_attn_latent",
    )(q, k, k, v, v)


def _out_proj_kernel(ap_ref, as_ref, w_ref, x_ref, gate_ref, o_ref, *, tm):
    def finish(a_ref):
        o_ref[...] = x_ref[...] + gate_ref[...] * _dot(a_ref[...], w_ref[...])

    is_prompt = pl.program_id(0) < NP // tm
    pl.when(is_prompt)(lambda: finish(ap_ref))
    pl.when(jnp.logical_not(is_prompt))(lambda: finish(as_ref))


def _out_proj_call(a_prompt, a_latent, w, x, mod):
    tm, tn = 512, 1024
    return pl.pallas_call(
        functools.partial(_out_proj_kernel, tm=tm),
        grid=(TT // tm, D_MODEL // tn),
        in_specs=[
            pl.BlockSpec((tm, D_MODEL), lambda i, j: (jnp.minimum(i, NP // tm - 1), 0)),
            pl.BlockSpec((tm, D_MODEL), lambda i, j: (jnp.maximum(i - NP // tm, 0), 0)),
            pl.BlockSpec((D_MODEL, tn), lambda i, j: (0, j)),
            pl.BlockSpec((tm, tn), lambda i, j: (i, j)),
            _mod_spec(tm, 2, width=tn, with_j=True),
        ],
        out_specs=pl.BlockSpec((tm, tn), lambda i, j: (i, j)),
        out_shape=jax.ShapeDtypeStruct((TT, D_MODEL), F32),
        compiler_params=_cparams(("arbitrary", "arbitrary")),
        name="attn_out_proj",
    )(a_prompt, a_latent, w, x, mod)


def _diff_qkv_kernel(x_ref, g_ref, sh_ref, sc_ref, w_ref, qn_ref, kn_ref, cos_ref, sin_ref,
                     q_ref, k_ref, v_ref, h_s, *, tm, nq):
    i = pl.program_id(0)
    j = pl.program_id(1)

    @pl.when(j == 0)
    def _():
        h_s[...] = _modulate(x_ref[...], g_ref[...], sh_ref[...], sc_ref[...]).astype(BF16)

    o = _dot(h_s[...], w_ref[...])
    cos, sin = _latent_tables(cos_ref, sin_ref, i >= NP // tm)

    def norm_rope(g, scale):
        parts = []
        for s in range(o.shape[1] // 128):
            y = _rms(o[:, s * 128:(s + 1) * 128], g) * scale
            parts.append(_rope128(y, cos, sin, DIFF_HEAD_DIM // 4))
        return parts

    @pl.when(j < nq)
    def _():
        for s, y in enumerate(norm_rope(qn_ref[...], DIFF_HEAD_DIM ** -0.5)):
            q_ref[:, s * 128:(s + 1) * 128] = y.astype(BF16)

    @pl.when((j >= nq) & (j < 2 * nq))
    def _():
        for s, y in enumerate(norm_rope(kn_ref[...], 1.0)):
            k_ref[:, s * 128:(s + 1) * 128] = y

    @pl.when(j >= 2 * nq)
    def _():
        v_ref[...] = o


def _diff_qkv_call(x, g, mod, w_qkv, q_norm, k_norm, cos, sin):
    tm, tn = 512, 512
    nq = D_MODEL // tn
    return pl.pallas_call(
        functools.partial(_diff_qkv_kernel, tm=tm, nq=nq),
        grid=(TT // tm, 3 * nq),
        in_specs=[
            pl.BlockSpec((tm, D_MODEL), lambda i, j: (i, 0)),
            pl.BlockSpec((1, D_MODEL), lambda i, j: (0, 0)),
            pl.BlockSpec((None, 1, D_MODEL), lambda i, j: (_cond_row(i * tm), 0, 0)),
            pl.BlockSpec((None, 1, D_MODEL), lambda i, j: (_cond_row(i * tm), 0, 1)),
            pl.BlockSpec((D_MODEL, tn), lambda i, j: (0, j)),
            pl.BlockSpec((1, DIFF_HEAD_DIM), lambda i, j: (0, 0)),
            pl.BlockSpec((1, DIFF_HEAD_DIM), lambda i, j: (0, 0)),
            pl.BlockSpec((tm, 128), lambda i, j: (_pos_block(i, tm), 0)),
            pl.BlockSpec((tm, 128), lambda i, j: (_pos_block(i, tm), 0)),
        ],
        out_specs=[
            pl.BlockSpec((tm, tn), lambda i, j: (i, jnp.minimum(j, nq - 1))),
            pl.BlockSpec((tm, tn), lambda i, j: (i, jnp.clip(j - nq, 0, nq - 1))),
            pl.BlockSpec((tm, tn), lambda i, j: (i, jnp.clip(j - 2 * nq, 0, nq - 1))),
        ],
        out_shape=[
            jax.ShapeDtypeStruct((TT, D_MODEL), BF16),
            jax.ShapeDtypeStruct((TT, D_MODEL), F32),
            jax.ShapeDtypeStruct((TT, D_MODEL), F32),
        ],
        scratch_shapes=[pltpu.VMEM((tm, D_MODEL), BF16)],
        compiler_params=_cparams(("arbitrary", "arbitrary")),
        name="diff_qkv",
    )(x, g, mod, mod, w_qkv, q_norm, k_norm, cos, sin)


def _diff_lambda(lq1_ref, lk1_ref, lq2_ref, lk2_ref, lam_init):
    e1 = jnp.exp(jnp.sum(lq1_ref[...] * lk1_ref[...], axis=-1, keepdims=True))
    e2 = jnp.exp(jnp.sum(lq2_ref[...] * lk2_ref[...], axis=-1, keepdims=True))
    return e1 - e2 + lam_init


def _softmax_rows(s):
    p = jnp.exp(s - jnp.max(s, axis=-1, keepdims=True))
    return p / jnp.sum(p, axis=-1, keepdims=True)


def _diff_attn_prompt_kernel(q_ref, k_ref, v_ref, lq1, lk1, lq2, lk2, on_ref, o_ref, *, heads, lam_init):
    lam = _diff_lambda(lq1, lk1, lq2, lk2, lam_init)
    for hh in range(heads):
        c0 = hh * 256
        p1 = _softmax_rows(_dot_nt(q_ref[:, c0:c0 + 128], k_ref[:, c0:c0 + 128].astype(BF16)))
        p2 = _softmax_rows(_dot_nt(q_ref[:, c0 + 128:c0 + 256], k_ref[:, c0 + 128:c0 + 256].astype(BF16)))
        a = (p1 - lam * p2).astype(BF16)
        o = _dot(a, v_ref[:, c0:c0 + 256].astype(BF16))
        o_ref[:, c0:c0 + 256] = (_rms(o, on_ref[...]) * (1.0 - lam_init)).astype(BF16)


def _vec_spec(n, nd):
    return pl.BlockSpec((1, n), (lambda *a: (0, 0)))


def _diff_attn_prompt_call(q, k, v, lams, out_norm, lam_init):
    heads = 4
    w = heads * 256
    blk = pl.BlockSpec((SEQ, w), lambda b, j: (b, j))
    return pl.pallas_call(
        functools.partial(_diff_attn_prompt_kernel, heads=heads, lam_init=lam_init),
        grid=(BATCH, DIFF_HEADS // heads),
        in_specs=[blk, blk, blk] + [_vec_spec(DIFF_HEAD_DIM, 2)] * 4 + [_vec_spec(DIFF_V_DIM, 2)],
        out_specs=blk,
        out_shape=jax.ShapeDtypeStruct((NP, D_MODEL), BF16),
        compiler_params=_cparams(("arbitrary", "arbitrary")),
        name="diff_attn_prompt",
    )(q, k, v, *lams, out_norm)


def _softmax2(sa, sb):
    m = jnp.maximum(jnp.max(sa, axis=-1, keepdims=True), jnp.max(sb, axis=-1, keepdims=True))
    pa = jnp.exp(sa - m)
    pb = jnp.exp(sb - m)
    l = jnp.sum(pa, axis=-1, keepdims=True) + jnp.sum(pb, axis=-1, keepdims=True)
    return pa / l, pb / l


def _diff_attn_latent_kernel(q_ref, kl_ref, kc_ref, vl_ref, vc_ref, lq1, lk1, lq2, lk2, on_ref,
                             o_ref, *, heads, lam_init):
    lam = _diff_lambda(lq1, lk1, lq2, lk2, lam_init)
    for hh in range(heads):
        c0 = hh * 256
        q1 = q_ref[:, c0:c0 + 128]
        q2 = q_ref[:, c0 + 128:c0 + 256]
        p1l, p1c = _softmax2(_dot_nt(q1, kl_ref[:, c0:c0 + 128].astype(BF16)),
                             _dot_nt(q1, kc_ref[:, c0:c0 + 128].astype(BF16)))
        p2l, p2c = _softmax2(_dot_nt(q2, kl_ref[:, c0 + 128:c0 + 256].astype(BF16)),
                             _dot_nt(q2, kc_ref[:, c0 + 128:c0 + 256].astype(BF16)))
        o = _dot((p1l - lam * p2l).astype(BF16), vl_ref[:, c0:c0 + 256].astype(BF16))
        o = o + _dot((p1c - lam * p2c).astype(BF16), vc_ref[:, c0:c0 + 256].astype(BF16))
        o_ref[:, c0:c0 + 256] = (_rms(o, on_ref[...]) * (1.0 - lam_init)).astype(BF16)


def _diff_attn_latent_call(q, k, v, k_ctx, v_ctx, lams, out_norm, lam_init):
    heads, tq = 4, 256
    w = heads * 256
    nq = DEC_SEQ // tq
    qrow = lambda b, j, t: (NP // tq + b * nq + t, j)
    lat = pl.BlockSpec((DEC_SEQ, w), lambda b, j, t: (NP // DEC_SEQ + b, j))
    ctx = pl.BlockSpec((PAST_LEN, w), lambda b, j, t: (b, j))
    return pl.pallas_call(
        functools.partial(_diff_attn_latent_kernel, heads=heads, lam_init=lam_init),
        grid=(DEC_BATCH, DIFF_HEADS // heads, nq),
        in_specs=[pl.BlockSpec((tq, w), qrow), lat, ctx, lat, ctx]
        + [_vec_spec(DIFF_HEAD_DIM, 3)] * 4 + [_vec_spec(DIFF_V_DIM, 3)],
        out_specs=pl.BlockSpec((tq, w), lambda b, j, t: (b * nq + t, j)),
        out_shape=jax.ShapeDtypeStruct((NS, D_MODEL), BF16),
        compiler_params=_cparams(("arbitrary", "arbitrary", "arbitrary")),
        name="diff_attn_latent",
    )(q, k, k_ctx, v, v_ctx, *lams, out_norm)


def _router_kernel(x_ref, g_ref, sh_ref, sc_ref, w_ref, b_ref, h_ref, ids_ref, gates_ref):
    h = _modulate(x_ref[...], g_ref[...], sh_ref[...], sc_ref[...])
    h_ref[...] = h
    h_hi = h.astype(BF16)
    h_lo = (h - h_hi.astype(F32)).astype(BF16)
    two = _dot(h_hi, w_ref[...])
    logits = two[:, :ROUTER_LANES] + two[:, ROUTER_LANES:] + _dot(h_lo, w_ref[:, :ROUTER_LANES]) + b_ref[...]

    lane = lax.broadcasted_iota(I32, logits.shape, 1).astype(F32)
    neg = -jnp.inf
    gl = jnp.where(lane < N_GROUPS, logits, neg)
    gmax = jnp.max(gl, axis=-1, keepdims=True)
    g_sel = jnp.min(jnp.where(gl == gmax, lane, ROUTER_LANES), axis=-1, keepdims=True)
    p_group = 1.0 / jnp.sum(jnp.exp(gl - gmax), axis=-1, keepdims=True)

    lo = N_GROUPS + g_sel * EXPERTS_PER_GROUP
    el = jnp.where(lane >= lo, jnp.where(lane < lo + EXPERTS_PER_GROUP, logits, neg), neg)
    v1 = jnp.max(el, axis=-1, keepdims=True)
    i1 = jnp.min(jnp.where(el == v1, lane, ROUTER_LANES), axis=-1, keepdims=True)
    el2 = jnp.where(lane == i1, neg, el)
    v2 = jnp.max(el2, axis=-1, keepdims=True)
    i2 = jnp.min(jnp.where(el2 == v2, lane, ROUTER_LANES), axis=-1, keepdims=True)
    e2 = jnp.exp(v2 - v1)
    w1 = p_group * (1.0 / (1.0 + e2))
    w2 = p_group * (e2 / (1.0 + e2))
    ids = jnp.where(lane == 0, i1 - N_GROUPS, jnp.where(lane == 1, i2 - N_GROUPS, 0.0))
    ids_ref[...] = ids.astype(I32)
    gates_ref[...] = jnp.where(lane == 0, w1, jnp.where(lane == 1, w2, 0.0))


def _router_call(x, g, mod, w_router2, b_router):
    tm = 256
    return pl.pallas_call(
        _router_kernel,
        grid=(TT // tm,),
        in_specs=[
            pl.BlockSpec((tm, D_MODEL), lambda i: (i, 0)),
            pl.BlockSpec((1, D_MODEL), lambda i: (0, 0)),
            _mod_spec(tm, 3),
            _mod_spec(tm, 4),
            pl.BlockSpec((D_MODEL, 2 * ROUTER_LANES), lambda i: (0, 0)),
            pl.BlockSpec((1, ROUTER_LANES), lambda i: (0, 0)),
        ],
        out_specs=[
            pl.BlockSpec((tm, D_MODEL), lambda i: (i, 0)),
            pl.BlockSpec((tm, ROUTER_LANES), lambda i: (i, 0)),
            pl.BlockSpec((tm, ROUTER_LANES), lambda i: (i, 0)),
        ],
        out_shape=[
            jax.ShapeDtypeStruct((TT, D_MODEL), F32),
            jax.ShapeDtypeStruct((TT, ROUTER_LANES), I32),
            jax.ShapeDtypeStruct((TT, ROUTER_LANES), F32),
        ],
        compiler_params=_cparams(("arbitrary",)),
        name="moe_router",
    )(x, g, mod, mod, w_router2, b_router)


def _row_block(b, nb_ref):
    return jnp.minimum(b, nb_ref[0] - 1)


def _row_copy(src_hbm, dst, src_row, dst_row, sem):
    return pltpu.make_async_copy(src_hbm.at[pl.ds(src_row, 1), :], dst.at[pl.ds(dst_row, 1), :], sem)


def _gather_kernel(tok_ref, nb_ref, src_hbm, o_ref, buf, sem):
    b = pl.program_id(0)

    @pl.when(b < nb_ref[0])
    def _():
        base = b * MOE_ROWS

        def start(r, c):
            _row_copy(src_hbm, buf, tok_ref[base + r], r, sem).start()
            return c

        def wait(r, c):
            _row_copy(src_hbm, buf, tok_ref[base + r], r, sem).wait()
            return c

        lax.fori_loop(0, MOE_ROWS, start, 0)
        lax.fori_loop(0, MOE_ROWS, wait, 0)
        o_ref[...] = buf[...].astype(BF16)

    @pl.when(b >= nb_ref[0])
    def _():
        o_ref[...] = jnp.zeros_like(o_ref)


def _gather_call(slot_tok, n_blocks, h):
    return pl.pallas_call(
        _gather_kernel,
        grid_spec=pltpu.PrefetchScalarGridSpec(
            num_scalar_prefetch=2,
            grid=(N_BLOCKS,),
            in_specs=[pl.BlockSpec(memory_space=pl.ANY)],
            out_specs=pl.BlockSpec((MOE_ROWS, D_MODEL), lambda b, tok, nb: (b, 0)),
            scratch_shapes=[pltpu.VMEM((MOE_ROWS, D_MODEL), F32), pltpu.SemaphoreType.DMA(())],
        ),
        out_shape=jax.ShapeDtypeStruct((N_SLOTS, D_MODEL), BF16),
        compiler_params=_cparams(("arbitrary",)),
        name="moe_gather",
    )(slot_tok, n_blocks, h)


def _gmm1_kernel(be_ref, first_ref, nb_ref, x_ref, wg_ref, wu_ref, o_ref, wg_s, wu_s):
    b = pl.program_id(1)

    @pl.when(first_ref[b] == 1)
    def _():
        wg_s[...] = wg_ref[...].astype(BF16)
        wu_s[...] = wu_ref[...].astype(BF16)

    @pl.when(b < nb_ref[0])
    def _():
        x = x_ref[...]
        gate = _dot(x, wg_s[...])
        up = _dot(x, wu_s[...])
        o_ref[...] = (gate * jax.nn.sigmoid(gate) * up).astype(BF16)

    @pl.when(b >= nb_ref[0])
    def _():
        o_ref[...] = jnp.zeros_like(o_ref)


def _gmm1_call(layer, block_e, first, n_blocks, xs, w_gate, w_up):
    hc = 256
    wspec = pl.BlockSpec((None, None, D_MODEL, hc), lambda j, b, be, fi, nb: (layer, be[b], 0, j))
    return pl.pallas_call(
        _gmm1_kernel,
        grid_spec=pltpu.PrefetchScalarGridSpec(
            num_scalar_prefetch=3,
            grid=(EXPERT_HIDDEN // hc, N_BLOCKS),
            in_specs=[
                pl.BlockSpec((MOE_ROWS, D_MODEL), lambda j, b, be, fi, nb: (_row_block(b, nb), 0)),
                wspec,
                wspec,
            ],
            out_specs=pl.BlockSpec((MOE_ROWS, hc), lambda j, b, be, fi, nb: (b, j)),
            scratch_shapes=[pltpu.VMEM((D_MODEL, hc), BF16), pltpu.VMEM((D_MODEL, hc), BF16)],
        ),
        out_shape=jax.ShapeDtypeStruct((N_SLOTS, EXPERT_HIDDEN), BF16),
        compiler_params=_cparams(("arbitrary", "arbitrary")),
        name="moe_gate_up",
    )(block_e, first, n_blocks, xs, w_gate, w_up)


def _gmm2_kernel(be_ref, first_ref, nb_ref, h_ref, wd_ref, g_ref, o_ref, wd_s):
    b = pl.program_id(1)

    @pl.when(first_ref[b] == 1)
    def _():
        wd_s[...] = wd_ref[...].astype(BF16)

    @pl.when(b < nb_ref[0])
    def _():
        o_ref[...] = _dot(h_ref[...], wd_s[...]) * g_ref[...]

    @pl.when(b >= nb_ref[0])
    def _():
        o_ref[...] = jnp.zeros_like(o_ref)


def _gmm2_call(layer, block_e, first, n_blocks, hid, w_down, slot_gate):
    nc = 1024
    return pl.pallas_call(
        _gmm2_kernel,
        grid_spec=pltpu.PrefetchScalarGridSpec(
            num_scalar_prefetch=3,
            grid=(D_MODEL // nc, N_BLOCKS),
            in_specs=[
                pl.BlockSpec((MOE_ROWS, EXPERT_HIDDEN), lambda j, b, be, fi, nb: (_row_block(b, nb), 0)),
                pl.BlockSpec((None, None, EXPERT_HIDDEN, nc), lambda j, b, be, fi, nb: (layer, be[b], 0, j)),
                pl.BlockSpec((MOE_ROWS, 1), lambda j, b, be, fi, nb: (_row_block(b, nb), 0)),
            ],
            out_specs=pl.BlockSpec((MOE_ROWS, nc), lambda j, b, be, fi, nb: (b, j)),
            scratch_shapes=[pltpu.VMEM((EXPERT_HIDDEN, nc), BF16)],
        ),
        out_shape=jax.ShapeDtypeStruct((N_SLOTS, D_MODEL), F32),
        compiler_params=_cparams(("arbitrary", "arbitrary")),
        name="moe_down",
    )(block_e, first, n_blocks, hid, w_down, slot_gate)


def _combine_kernel(pos_ref, ys_hbm, x_ref, gate_ref, o_ref, buf, sem, *, tm):
    base = pl.program_id(0) * (tm * TOP_K)

    def start(r, c):
        for k in range(TOP_K):
            _row_copy(ys_hbm, buf.at[k], pos_ref[base + TOP_K * r + k], r, sem).start()
        return c

    def wait(r, c):
        for k in range(TOP_K):
            _row_copy(ys_hbm, buf.at[k], pos_ref[base + TOP_K * r + k], r, sem).wait()
        return c

    lax.fori_loop(0, tm, start, 0)
    lax.fori_loop(0, tm, wait, 0)
    o_ref[...] = x_ref[...] + gate_ref[...] * (buf[0] + buf[1])


def _combine_call(pos, ys, x, mod):
    tm = 256
    return pl.pallas_call(
        functools.partial(_combine_kernel, tm=tm),
        grid_spec=pltpu.PrefetchScalarGridSpec(
            num_scalar_prefetch=1,
            grid=(TT // tm,),
            in_specs=[
                pl.BlockSpec(memory_space=pl.ANY),
                pl.BlockSpec((tm, D_MODEL), lambda i, pos: (i, 0)),
                pl.BlockSpec((None, 1, D_MODEL), lambda i, pos: (_cond_row(i * tm), 0, 5)),
            ],
            out_specs=pl.BlockSpec((tm, D_MODEL), lambda i, pos: (i, 0)),
            scratch_shapes=[pltpu.VMEM((TOP_K, tm, D_MODEL), F32), pltpu.SemaphoreType.DMA(())],
        ),
        out_shape=jax.ShapeDtypeStruct((TT, D_MODEL), F32),
        compiler_params=_cparams(("arbitrary",)),
        name="moe_combine",
    )(pos, ys, x, mod)


def _dispatch_plan(ids, gates):
    flat_e = ids[:, :TOP_K].reshape(N_ASSIGN)
    flat_g = gates[:, :TOP_K].reshape(N_ASSIGN)
    onehot = (flat_e[:, None] == jnp.arange(N_EXPERTS, dtype=I32)[None, :]).astype(I32)
    csum = jnp.cumsum(onehot, axis=0)
    rank = jnp.take_along_axis(csum, flat_e[:, None], axis=1)[:, 0] - 1
    counts = csum[-1]
    padded = (counts + MOE_ROWS - 1) // MOE_ROWS * MOE_ROWS
    pad_ends = jnp.cumsum(padded)
    pad_starts = pad_ends - padded
    dest = (pad_starts[flat_e] + rank).astype(I32)
    slot_tok = jnp.zeros((N_SLOTS,), I32).at[dest].set(jnp.arange(N_ASSIGN, dtype=I32) // TOP_K)
    slot_gate = jnp.zeros((N_SLOTS,), F32).at[dest].set(flat_g)
    n_blocks = (pad_ends[-1] // MOE_ROWS).astype(I32)
    blk = jnp.minimum(jnp.arange(N_BLOCKS, dtype=I32), n_blocks - 1)
    block_e = jnp.minimum(jnp.searchsorted(pad_ends, blk * MOE_ROWS, side="right"), N_EXPERTS - 1).astype(I32)
    first = jnp.concatenate([jnp.ones((1,), I32), (block_e[1:] != block_e[:-1]).astype(I32)])
    return dest, slot_tok, slot_gate, block_e, first, n_blocks.reshape(1)


def _moe(layer, x, g, mod, w_router2, b_router, w_gate, w_up, w_down):
    h, ids, gates = _router_call(x, g, mod, w_router2, b_router)
    dest, slot_tok, slot_gate, block_e, first, n_blocks = _dispatch_plan(ids, gates)
    xs = _gather_call(slot_tok, n_blocks, h)
    hid = _gmm1_call(layer, block_e, first, n_blocks, xs, w_gate, w_up)
    ys = _gmm2_call(layer, block_e, first, n_blocks, hid, w_down, slot_gate.reshape(N_SLOTS, 1))
    return _combine_call(dest, ys, x, mod)


def _rope_tables(rot_dim):
    rows = DEC_SEQ // GRID_W
    nf = rot_dim // 4
    inv = jnp.exp(-math.log(ROPE_THETA) * jnp.arange(nf, dtype=F32) / nf)
    row_pos = jnp.repeat(jnp.arange(rows, dtype=F32), GRID_W)
    col_pos = jnp.tile(jnp.arange(GRID_W, dtype=F32), rows)
    ang = jnp.stack([row_pos[:, None] * inv, col_pos[:, None] * inv], axis=1)
    cos = jnp.broadcast_to(jnp.cos(ang)[:, :, None, :], (DEC_SEQ, 2, 2, nf)).reshape(DEC_SEQ, rot_dim)
    sgn = jnp.array([-1.0, 1.0], F32)[None, None, :, None]
    sin = (jnp.sin(ang)[:, :, None, :] * sgn).reshape(DEC_SEQ, rot_dim)
    pad = 128 - rot_dim
    if pad:
        cos = jnp.concatenate([cos, jnp.ones((DEC_SEQ, pad), F32)], axis=1)
        sin = jnp.concatenate([sin, jnp.zeros((DEC_SEQ, pad), F32)], axis=1)
    return cos, sin


def _pad_head_vec(v):
    return jnp.concatenate([v, jnp.zeros((MLA_HEAD_PAD - MLA_QK,), F32)]).reshape(1, MLA_HEAD_PAD)


def kernel(x_prompt, x_sample, cache_mla_ckv, cache_mla_krope, cache_diff_k, cache_diff_v, c, c_ctx, norm1_g, norm2_g, ada_w, ada_b, mla_w_in, mla_q_a_norm, mla_w_q_up, mla_kv_a_norm, mla_w_kv_up, mla_q_norm, mla_k_norm, mla_w_out, diff_w_qkv, diff_q_norm, diff_k_norm, diff_lambda_q1, diff_lambda_k1, diff_lambda_q2, diff_lambda_k2, diff_out_norm, diff_w_out, moe_w_group, moe_b_group, moe_w_router, moe_b_router, moe_w_gate, moe_w_up, moe_w_down):
    x = jnp.concatenate([x_prompt.reshape(NP, D_MODEL), x_sample.reshape(NS, D_MODEL)], axis=0)
    cond = jnp.concatenate([c_ctx[None, :], c, jnp.zeros((N_COND - 1 - DEC_BATCH, D_MODEL), F32)], axis=0)
    mods = _ada_call(cond, ada_w, ada_b).reshape(DEPTH, N_COND, 1, 6 * D_MODEL)

    new_ckv = new_krope = new_k = new_v = None
    for i in range(DEPTH):
        mod = mods[i]
        j = i // 2
        g1 = norm1_g[i].reshape(1, D_MODEL)
        if i % 2 == 0:
            cos, sin = _rope_tables(MLA_ROPE)
            w_in = jnp.pad(mla_w_in[j], ((0, 0), (0, MLA_IN_PAD - MLA_IN))).astype(BF16)
            w_q = jnp.pad(mla_w_q_up[j].reshape(MLA_Q_LORA, MLA_HEADS, MLA_QK),
                          ((0, 0), (0, 0), (0, MLA_HEAD_PAD - MLA_QK)))
            w_q = w_q.reshape(MLA_Q_LORA, MLA_HEADS * MLA_HEAD_PAD).astype(BF16)
            cq, ckv, kr = _mla_in_call(x, g1, mod, w_in, mla_q_a_norm[j].reshape(1, -1),
                                       mla_kv_a_norm[j].reshape(1, -1))
            q = _q_up_call(cq, w_q, _pad_head_vec(mla_q_norm[j]), cos, sin)
            ckv_all = jnp.concatenate([ckv, cache_mla_ckv[:, j].reshape(NCTX, MLA_KV_LORA)], axis=0)
            kr_ctx = jnp.pad(cache_mla_krope[:, j].reshape(NCTX, MLA_ROPE), ((0, 0), (0, 128 - MLA_ROPE)))
            kr_all = jnp.concatenate([kr, kr_ctx], axis=0)
            k, v = _kv_up_call(ckv_all, kr_all, mla_w_kv_up[j].astype(BF16), _pad_head_vec(mla_k_norm[j]),
                               cos, sin)
            o_p = _mla_attn_prompt_call(q, k, v)
            o_s = _mla_attn_latent_call(q, k, v)
            x = _out_proj_call(o_p, o_s, mla_w_out[j].astype(BF16), x, mod)
            new_ckv = ckv[:NP].reshape(BATCH, 1, SEQ, MLA_KV_LORA)
            new_krope = kr[:NP, :MLA_ROPE].reshape(BATCH, 1, SEQ, MLA_ROPE)
        else:
            lam_init = 0.8 - 0.6 * math.exp(-0.3 * i)
            cos, sin = _rope_tables(DIFF_HEAD_DIM)
            q, k, v = _diff_qkv_call(x, g1, mod, diff_w_qkv[j].astype(BF16), diff_q_norm[j].reshape(1, -1),
                                     diff_k_norm[j].reshape(1, -1), cos, sin)
            lams = [t[j].reshape(1, DIFF_HEAD_DIM) for t in
                    (diff_lambda_q1, diff_lambda_k1, diff_lambda_q2, diff_lambda_k2)]
            out_norm = diff_out_norm[j].reshape(1, DIFF_V_DIM)
            o_p = _diff_attn_prompt_call(q, k, v, lams, out_norm, lam_init)
            o_s = _diff_attn_latent_call(q, k, v, cache_diff_k[:, j].reshape(NCTX, D_MODEL),
                                         cache_diff_v[:, j].reshape(NCTX, D_MODEL), lams, out_norm, lam_init)
            x = _out_proj_call(o_p, o_s, diff_w_out[j].astype(BF16), x, mod)
            new_k = k[:NP].reshape(BATCH, 1, SEQ, DIFF_HEADS, 2, DIFF_HEAD_DIM)
            new_v = v[:NP].reshape(BATCH, 1, SEQ, DIFF_HEADS, DIFF_V_DIM)

        w_r = jnp.concatenate([moe_w_group[i], moe_w_router[i],
                               jnp.zeros((D_MODEL, ROUTER_LANES - N_GROUPS - N_EXPERTS), F32)], axis=1)
        w_r_hi = w_r.astype(BF16)
        w_r_lo = (w_r - w_r_hi.astype(F32)).astype(BF16)
        b_r = jnp.concatenate([moe_b_group[i], moe_b_router[i],
                               jnp.zeros((ROUTER_LANES - N_GROUPS - N_EXPERTS,), F32)]).reshape(1, ROUTER_LANES)
        x = _moe(i, x, norm2_g[i].reshape(1, D_MODEL), mod, jnp.concatenate([w_r_hi, w_r_lo], axis=1), b_r,
                 moe_w_gate, moe_w_up, moe_w_down)

    return (x[:NP].reshape(BATCH, SEQ, D_MODEL), x[NP:].reshape(DEC_BATCH, DEC_SEQ, D_MODEL),
            new_ckv, new_krope, new_k, new_v)
```

```python
import functools
import math

import jax
import jax.numpy as jnp
from jax import lax
from jax.experimental import pallas as pl
from jax.experimental.pallas import tpu as pltpu

F32 = jnp.float32
BF16 = jnp.bfloat16
I32 = jnp.int32
U32 = jnp.uint32

D_MODEL = 4096
BATCH = 32
SEQ = 256
DEPTH = 2
DEC_BATCH = 4
DEC_SEQ = 1024
PAST_LEN = 256
GRID_W = 64
MLA_HEADS = 32
MLA_NOPE = 128
MLA_ROPE = 64
MLA_QK = MLA_NOPE + MLA_ROPE
MLA_V = 128
MLA_Q_LORA = 1024
MLA_KV_LORA = 512
MLA_IN = MLA_Q_LORA + MLA_KV_LORA + MLA_ROPE
MLA_IN_PAD = MLA_Q_LORA + MLA_KV_LORA + 128
MLA_HEAD_PAD = 256
DIFF_HEAD_DIM = 128
DIFF_HEADS = 16
DIFF_V_DIM = 256
N_GROUPS = 8
EXPERTS_PER_GROUP = 8
N_EXPERTS = 64
TOP_K = 2
EXPERT_HIDDEN = 1024
ROPE_THETA = 10000.0
NORM_EPS = 1e-6

NP = BATCH * SEQ
NS = DEC_BATCH * DEC_SEQ
TT = NP + NS
NCTX = DEC_BATCH * PAST_LEN
N_COND = 8
N_ASSIGN = TT * TOP_K
ROUTER_LANES = 128
HALF_D = D_MODEL // 2

MOE_ROWS = 512
MOE_HC = 256
MOE_NC = 1024
MOE_NJ = EXPERT_HIDDEN // MOE_HC
MOE_STEPS = MOE_NJ + D_MODEL // MOE_NC
MOE_PREFETCH = MOE_ROWS // MOE_STEPS
N_BLOCKS = N_ASSIGN // MOE_ROWS + N_EXPERTS + 1
N_SLOTS = N_BLOCKS * MOE_ROWS

VMEM_LIMIT = 56 * 1024 * 1024


def _cparams(sem):
    return pltpu.CompilerParams(dimension_semantics=sem, vmem_limit_bytes=VMEM_LIMIT)


def _cond_row(row0):
    return jnp.where(row0 < NP, 0, 1 + (row0 - NP) // DEC_SEQ)


def _rms(x, g):
    return x * lax.rsqrt(jnp.mean(x * x, axis=-1, keepdims=True) + NORM_EPS) * g


def _modulate(x, g, shift, scale):
    return _rms(x, g) * (1.0 + scale) + shift


def _dot(a, b):
    return jnp.dot(a, b, preferred_element_type=F32)


def _dot_nt(a, b):
    return lax.dot_general(a, b, (((1,), (1,)), ((), ())), preferred_element_type=F32)


def _rope128(x, cos, sin_signed, half):
    lane = lax.broadcasted_iota(I32, x.shape, 1)
    first_half = ((lane // half) % 2) == 0
    partner = jnp.where(first_half, pltpu.roll(x, 128 - half, 1), pltpu.roll(x, half, 1))
    return x * cos + partner * sin_signed


def _latent_tables(cos_ref, sin_ref, is_latent):
    m = is_latent.astype(F32)
    return 1.0 + m * (cos_ref[...] - 1.0), m * sin_ref[...]


def _row_pair_specs(tm, tn, split):
    npt = NP // tm
    off = npt if split else 0
    return (pl.BlockSpec((tm, tn), lambda i, *a: (jnp.minimum(i, npt - 1), a[0] if tn != D_MODEL else 0)),
            pl.BlockSpec((tm, tn), lambda i, *a: (jnp.maximum(i, npt) - off, a[0] if tn != D_MODEL else 0)))


def _ada_kernel(c_ref, w_ref, b_ref, o_ref):
    c = c_ref[...]
    a = (c * jax.nn.sigmoid(c)).astype(BF16)
    o_ref[...] = _dot(a, w_ref[...].astype(BF16)) + b_ref[...]


def _ada_call(cond, ada_w, ada_b):
    tn = 512
    n = 6 * D_MODEL
    return pl.pallas_call(
        _ada_kernel,
        grid=(DEPTH, n // tn),
        in_specs=[
            pl.BlockSpec((N_COND, D_MODEL), lambda l, j: (0, 0)),
            pl.BlockSpec((None, D_MODEL, tn), lambda l, j: (l, 0, j)),
            pl.BlockSpec((None, 1, tn), lambda l, j: (l, 0, j)),
        ],
        out_specs=pl.BlockSpec((None, N_COND, tn), lambda l, j: (l, 0, j)),
        out_shape=jax.ShapeDtypeStruct((DEPTH, N_COND, n), F32),
        compiler_params=_cparams(("arbitrary", "arbitrary")),
        name="ada_table",
    )(cond, ada_w, ada_b.reshape(DEPTH, 1, n))


def _mod_spec(tm, chunk, width=D_MODEL):
    per = D_MODEL // width
    return pl.BlockSpec((None, 1, width),
                        lambda i, *a: (_cond_row(i * tm), 0, chunk * per + (a[0] if per > 1 else 0)))


def _mla_in_kernel(xp_ref, xs_ref, g_ref, sh_ref, sc_ref, w_ref, qg_ref, kvg_ref, cq_ref, ckv_ref, kr_ref, *, tm):
    def run(x_ref):
        h = _modulate(x_ref[...], g_ref[...], sh_ref[...], sc_ref[...]).astype(BF16)
        p = _dot(h, w_ref[...])
        cq_ref[...] = _rms(p[:, :MLA_Q_LORA], qg_ref[...]).astype(BF16)
        ckv_ref[...] = _rms(p[:, MLA_Q_LORA:MLA_Q_LORA + MLA_KV_LORA], kvg_ref[...])
        kr_ref[...] = p[:, MLA_Q_LORA + MLA_KV_LORA:]

    is_prompt = pl.program_id(0) < NP // tm
    pl.when(is_prompt)(lambda: run(xp_ref))
    pl.when(jnp.logical_not(is_prompt))(lambda: run(xs_ref))


def _mla_in_call(x_prompt, x_latent, g, mod, w_in, q_a_norm, kv_a_norm):
    tm = 256
    xp_spec, xs_spec = _row_pair_specs(tm, D_MODEL, split=True)
    return pl.pallas_call(
        functools.partial(_mla_in_kernel, tm=tm),
        grid=(TT // tm,),
        in_specs=[
            xp_spec,
            xs_spec,
            pl.BlockSpec((1, D_MODEL), lambda i: (0, 0)),
            _mod_spec(tm, 0),
            _mod_spec(tm, 1),
            pl.BlockSpec((D_MODEL, MLA_IN_PAD), lambda i: (0, 0)),
            pl.BlockSpec((1, MLA_Q_LORA), lambda i: (0, 0)),
            pl.BlockSpec((1, MLA_KV_LORA), lambda i: (0, 0)),
        ],
        out_specs=[
            pl.BlockSpec((tm, MLA_Q_LORA), lambda i: (i, 0)),
            pl.BlockSpec((tm, MLA_KV_LORA), lambda i: (i, 0)),
            pl.BlockSpec((tm, 128), lambda i: (i, 0)),
        ],
        out_shape=[
            jax.ShapeDtypeStruct((TT, MLA_Q_LORA), BF16),
            jax.ShapeDtypeStruct((TT, MLA_KV_LORA), F32),
            jax.ShapeDtypeStruct((TT, 128), F32),
        ],
        compiler_params=_cparams(("arbitrary",)),
        name="mla_in",
    )(x_prompt, x_latent, g, mod, mod, w_in, q_a_norm, kv_a_norm)


def _q_up_kernel(c_ref, w_ref, g_ref, cos_ref, sin_ref, q_ref, *, tm, heads):
    i = pl.program_id(0)
    cos, sin = _latent_tables(cos_ref, sin_ref, i >= NP // tm)
    o = _dot(c_ref[...], w_ref[...])
    g = g_ref[...] * (MLA_QK ** -0.5)
    for hh in range(heads):
        s = o[:, hh * 256:(hh + 1) * 256]
        r = lax.rsqrt(jnp.sum(s * s, axis=-1, keepdims=True) / MLA_QK + NORM_EPS)
        lo = s[:, :128] * r * g[:, :128]
        hi = _rope128(s[:, 128:] * r * g[:, 128:], cos, sin, MLA_ROPE // 4)
        q_ref[:, hh * 256:hh * 256 + 128] = lo.astype(BF16)
        q_ref[:, hh * 256 + 128:(hh + 1) * 256] = hi.astype(BF16)


def _pos_block(i, tm):
    return jnp.clip(i - NP // tm, 0, NS // tm - 1) % (DEC_SEQ // tm)


def _q_up_call(cq, w_q, q_norm_pad, cos, sin):
    tm, heads = 512, 8
    tn = heads * MLA_HEAD_PAD
    return pl.pallas_call(
        functools.partial(_q_up_kernel, tm=tm, heads=heads),
        grid=(TT // tm, MLA_HEADS // heads),
        in_specs=[
            pl.BlockSpec((tm, MLA_Q_LORA), lambda i, j: (i, 0)),
            pl.BlockSpec((MLA_Q_LORA, tn), lambda i, j: (0, j)),
            pl.BlockSpec((1, MLA_HEAD_PAD), lambda i, j: (0, 0)),
            pl.BlockSpec((tm, 128), lambda i, j: (_pos_block(i, tm), 0)),
            pl.BlockSpec((tm, 128), lambda i, j: (_pos_block(i, tm), 0)),
        ],
        out_specs=pl.BlockSpec((tm, tn), lambda i, j: (i, j)),
        out_shape=jax.ShapeDtypeStruct((TT, MLA_HEADS * MLA_HEAD_PAD), BF16),
        compiler_params=_cparams(("arbitrary", "arbitrary")),
        name="mla_q_up",
    )(cq, w_q, q_norm_pad, cos, sin)


def _kv_up_kernel(c_ref, kr_ref, w_ref, g_ref, cos_ref, sin_ref, k_ref, v_ref, *, tm, heads):
    i = pl.program_id(0)
    cos, sin = _latent_tables(cos_ref, sin_ref, (i >= NP // tm) & (i < TT // tm))
    kv = _dot(c_ref[...].astype(BF16), w_ref[...])
    g = g_ref[...]
    kr = kr_ref[...]
    ss_rope = jnp.sum(kr * kr, axis=-1, keepdims=True)
    base = _rope128(kr * g[:, 128:], cos, sin, MLA_ROPE // 4)
    for hh in range(heads):
        kn = kv[:, hh * 256:hh * 256 + 128]
        r = lax.rsqrt((jnp.sum(kn * kn, axis=-1, keepdims=True) + ss_rope) / MLA_QK + NORM_EPS)
        k_ref[:, hh * 256:hh * 256 + 128] = (kn * r * g[:, :128]).astype(BF16)
        k_ref[:, hh * 256 + 128:(hh + 1) * 256] = (base * r).astype(BF16)
        v_ref[:, hh * 128:(hh + 1) * 128] = kv[:, hh * 256 + 128:(hh + 1) * 256].astype(BF16)


def _kv_up_call(ckv_all, kr_all, w_kv, k_norm_pad, cos, sin):
    tm, heads = 512, 8
    rows = TT + NCTX
    return pl.pallas_call(
        functools.partial(_kv_up_kernel, tm=tm, heads=heads),
        grid=(rows // tm, MLA_HEADS // heads),
        in_specs=[
            pl.BlockSpec((tm, MLA_KV_LORA), lambda i, j: (i, 0)),
            pl.BlockSpec((tm, 128), lambda i, j: (i, 0)),
            pl.BlockSpec((MLA_KV_LORA, heads * 256), lambda i, j: (0, j)),
            pl.BlockSpec((1, MLA_HEAD_PAD), lambda i, j: (0, 0)),
            pl.BlockSpec((tm, 128), lambda i, j: (_pos_block(i, tm), 0)),
            pl.BlockSpec((tm, 128), lambda i, j: (_pos_block(i, tm), 0)),
        ],
        out_specs=[
            pl.BlockSpec((tm, heads * MLA_HEAD_PAD), lambda i, j: (i, j)),
            pl.BlockSpec((tm, heads * MLA_V), lambda i, j: (i, j)),
        ],
        out_shape=[
            jax.ShapeDtypeStruct((rows, MLA_HEADS * MLA_HEAD_PAD), BF16),
            jax.ShapeDtypeStruct((rows, MLA_HEADS * MLA_V), BF16),
        ],
        compiler_params=_cparams(("arbitrary", "arbitrary")),
        name="mla_kv_up",
    )(ckv_all, kr_all, w_kv, k_norm_pad, cos, sin)


def _mla_attn_prompt_kernel(q_ref, k_ref, v_ref, o_ref, *, heads):
    for hh in range(heads):
        q = q_ref[:, hh * 256:(hh + 1) * 256]
        k = k_ref[:, hh * 256:(hh + 1) * 256]
        s = _dot_nt(q, k)
        p = jnp.exp(s - jnp.max(s, axis=-1, keepdims=True))
        l = jnp.sum(p, axis=-1, keepdims=True)
        o = _dot(p.astype(BF16), v_ref[:, hh * 128:(hh + 1) * 128]) / l
        o_ref[:, hh * 128:(hh + 1) * 128] = o.astype(BF16)


def _mla_attn_prompt_call(q, k, v):
    heads = 8
    return pl.pallas_call(
        functools.partial(_mla_attn_prompt_kernel, heads=heads),
        grid=(BATCH, MLA_HEADS // heads),
        in_specs=[
            pl.BlockSpec((SEQ, heads * 256), lambda b, j: (b, j)),
            pl.BlockSpec((SEQ, heads * 256), lambda b, j: (b, j)),
            pl.BlockSpec((SEQ, heads * 128), lambda b, j: (b, j)),
        ],
        out_specs=pl.BlockSpec((SEQ, heads * 128), lambda b, j: (b, j)),
        out_shape=jax.ShapeDtypeStruct((NP, MLA_HEADS * MLA_V), BF16),
        compiler_params=_cparams(("arbitrary", "arbitrary")),
        name="mla_attn_prompt",
    )(q, k, v)


def _mla_attn_latent_kernel(q_ref, kl_ref, kc_ref, vl_ref, vc_ref, o_ref, *, heads):
    for hh in range(heads):
        q = q_ref[:, hh * 256:(hh + 1) * 256]
        sl = _dot_nt(q, kl_ref[:, hh * 256:(hh + 1) * 256])
        sc = _dot_nt(q, kc_ref[:, hh * 256:(hh + 1) * 256])
        m = jnp.maximum(jnp.max(sl, axis=-1, keepdims=True), jnp.max(sc, axis=-1, keepdims=True))
        pl_ = jnp.exp(sl - m)
        pc = jnp.exp(sc - m)
        l = jnp.sum(pl_, axis=-1, keepdims=True) + jnp.sum(pc, axis=-1, keepdims=True)
        o = _dot(pl_.astype(BF16), vl_ref[:, hh * 128:(hh + 1) * 128])
        o = o + _dot(pc.astype(BF16), vc_ref[:, hh * 128:(hh + 1) * 128])
        o_ref[:, hh * 128:(hh + 1) * 128] = (o / l).astype(BF16)


def _mla_attn_latent_call(q, k, v):
    heads, tq = 8, 512
    nq = DEC_SEQ // tq
    return pl.pallas_call(
        functools.partial(_mla_attn_latent_kernel, heads=heads),
        grid=(DEC_BATCH, MLA_HEADS // heads, nq),
        in_specs=[
            pl.BlockSpec((tq, heads * 256), lambda b, j, t: (NP // tq + b * nq + t, j)),
            pl.BlockSpec((DEC_SEQ, heads * 256), lambda b, j, t: (NP // DEC_SEQ + b, j)),
            pl.BlockSpec((PAST_LEN, heads * 256), lambda b, j, t: (TT // PAST_LEN + b, j)),
            pl.BlockSpec((DEC_SEQ, heads * 128), lambda b, j, t: (NP // DEC_SEQ + b, j)),
            pl.BlockSpec((PAST_LEN, heads * 128), lambda b, j, t: (TT // PAST_LEN + b, j)),
        ],
        out_specs=pl.BlockSpec((tq, heads * 128), lambda b, j, t: (b * nq + t, j)),
        out_shape=jax.ShapeDtypeStruct((NS, MLA_HEADS * MLA_V), BF16),
        compiler_params=_cparams(("arbitrary", "arbitrary", "arbitrary")),
        name="mla_attn_latent",
    )(q, k, k, v, v)


def _out_proj_kernel(ap_ref, as_ref, w_ref, xp_ref, xs_ref, gate_ref, o_ref, *, tm):
    def finish(a_ref, x_ref):
        o_ref[...] = x_ref[...] + gate_ref[...] * _dot(a_ref[...], w_ref[...])

    is_prompt = pl.program_id(0) < NP // tm
    pl.when(is_prompt)(lambda: finish(ap_ref, xp_ref))
    pl.when(jnp.logical_not(is_prompt))(lambda: finish(as_ref, xs_ref))


def _out_proj_call(a_prompt, a_latent, w, x_prompt, x_latent, x_split, mod):
    tm, tn = 512, 1024
    ap_spec, as_spec = _row_pair_specs(tm, D_MODEL, split=True)
    xp_spec, xs_spec = _row_pair_specs(tm, tn, split=x_split)
    return pl.pallas_call(
        functools.partial(_out_proj_kernel, tm=tm),
        grid=(TT // tm, D_MODEL // tn),
        in_specs=[
            ap_spec,
            as_spec,
            pl.BlockSpec((D_MODEL, tn), lambda i, j: (0, j)),
            xp_spec,
            xs_spec,
            _mod_spec(tm, 2, width=tn),
        ],
        out_specs=pl.BlockSpec((tm, tn), lambda i, j: (i, j)),
        out_shape=jax.ShapeDtypeStruct((TT, D_MODEL), F32),
        compiler_params=_cparams(("arbitrary", "arbitrary")),
        name="attn_out_proj",
    )(a_prompt, a_latent, w, x_prompt, x_latent, mod)


def _diff_qkv_kernel(h_ref, w_ref, qn_ref, kn_ref, cos_ref, sin_ref, q_ref, k_ref, v_ref, *, tm, nq, sub):
    i = pl.program_id(0)
    j = pl.program_id(1)
    cos, sin = _latent_tables(cos_ref, sin_ref, i >= NP // tm)
    n_sub = w_ref.shape[1] // sub

    def norm_rope(o, g, scale):
        return [_rope128(_rms(o[:, s * 128:(s + 1) * 128], g) * scale, cos, sin, DIFF_HEAD_DIM // 4)
                for s in range(sub // 128)]

    @pl.when(j < nq)
    def _():
        for t in range(n_sub):
            o = _dot(h_ref[...], w_ref[:, t * sub:(t + 1) * sub])
            for s, y in enumerate(norm_rope(o, qn_ref[...], DIFF_HEAD_DIM ** -0.5)):
                q_ref[:, t * sub + s * 128:t * sub + (s + 1) * 128] = y.astype(BF16)

    @pl.when((j >= nq) & (j < 2 * nq))
    def _():
        for t in range(n_sub):
            o = _dot(h_ref[...], w_ref[:, t * sub:(t + 1) * sub])
            for s, y in enumerate(norm_rope(o, kn_ref[...], 1.0)):
                k_ref[:, t * sub + s * 128:t * sub + (s + 1) * 128] = y

    @pl.when(j >= 2 * nq)
    def _():
        v_ref[...] = _dot(h_ref[...], w_ref[...])


def _diff_qkv_call(h, w_qkv, q_norm, k_norm, cos, sin):
    tm, tn, sub = 512, 1024, 512
    nq = D_MODEL // tn
    return pl.pallas_call(
        functools.partial(_diff_qkv_kernel, tm=tm, nq=nq, sub=sub),
        grid=(TT // tm, 3 * nq),
        in_specs=[
            pl.BlockSpec((tm, D_MODEL), lambda i, j: (i, 0)),
            pl.BlockSpec((D_MODEL, tn), lambda i, j: (0, j)),
            pl.BlockSpec((1, DIFF_HEAD_DIM), lambda i, j: (0, 0)),
            pl.BlockSpec((1, DIFF_HEAD_DIM), lambda i, j: (0, 0)),
            pl.BlockSpec((tm, 128), lambda i, j: (_pos_block(i, tm), 0)),
            pl.BlockSpec((tm, 128), lambda i, j: (_pos_block(i, tm), 0)),
        ],
        out_specs=[
            pl.BlockSpec((tm, tn), lambda i, j: (i, jnp.minimum(j, nq - 1))),
            pl.BlockSpec((tm, tn), lambda i, j: (i, jnp.clip(j - nq, 0, nq - 1))),
            pl.BlockSpec((tm, tn), lambda i, j: (i, jnp.clip(j - 2 * nq, 0, nq - 1))),
        ],
        out_shape=[
            jax.ShapeDtypeStruct((TT, D_MODEL), BF16),
            jax.ShapeDtypeStruct((TT, D_MODEL), F32),
            jax.ShapeDtypeStruct((TT, D_MODEL), F32),
        ],
        compiler_params=_cparams(("arbitrary", "arbitrary")),
        name="diff_qkv",
    )(h, w_qkv, q_norm, k_norm, cos, sin)


def _diff_lambda(lq1_ref, lk1_ref, lq2_ref, lk2_ref, lam_init):
    e1 = jnp.exp(jnp.sum(lq1_ref[...] * lk1_ref[...], axis=-1, keepdims=True))
    e2 = jnp.exp(jnp.sum(lq2_ref[...] * lk2_ref[...], axis=-1, keepdims=True))
    return e1 - e2 + lam_init


def _softmax_rows(s):
    p = jnp.exp(s - jnp.max(s, axis=-1, keepdims=True))
    return p / jnp.sum(p, axis=-1, keepdims=True)


def _diff_attn_prompt_kernel(q_ref, k_ref, v_ref, lq1, lk1, lq2, lk2, on_ref, o_ref, *, heads, lam_init):
    lam = _diff_lambda(lq1, lk1, lq2, lk2, lam_init)
    for hh in range(heads):
        c0 = hh * 256
        p1 = _softmax_rows(_dot_nt(q_ref[:, c0:c0 + 128], k_ref[:, c0:c0 + 128].astype(BF16)))
        p2 = _softmax_rows(_dot_nt(q_ref[:, c0 + 128:c0 + 256], k_ref[:, c0 + 128:c0 + 256].astype(BF16)))
        a = (p1 - lam * p2).astype(BF16)
        o = _dot(a, v_ref[:, c0:c0 + 256].astype(BF16))
        o_ref[:, c0:c0 + 256] = (_rms(o, on_ref[...]) * (1.0 - lam_init)).astype(BF16)


def _vec_spec(n):
    return pl.BlockSpec((1, n), lambda *a: (0, 0))


def _diff_attn_prompt_call(q, k, v, lams, out_norm, lam_init):
    heads = 4
    w = heads * 256
    blk = pl.BlockSpec((SEQ, w), lambda b, j: (b, j))
    return pl.pallas_call(
        functools.partial(_diff_attn_prompt_kernel, heads=heads, lam_init=lam_init),
        grid=(BATCH, DIFF_HEADS // heads),
        in_specs=[blk, blk, blk] + [_vec_spec(DIFF_HEAD_DIM)] * 4 + [_vec_spec(DIFF_V_DIM)],
        out_specs=blk,
        out_shape=jax.ShapeDtypeStruct((NP, D_MODEL), BF16),
        compiler_params=_cparams(("arbitrary", "arbitrary")),
        name="diff_attn_prompt",
    )(q, k, v, *lams, out_norm)


def _softmax2(sa, sb):
    m = jnp.maximum(jnp.max(sa, axis=-1, keepdims=True), jnp.max(sb, axis=-1, keepdims=True))
    pa = jnp.exp(sa - m)
    pb = jnp.exp(sb - m)
    l = jnp.sum(pa, axis=-1, keepdims=True) + jnp.sum(pb, axis=-1, keepdims=True)
    return pa / l, pb / l


def _diff_attn_latent_kernel(q_ref, kl_ref, kc_ref, vl_ref, vc_ref, lq1, lk1, lq2, lk2, on_ref,
                             o_ref, *, heads, lam_init):
    lam = _diff_lambda(lq1, lk1, lq2, lk2, lam_init)
    for hh in range(heads):
        c0 = hh * 256
        q1 = q_ref[:, c0:c0 + 128]
        q2 = q_ref[:, c0 + 128:c0 + 256]
        p1l, p1c = _softmax2(_dot_nt(q1, kl_ref[:, c0:c0 + 128].astype(BF16)),
                             _dot_nt(q1, kc_ref[:, c0:c0 + 128].astype(BF16)))
        p2l, p2c = _softmax2(_dot_nt(q2, kl_ref[:, c0 + 128:c0 + 256].astype(BF16)),
                             _dot_nt(q2, kc_ref[:, c0 + 128:c0 + 256].astype(BF16)))
        o = _dot((p1l - lam * p2l).astype(BF16), vl_ref[:, c0:c0 + 256].astype(BF16))
        o = o + _dot((p1c - lam * p2c).astype(BF16), vc_ref[:, c0:c0 + 256].astype(BF16))
        o_ref[:, c0:c0 + 256] = (_rms(o, on_ref[...]) * (1.0 - lam_init)).astype(BF16)


def _diff_attn_latent_call(q, k, v, k_ctx, v_ctx, lams, out_norm, lam_init):
    heads, tq = 4, 256
    w = heads * 256
    nq = DEC_SEQ // tq
    lat = pl.BlockSpec((DEC_SEQ, w), lambda b, j, t: (NP // DEC_SEQ + b, j))
    ctx = pl.BlockSpec((PAST_LEN, w), lambda b, j, t: (b, j))
    return pl.pallas_call(
        functools.partial(_diff_attn_latent_kernel, heads=heads, lam_init=lam_init),
        grid=(DEC_BATCH, DIFF_HEADS // heads, nq),
        in_specs=[pl.BlockSpec((tq, w), lambda b, j, t: (NP // tq + b * nq + t, j)), lat, ctx, lat, ctx]
        + [_vec_spec(DIFF_HEAD_DIM)] * 4 + [_vec_spec(DIFF_V_DIM)],
        out_specs=pl.BlockSpec((tq, w), lambda b, j, t: (b * nq + t, j)),
        out_shape=jax.ShapeDtypeStruct((NS, D_MODEL), BF16),
        compiler_params=_cparams(("arbitrary", "arbitrary", "arbitrary")),
        name="diff_attn_latent",
    )(q, k, k_ctx, v, v_ctx, *lams, out_norm)


def _router_kernel(x_ref, g_ref, sh_ref, sc_ref, w_ref, b_ref, h_ref, ids_ref, gates_ref):
    h = _modulate(x_ref[...], g_ref[...], sh_ref[...], sc_ref[...])
    h_hi = h.astype(BF16)
    h_hi32 = h_hi.astype(F32)
    bits = lax.bitcast_convert_type(h_hi32, U32)
    h_ref[...] = (bits[:, :HALF_D] >> 16) | (bits[:, HALF_D:] & jnp.uint32(0xFFFF0000))

    h_lo = (h - h_hi32).astype(BF16)
    two = _dot(h_hi, w_ref[...])
    logits = two[:, :ROUTER_LANES] + two[:, ROUTER_LANES:] + _dot(h_lo, w_ref[:, :ROUTER_LANES]) + b_ref[...]

    lane = lax.broadcasted_iota(I32, logits.shape, 1).astype(F32)
    neg = -jnp.inf
    gl = jnp.where(lane < N_GROUPS, logits, neg)
    gmax = jnp.max(gl, axis=-1, keepdims=True)
    g_sel = jnp.min(jnp.where(gl == gmax, lane, ROUTER_LANES), axis=-1, keepdims=True)
    p_group = 1.0 / jnp.sum(jnp.exp(gl - gmax), axis=-1, keepdims=True)

    lo = N_GROUPS + g_sel * EXPERTS_PER_GROUP
    el = jnp.where(lane >= lo, jnp.where(lane < lo + EXPERTS_PER_GROUP, logits, neg), neg)
    v1 = jnp.max(el, axis=-1, keepdims=True)
    i1 = jnp.min(jnp.where(el == v1, lane, ROUTER_LANES), axis=-1, keepdims=True)
    el2 = jnp.where(lane == i1, neg, el)
    v2 = jnp.max(el2, axis=-1, keepdims=True)
    i2 = jnp.min(jnp.where(el2 == v2, lane, ROUTER_LANES), axis=-1, keepdims=True)
    e2 = jnp.exp(v2 - v1)
    w1 = p_group * (1.0 / (1.0 + e2))
    w2 = p_group * (e2 / (1.0 + e2))
    ids = jnp.where(lane == 0, i1 - N_GROUPS, jnp.where(lane == 1, i2 - N_GROUPS, 0.0))
    ids_ref[...] = ids.astype(I32)
    gates_ref[...] = jnp.where(lane == 0, w1, jnp.where(lane == 1, w2, 0.0))


def _router_call(x, g, mod, w_router2, b_router):
    tm = 256
    return pl.pallas_call(
        _router_kernel,
        grid=(TT // tm,),
        in_specs=[
            pl.BlockSpec((tm, D_MODEL), lambda i: (i, 0)),
            pl.BlockSpec((1, D_MODEL), lambda i: (0, 0)),
            _mod_spec(tm, 3),
            _mod_spec(tm, 4),
            pl.BlockSpec((D_MODEL, 2 * ROUTER_LANES), lambda i: (0, 0)),
            pl.BlockSpec((1, ROUTER_LANES), lambda i: (0, 0)),
        ],
        out_specs=[
            pl.BlockSpec((tm, HALF_D), lambda i: (i, 0)),
            pl.BlockSpec((tm, ROUTER_LANES), lambda i: (i, 0)),
            pl.BlockSpec((tm, ROUTER_LANES), lambda i: (i, 0)),
        ],
        out_shape=[
            jax.ShapeDtypeStruct((TT, HALF_D), U32),
            jax.ShapeDtypeStruct((TT, ROUTER_LANES), I32),
            jax.ShapeDtypeStruct((TT, ROUTER_LANES), F32),
        ],
        compiler_params=_cparams(("arbitrary",)),
        name="moe_router",
    )(x, g, mod, mod, w_router2, b_router)


def _moe_ffn_kernel(tok_ref, be_ref, nb_ref, h_hbm, wg_ref, wu_ref, wd_ref, o_ref, xbuf, hid_s, sems):
    b = pl.program_id(0)
    j = pl.program_id(1)
    nb = nb_ref[0]
    cur = b % 2

    def row_copy(blk, r, s):
        return pltpu.make_async_copy(h_hbm.at[pl.ds(tok_ref[blk * MOE_ROWS + r], 1), :],
                                     xbuf.at[s, pl.ds(r, 1), :], sems.at[s])

    def request_next_rows():
        for r in range(MOE_PREFETCH):
            row_copy(b + 1, j * MOE_PREFETCH + r, 1 - cur).start()

    @pl.when((b == 0) & (j == 0))
    def _():
        def start(r, c):
            row_copy(0, r, 0).start()
            return c
        lax.fori_loop(0, MOE_ROWS, start, 0)

    @pl.when((j == 0) & (b <= nb))
    def _():
        def wait(g, c):
            for r in range(MOE_PREFETCH):
                row_copy(b, g * MOE_PREFETCH + r, cur).wait()
            return c
        lax.fori_loop(0, MOE_STEPS, wait, 0)

    @pl.when((b < nb) & (j < MOE_NJ))
    def _():
        request_next_rows()
        words = xbuf[cur]
        x_lo = lax.bitcast_convert_type(words << 16, F32).astype(BF16)
        x_hi = lax.bitcast_convert_type(words & jnp.uint32(0xFFFF0000), F32).astype(BF16)

        def proj(w_ref):
            return (_dot(x_lo, w_ref[:HALF_D, :].astype(BF16)) + _dot(x_hi, w_ref[HALF_D:, :].astype(BF16)))

        gate = proj(wg_ref)
        up = proj(wu_ref)
        hid_s[j] = (gate * jax.nn.sigmoid(gate) * up).astype(BF16)

    @pl.when((b < nb) & (j >= MOE_NJ))
    def _():
        request_next_rows()
        acc = _dot(hid_s[0], wd_ref[:MOE_HC, :].astype(BF16))
        for k in range(1, MOE_NJ):
            acc = acc + _dot(hid_s[k], wd_ref[k * MOE_HC:(k + 1) * MOE_HC, :].astype(BF16))
        o_ref[...] = acc

    @pl.when(b >= nb)
    def _():
        o_ref[...] = jnp.zeros_like(o_ref)


def _moe_ffn_call(layer, slot_tok, block_e, n_blocks, h_packed, w_gate, w_up, w_down):
    last = MOE_NJ - 1

    def gate_up_map(b, j, tok, be, nb):
        return (layer, be[b], 0, jnp.where(b < nb[0], jnp.minimum(j, last), last))

    def down_map(b, j, tok, be, nb):
        used = b < nb[0]
        early = j < MOE_NJ
        e = jnp.where(used & early, be[jnp.maximum(b - 1, 0)], be[b])
        c = jnp.where(used & jnp.logical_not(early), j - MOE_NJ, D_MODEL // MOE_NC - 1)
        return (layer, e, 0, c)

    return pl.pallas_call(
        _moe_ffn_kernel,
        grid_spec=pltpu.PrefetchScalarGridSpec(
            num_scalar_prefetch=3,
            grid=(N_BLOCKS, MOE_STEPS),
            in_specs=[
                pl.BlockSpec(memory_space=pl.ANY),
                pl.BlockSpec((None, None, D_MODEL, MOE_HC), gate_up_map),
                pl.BlockSpec((None, None, D_MODEL, MOE_HC), gate_up_map),
                pl.BlockSpec((None, None, EXPERT_HIDDEN, MOE_NC), down_map),
            ],
            out_specs=pl.BlockSpec((MOE_ROWS, MOE_NC), lambda b, j, tok, be, nb: (b, jnp.maximum(j - MOE_NJ, 0))),
            scratch_shapes=[
                pltpu.VMEM((2, MOE_ROWS, HALF_D), U32),
                pltpu.VMEM((MOE_NJ, MOE_ROWS, MOE_HC), BF16),
                pltpu.SemaphoreType.DMA((2,)),
            ],
        ),
        out_shape=jax.ShapeDtypeStruct((N_SLOTS, D_MODEL), F32),
        compiler_params=_cparams(("arbitrary", "arbitrary")),
        name="moe_experts",
    )(slot_tok, block_e, n_blocks, h_packed, w_gate, w_up, w_down)


def _combine_kernel(pos_ref, ys_hbm, x_ref, gates_ref, gate2_ref, *rest, tm, unroll, emit_next, split_out):
    if emit_next:
        ng_ref, nsh_ref, nsc_ref = rest[:3]
        rest = rest[3:]
    outs, (buf, sem) = rest[:-2], rest[-2:]
    i = pl.program_id(0)
    base = i * (tm * TOP_K)

    def copies(r):
        return [pltpu.make_async_copy(ys_hbm.at[pl.ds(pos_ref[base + TOP_K * r + k], 1), :],
                                      buf.at[k, pl.ds(r, 1), :], sem) for k in range(TOP_K)]

    def start(t, c):
        for u in range(unroll):
            for cp in copies(t * unroll + u):
                cp.start()
        return c

    def wait(t, c):
        for u in range(unroll):
            for cp in copies(t * unroll + u):
                cp.wait()
        return c

    lax.fori_loop(0, tm // unroll, start, 0)
    lax.fori_loop(0, tm // unroll, wait, 0)
    gates = gates_ref[...]
    y = gates[:, 0:1] * buf[0] + gates[:, 1:2] * buf[1]
    x = x_ref[...] + gate2_ref[...] * y
    if split_out:
        is_prompt = i < NP // tm

        @pl.when(is_prompt)
        def _():
            outs[0][...] = x

        @pl.when(jnp.logical_not(is_prompt))
        def _():
            outs[1][...] = x
    else:
        outs[0][...] = x
    if emit_next:
        outs[-1][...] = _modulate(x, ng_ref[...], nsh_ref[...], nsc_ref[...]).astype(BF16)


def _combine_call(pos, ys, x, gates, mod, next_norm=None, split_out=False):
    tm, unroll = 256, 8
    npt = NP // tm
    row = lambda i, pos: (i, 0)
    in_specs = [
        pl.BlockSpec(memory_space=pl.ANY),
        pl.BlockSpec((tm, D_MODEL), row),
        pl.BlockSpec((tm, ROUTER_LANES), row),
        pl.BlockSpec((None, 1, D_MODEL), lambda i, pos: (_cond_row(i * tm), 0, 5)),
    ]
    args = [ys, x, gates, mod]
    if next_norm is not None:
        g_next, mod_next = next_norm
        in_specs += [
            pl.BlockSpec((1, D_MODEL), lambda i, pos: (0, 0)),
            pl.BlockSpec((None, 1, D_MODEL), lambda i, pos: (_cond_row(i * tm), 0, 0)),
            pl.BlockSpec((None, 1, D_MODEL), lambda i, pos: (_cond_row(i * tm), 0, 1)),
        ]
        args += [g_next, mod_next, mod_next]
    if split_out:
        out_specs = [pl.BlockSpec((tm, D_MODEL), lambda i, pos: (jnp.minimum(i, npt - 1), 0)),
                     pl.BlockSpec((tm, D_MODEL), lambda i, pos: (jnp.maximum(i - npt, 0), 0))]
        out_shape = [jax.ShapeDtypeStruct((NP, D_MODEL), F32), jax.ShapeDtypeStruct((NS, D_MODEL), F32)]
    else:
        out_specs = [pl.BlockSpec((tm, D_MODEL), row)]
        out_shape = [jax.ShapeDtypeStruct((TT, D_MODEL), F32)]
    if next_norm is not None:
        out_specs.append(pl.BlockSpec((tm, D_MODEL), row))
        out_shape.append(jax.ShapeDtypeStruct((TT, D_MODEL), BF16))
    return pl.pallas_call(
        functools.partial(_combine_kernel, tm=tm, unroll=unroll, emit_next=next_norm is not None,
                          split_out=split_out),
        grid_spec=pltpu.PrefetchScalarGridSpec(
            num_scalar_prefetch=1,
            grid=(TT // tm,),
            in_specs=in_specs,
            out_specs=out_specs,
            scratch_shapes=[pltpu.VMEM((TOP_K, tm, D_MODEL), F32), pltpu.SemaphoreType.DMA(())],
        ),
        out_shape=out_shape,
        compiler_params=_cparams(("arbitrary",)),
        name="moe_combine",
    )(pos, *args)


def _dispatch_plan(ids):
    flat_e = ids[:, :TOP_K].reshape(N_ASSIGN)
    onehot = (flat_e[:, None] == jnp.arange(N_EXPERTS, dtype=I32)[None, :]).astype(I32)
    csum = jnp.cumsum(onehot, axis=0)
    rank = jnp.take_along_axis(csum, flat_e[:, None], axis=1)[:, 0] - 1
    counts = csum[-1]
    padded = (counts + MOE_ROWS - 1) // MOE_ROWS * MOE_ROWS
    pad_ends = jnp.cumsum(padded)
    pad_starts = pad_ends - padded
    dest = (pad_starts[flat_e] + rank).astype(I32)
    slot_tok = jnp.zeros((N_SLOTS,), I32).at[dest].set(jnp.arange(N_ASSIGN, dtype=I32) // TOP_K)
    n_blocks = (pad_ends[-1] // MOE_ROWS).astype(I32)
    blk = jnp.minimum(jnp.arange(N_BLOCKS, dtype=I32), n_blocks - 1)
    block_e = jnp.minimum(jnp.searchsorted(pad_ends, blk * MOE_ROWS, side="right"), N_EXPERTS - 1).astype(I32)
    return dest, slot_tok, block_e, n_blocks.reshape(1)


def _moe(layer, x, g, mod, w_router2, b_router, w_gate, w_up, w_down, **combine_kw):
    h_packed, ids, gates = _router_call(x, g, mod, w_router2, b_router)
    dest, slot_tok, block_e, n_blocks = _dispatch_plan(ids)
    ys = _moe_ffn_call(layer, slot_tok, block_e, n_blocks, h_packed, w_gate, w_up, w_down)
    return _combine_call(dest, ys, x, gates, mod, **combine_kw)


def _rope_tables(rot_dim):
    rows = DEC_SEQ // GRID_W
    nf = rot_dim // 4
    inv = jnp.exp(-math.log(ROPE_THETA) * jnp.arange(nf, dtype=F32) / nf)
    row_pos = jnp.repeat(jnp.arange(rows, dtype=F32), GRID_W)
    col_pos = jnp.tile(jnp.arange(GRID_W, dtype=F32), rows)
    ang = jnp.stack([row_pos[:, None] * inv, col_pos[:, None] * inv], axis=1)
    cos = jnp.broadcast_to(jnp.cos(ang)[:, :, None, :], (DEC_SEQ, 2, 2, nf)).reshape(DEC_SEQ, rot_dim)
    sgn = jnp.array([-1.0, 1.0], F32)[None, None, :, None]
    sin = (jnp.sin(ang)[:, :, None, :] * sgn).reshape(DEC_SEQ, rot_dim)
    pad = 128 - rot_dim
    if pad:
        cos = jnp.concatenate([cos, jnp.ones((DEC_SEQ, pad), F32)], axis=1)
        sin = jnp.concatenate([sin, jnp.zeros((DEC_SEQ, pad), F32)], axis=1)
    return cos, sin


def _pad_head_vec(v):
    return jnp.concatenate([v, jnp.zeros((MLA_HEAD_PAD - MLA_QK,), F32)]).reshape(1, MLA_HEAD_PAD)


def _router_weights(w_group, b_group, w_router, b_router):
    pad = ROUTER_LANES - N_GROUPS - N_EXPERTS
    w = jnp.concatenate([w_group, w_router, jnp.zeros((D_MODEL, pad), F32)], axis=1)
    w_hi = w.astype(BF16)
    w_lo = (w - w_hi.astype(F32)).astype(BF16)
    b = jnp.concatenate([b_group, b_router, jnp.zeros((pad,), F32)]).reshape(1, ROUTER_LANES)
    return jnp.concatenate([w_hi, w_lo], axis=1), b


def kernel(x_prompt, x_sample, cache_mla_ckv, cache_mla_krope, cache_diff_k, cache_diff_v, c, c_ctx, norm1_g, norm2_g, ada_w, ada_b, mla_w_in, mla_q_a_norm, mla_w_q_up, mla_kv_a_norm, mla_w_kv_up, mla_q_norm, mla_k_norm, mla_w_out, diff_w_qkv, diff_q_norm, diff_k_norm, diff_lambda_q1, diff_lambda_k1, diff_lambda_q2, diff_lambda_k2, diff_out_norm, diff_w_out, moe_w_group, moe_b_group, moe_w_router, moe_b_router, moe_w_gate, moe_w_up, moe_w_down):
    assert DEPTH == 2
    xp0 = x_prompt.reshape(NP, D_MODEL)
    xs0 = x_sample.reshape(NS, D_MODEL)
    cond = jnp.concatenate([c_ctx[None, :], c, jnp.zeros((N_COND - 1 - DEC_BATCH, D_MODEL), F32)], axis=0)
    mods = _ada_call(cond, ada_w, ada_b).reshape(DEPTH, N_COND, 1, 6 * D_MODEL)
    routers = [_router_weights(moe_w_group[i], moe_b_group[i], moe_w_router[i], moe_b_router[i])
               for i in range(DEPTH)]
    g2 = [norm2_g[i].reshape(1, D_MODEL) for i in range(DEPTH)]

    mod = mods[0]
    cos, sin = _rope_tables(MLA_ROPE)
    w_in = jnp.pad(mla_w_in[0], ((0, 0), (0, MLA_IN_PAD - MLA_IN))).astype(BF16)
    w_q = jnp.pad(mla_w_q_up[0].reshape(MLA_Q_LORA, MLA_HEADS, MLA_QK), ((0, 0), (0, 0), (0, MLA_HEAD_PAD - MLA_QK)))
    w_q = w_q.reshape(MLA_Q_LORA, MLA_HEADS * MLA_HEAD_PAD).astype(BF16)
    cq, ckv, kr = _mla_in_call(xp0, xs0, norm1_g[0].reshape(1, D_MODEL), mod, w_in,
                               mla_q_a_norm[0].reshape(1, -1), mla_kv_a_norm[0].reshape(1, -1))
    q = _q_up_call(cq, w_q, _pad_head_vec(mla_q_norm[0]), cos, sin)
    ckv_all = jnp.concatenate([ckv, cache_mla_ckv[:, 0].reshape(NCTX, MLA_KV_LORA)], axis=0)
    kr_ctx = jnp.pad(cache_mla_krope[:, 0].reshape(NCTX, MLA_ROPE), ((0, 0), (0, 128 - MLA_ROPE)))
    kr_all = jnp.concatenate([kr, kr_ctx], axis=0)
    k, v = _kv_up_call(ckv_all, kr_all, mla_w_kv_up[0].astype(BF16), _pad_head_vec(mla_k_norm[0]), cos, sin)
    o_p = _mla_attn_prompt_call(q, k, v)
    o_s = _mla_attn_latent_call(q, k, v)
    x = _out_proj_call(o_p, o_s, mla_w_out[0].astype(BF16), xp0, xs0, True, mod)
    new_ckv = ckv[:NP].reshape(BATCH, 1, SEQ, MLA_KV_LORA)
    new_krope = kr[:NP, :MLA_ROPE].reshape(BATCH, 1, SEQ, MLA_ROPE)
    x, h = _moe(0, x, g2[0], mod, *routers[0], moe_w_gate, moe_w_up, moe_w_down,
                next_norm=(norm1_g[1].reshape(1, D_MODEL), mods[1]))

    mod = mods[1]
    lam_init = 0.8 - 0.6 * math.exp(-0.3 * 1)
    cos, sin = _rope_tables(DIFF_HEAD_DIM)
    q, k, v = _diff_qkv_call(h, diff_w_qkv[0].astype(BF16), diff_q_norm[0].reshape(1, -1),
                             diff_k_norm[0].reshape(1, -1), cos, sin)
    lams = [t[0].reshape(1, DIFF_HEAD_DIM) for t in
            (diff_lambda_q1, diff_lambda_k1, diff_lambda_q2, diff_lambda_k2)]
    out_norm = diff_out_norm[0].reshape(1, DIFF_V_DIM)
    o_p = _diff_attn_prompt_call(q, k, v, lams, out_norm, lam_init)
    o_s = _diff_attn_latent_call(q, k, v, cache_diff_k[:, 0].reshape(NCTX, D_MODEL),
                                 cache_diff_v[:, 0].reshape(NCTX, D_MODEL), lams, out_norm, lam_init)
    x = _out_proj_call(o_p, o_s, diff_w_out[0].astype(BF16), x, x, False, mod)
    new_k = k[:NP].reshape(BATCH, 1, SEQ, DIFF_HEADS, 2, DIFF_HEAD_DIM)
    new_v = v[:NP].reshape(BATCH, 1, SEQ, DIFF_HEADS, DIFF_V_DIM)
    y_p, y_s = _moe(1, x, g2[1], mod, *routers[1], moe_w_gate, moe_w_up, moe_w_down, split_out=True)

    return (y_p.reshape(BATCH, SEQ, D_MODEL), y_s.reshape(DEC_BATCH, DEC_SEQ, D_MODEL),
            new_ckv, new_krope, new_k, new_v)
```

```python
import functools
import math

import jax
import jax.numpy as jnp
from jax import lax
from jax.experimental import pallas as pl
from jax.experimental.pallas import tpu as pltpu

F32 = jnp.float32
BF16 = jnp.bfloat16
I32 = jnp.int32
U32 = jnp.uint32

D_MODEL = 4096
BATCH = 32
SEQ = 256
DEPTH = 2
DEC_BATCH = 4
DEC_SEQ = 1024
PAST_LEN = 256
GRID_W = 64
MLA_HEADS = 32
MLA_NOPE = 128
MLA_ROPE = 64
MLA_QK = MLA_NOPE + MLA_ROPE
MLA_V = 128
MLA_Q_LORA = 1024
MLA_KV_LORA = 512
MLA_IN = MLA_Q_LORA + MLA_KV_LORA + MLA_ROPE
MLA_IN_PAD = MLA_Q_LORA + MLA_KV_LORA + 128
MLA_HEAD_PAD = 256
DIFF_HEAD_DIM = 128
DIFF_HEADS = 16
DIFF_V_DIM = 256
N_GROUPS = 8
EXPERTS_PER_GROUP = 8
N_EXPERTS = 64
TOP_K = 2
EXPERT_HIDDEN = 1024
ROPE_THETA = 10000.0
NORM_EPS = 1e-6

NP = BATCH * SEQ
NS = DEC_BATCH * DEC_SEQ
TT = NP + NS
NCTX = DEC_BATCH * PAST_LEN
N_COND = 8
N_ASSIGN = TT * TOP_K
ROUTER_LANES = 128
HALF_D = D_MODEL // 2

MOE_ROWS = 512
GU_COLS = 256
GU_CHUNKS = EXPERT_HIDDEN // GU_COLS
GU_SLOTS = 3
DN_COLS = 512
DN_CHUNKS = D_MODEL // DN_COLS
DN_SLOTS = 4
GU_PHASE_ROWS = 64
DN_PHASE_ROWS = (MOE_ROWS - GU_CHUNKS * GU_PHASE_ROWS) // DN_CHUNKS
ROW_WAIT_GROUP = 64
N_BLOCKS = N_ASSIGN // MOE_ROWS + N_EXPERTS + 1
N_SLOTS = N_BLOCKS * MOE_ROWS

VMEM_LIMIT = 56 * 1024 * 1024


def _cparams(sem):
    return pltpu.CompilerParams(dimension_semantics=sem, vmem_limit_bytes=VMEM_LIMIT)


def _cond_row(row0):
    return jnp.where(row0 < NP, 0, 1 + (row0 - NP) // DEC_SEQ)


def _rms(x, g):
    return x * lax.rsqrt(jnp.mean(x * x, axis=-1, keepdims=True) + NORM_EPS) * g


def _modulate(x, g, shift, scale):
    return _rms(x, g) * (1.0 + scale) + shift


def _dot(a, b):
    return jnp.dot(a, b, preferred_element_type=F32)


def _dot_nt(a, b):
    return lax.dot_general(a, b, (((1,), (1,)), ((), ())), preferred_element_type=F32)


def _rope128(x, cos, sin_signed, half):
    lane = lax.broadcasted_iota(I32, x.shape, 1)
    first_half = ((lane // half) % 2) == 0
    partner = jnp.where(first_half, pltpu.roll(x, 128 - half, 1), pltpu.roll(x, half, 1))
    return x * cos + partner * sin_signed


def _latent_tables(cos_ref, sin_ref, is_latent):
    m = is_latent.astype(F32)
    return 1.0 + m * (cos_ref[...] - 1.0), m * sin_ref[...]


def _row_pair_specs(tm, tn, split):
    npt = NP // tm
    off = npt if split else 0
    last = D_MODEL // tn - 1

    def prompt_map(i, *a):
        return (jnp.minimum(i, npt - 1), jnp.where(i < npt, a[0], last) if last else 0)

    def latent_map(i, *a):
        return (jnp.maximum(i, npt) - off, jnp.where(i >= npt, a[0], 0) if last else 0)

    return pl.BlockSpec((tm, tn), prompt_map), pl.BlockSpec((tm, tn), latent_map)


def _ada_kernel(c_ref, w_ref, b_ref, o_ref):
    c = c_ref[...]
    a = (c * jax.nn.sigmoid(c)).astype(BF16)
    o_ref[...] = _dot(a, w_ref[...].astype(BF16)) + b_ref[...]


def _ada_call(cond, ada_w, ada_b):
    tn = 512
    n = 6 * D_MODEL
    return pl.pallas_call(
        _ada_kernel,
        grid=(DEPTH, n // tn),
        in_specs=[
            pl.BlockSpec((N_COND, D_MODEL), lambda l, j: (0, 0)),
            pl.BlockSpec((None, D_MODEL, tn), lambda l, j: (l, 0, j)),
            pl.BlockSpec((None, 1, tn), lambda l, j: (l, 0, j)),
        ],
        out_specs=pl.BlockSpec((None, N_COND, tn), lambda l, j: (l, 0, j)),
        out_shape=jax.ShapeDtypeStruct((DEPTH, N_COND, n), F32),
        compiler_params=_cparams(("arbitrary", "arbitrary")),
        name="ada_table",
    )(cond, ada_w, ada_b.reshape(DEPTH, 1, n))


def _mod_spec(tm, chunk, width=D_MODEL):
    per = D_MODEL // width
    return pl.BlockSpec((None, 1, width),
                        lambda i, *a: (_cond_row(i * tm), 0, chunk * per + (a[0] if per > 1 else 0)))


def _mla_in_kernel(xp_ref, xs_ref, g_ref, sh_ref, sc_ref, w_ref, qg_ref, kvg_ref, cq_ref, ckv_ref, kr_ref, *, tm):
    def run(x_ref):
        h = _modulate(x_ref[...], g_ref[...], sh_ref[...], sc_ref[...]).astype(BF16)
        p = _dot(h, w_ref[...])
        cq_ref[...] = _rms(p[:, :MLA_Q_LORA], qg_ref[...]).astype(BF16)
        ckv_ref[...] = _rms(p[:, MLA_Q_LORA:MLA_Q_LORA + MLA_KV_LORA], kvg_ref[...])
        kr_ref[...] = p[:, MLA_Q_LORA + MLA_KV_LORA:]

    is_prompt = pl.program_id(0) < NP // tm
    pl.when(is_prompt)(lambda: run(xp_ref))
    pl.when(jnp.logical_not(is_prompt))(lambda: run(xs_ref))


def _mla_in_call(x_prompt, x_latent, g, mod, w_in, q_a_norm, kv_a_norm):
    tm = 256
    xp_spec, xs_spec = _row_pair_specs(tm, D_MODEL, split=True)
    return pl.pallas_call(
        functools.partial(_mla_in_kernel, tm=tm),
        grid=(TT // tm,),
        in_specs=[
            xp_spec,
            xs_spec,
            pl.BlockSpec((1, D_MODEL), lambda i: (0, 0)),
            _mod_spec(tm, 0),
            _mod_spec(tm, 1),
            pl.BlockSpec((D_MODEL, MLA_IN_PAD), lambda i: (0, 0)),
            pl.BlockSpec((1, MLA_Q_LORA), lambda i: (0, 0)),
            pl.BlockSpec((1, MLA_KV_LORA), lambda i: (0, 0)),
        ],
        out_specs=[
            pl.BlockSpec((tm, MLA_Q_LORA), lambda i: (i, 0)),
            pl.BlockSpec((tm, MLA_KV_LORA), lambda i: (i, 0)),
            pl.BlockSpec((tm, 128), lambda i: (i, 0)),
        ],
        out_shape=[
            jax.ShapeDtypeStruct((TT, MLA_Q_LORA), BF16),
            jax.ShapeDtypeStruct((TT, MLA_KV_LORA), F32),
            jax.ShapeDtypeStruct((TT, 128), F32),
        ],
        compiler_params=_cparams(("arbitrary",)),
        name="mla_in",
    )(x_prompt, x_latent, g, mod, mod, w_in, q_a_norm, kv_a_norm)


def _q_up_kernel(c_ref, w_ref, g_ref, cos_ref, sin_ref, q_ref, *, tm, heads):
    i = pl.program_id(0)
    cos, sin = _latent_tables(cos_ref, sin_ref, i >= NP // tm)
    o = _dot(c_ref[...], w_ref[...])
    g = g_ref[...] * (MLA_QK ** -0.5)
    for hh in range(heads):
        s = o[:, hh * 256:(hh + 1) * 256]
        r = lax.rsqrt(jnp.sum(s * s, axis=-1, keepdims=True) / MLA_QK + NORM_EPS)
        lo = s[:, :128] * r * g[:, :128]
        hi = _rope128(s[:, 128:] * r * g[:, 128:], cos, sin, MLA_ROPE // 4)
        q_ref[:, hh * 256:hh * 256 + 128] = lo.astype(BF16)
        q_ref[:, hh * 256 + 128:(hh + 1) * 256] = hi.astype(BF16)


def _pos_block(i, tm):
    return jnp.clip(i - NP // tm, 0, NS // tm - 1) % (DEC_SEQ // tm)


def _q_up_call(cq, w_q, q_norm_pad, cos, sin):
    tm, heads = 512, 8
    tn = heads * MLA_HEAD_PAD
    return pl.pallas_call(
        functools.partial(_q_up_kernel, tm=tm, heads=heads),
        grid=(TT // tm, MLA_HEADS // heads),
        in_specs=[
            pl.BlockSpec((tm, MLA_Q_LORA), lambda i, j: (i, 0)),
            pl.BlockSpec((MLA_Q_LORA, tn), lambda i, j: (0, j)),
            pl.BlockSpec((1, MLA_HEAD_PAD), lambda i, j: (0, 0)),
            pl.BlockSpec((tm, 128), lambda i, j: (_pos_block(i, tm), 0)),
            pl.BlockSpec((tm, 128), lambda i, j: (_pos_block(i, tm), 0)),
        ],
        out_specs=pl.BlockSpec((tm, tn), lambda i, j: (i, j)),
        out_shape=jax.ShapeDtypeStruct((TT, MLA_HEADS * MLA_HEAD_PAD), BF16),
        compiler_params=_cparams(("arbitrary", "arbitrary")),
        name="mla_q_up",
    )(cq, w_q, q_norm_pad, cos, sin)


def _kv_up_kernel(c_ref, kr_ref, w_ref, g_ref, cos_ref, sin_ref, k_ref, v_ref, *, tm, heads):
    i = pl.program_id(0)
    cos, sin = _latent_tables(cos_ref, sin_ref, (i >= NP // tm) & (i < TT // tm))
    kv = _dot(c_ref[...].astype(BF16), w_ref[...])
    g = g_ref[...]
    kr = kr_ref[...]
    ss_rope = jnp.sum(kr * kr, axis=-1, keepdims=True)
    base = _rope128(kr * g[:, 128:], cos, sin, MLA_ROPE // 4)
    for hh in range(heads):
        kn = kv[:, hh * 256:hh * 256 + 128]
        r = lax.rsqrt((jnp.sum(kn * kn, axis=-1, keepdims=True) + ss_rope) / MLA_QK + NORM_EPS)
        k_ref[:, hh * 256:hh * 256 + 128] = (kn * r * g[:, :128]).astype(BF16)
        k_ref[:, hh * 256 + 128:(hh + 1) * 256] = (base * r).astype(BF16)
        v_ref[:, hh * 128:(hh + 1) * 128] = kv[:, hh * 256 + 128:(hh + 1) * 256].astype(BF16)


def _kv_up_call(ckv_all, kr_all, w_kv, k_norm_pad, cos, sin):
    tm, heads = 512, 8
    rows = TT + NCTX
    return pl.pallas_call(
        functools.partial(_kv_up_kernel, tm=tm, heads=heads),
        grid=(rows // tm, MLA_HEADS // heads),
        in_specs=[
            pl.BlockSpec((tm, MLA_KV_LORA), lambda i, j: (i, 0)),
            pl.BlockSpec((tm, 128), lambda i, j: (i, 0)),
            pl.BlockSpec((MLA_KV_LORA, heads * 256), lambda i, j: (0, j)),
            pl.BlockSpec((1, MLA_HEAD_PAD), lambda i, j: (0, 0)),
            pl.BlockSpec((tm, 128), lambda i, j: (_pos_block(i, tm), 0)),
            pl.BlockSpec((tm, 128), lambda i, j: (_pos_block(i, tm), 0)),
        ],
        out_specs=[
            pl.BlockSpec((tm, heads * MLA_HEAD_PAD), lambda i, j: (i, j)),
            pl.BlockSpec((tm, heads * MLA_V), lambda i, j: (i, j)),
        ],
        out_shape=[
            jax.ShapeDtypeStruct((rows, MLA_HEADS * MLA_HEAD_PAD), BF16),
            jax.ShapeDtypeStruct((rows, MLA_HEADS * MLA_V), BF16),
        ],
        compiler_params=_cparams(("arbitrary", "arbitrary")),
        name="mla_kv_up",
    )(ckv_all, kr_all, w_kv, k_norm_pad, cos, sin)


def _mla_attn_prompt_kernel(q_ref, k_ref, v_ref, o_ref, *, heads):
    for hh in range(heads):
        q = q_ref[:, hh * 256:(hh + 1) * 256]
        k = k_ref[:, hh * 256:(hh + 1) * 256]
        s = _dot_nt(q, k)
        p = jnp.exp(s - jnp.max(s, axis=-1, keepdims=True))
        l = jnp.sum(p, axis=-1, keepdims=True)
        o = _dot(p.astype(BF16), v_ref[:, hh * 128:(hh + 1) * 128]) / l
        o_ref[:, hh * 128:(hh + 1) * 128] = o.astype(BF16)


def _mla_attn_prompt_call(q, k, v):
    heads = 8
    return pl.pallas_call(
        functools.partial(_mla_attn_prompt_kernel, heads=heads),
        grid=(BATCH, MLA_HEADS // heads),
        in_specs=[
            pl.BlockSpec((SEQ, heads * 256), lambda b, j: (b, j)),
            pl.BlockSpec((SEQ, heads * 256), lambda b, j: (b, j)),
            pl.BlockSpec((SEQ, heads * 128), lambda b, j: (b, j)),
        ],
        out_specs=pl.BlockSpec((SEQ, heads * 128), lambda b, j: (b, j)),
        out_shape=jax.ShapeDtypeStruct((NP, MLA_HEADS * MLA_V), BF16),
        compiler_params=_cparams(("arbitrary", "arbitrary")),
        name="mla_attn_prompt",
    )(q, k, v)


def _mla_attn_latent_kernel(q_ref, kl_ref, kc_ref, vl_ref, vc_ref, o_ref, *, heads):
    for hh in range(heads):
        q = q_ref[:, hh * 256:(hh + 1) * 256]
        sl = _dot_nt(q, kl_ref[:, hh * 256:(hh + 1) * 256])
        sc = _dot_nt(q, kc_ref[:, hh * 256:(hh + 1) * 256])
        m = jnp.maximum(jnp.max(sl, axis=-1, keepdims=True), jnp.max(sc, axis=-1, keepdims=True))
        pl_ = jnp.exp(sl - m)
        pc = jnp.exp(sc - m)
        l = jnp.sum(pl_, axis=-1, keepdims=True) + jnp.sum(pc, axis=-1, keepdims=True)
        o = _dot(pl_.astype(BF16), vl_ref[:, hh * 128:(hh + 1) * 128])
        o = o + _dot(pc.astype(BF16), vc_ref[:, hh * 128:(hh + 1) * 128])
        o_ref[:, hh * 128:(hh + 1) * 128] = (o / l).astype(BF16)


def _mla_attn_latent_call(q, k, v):
    heads, tq = 8, 512
    nq = DEC_SEQ // tq
    return pl.pallas_call(
        functools.partial(_mla_attn_latent_kernel, heads=heads),
        grid=(DEC_BATCH, MLA_HEADS // heads, nq),
        in_specs=[
            pl.BlockSpec((tq, heads * 256), lambda b, j, t: (NP // tq + b * nq + t, j)),
            pl.BlockSpec((DEC_SEQ, heads * 256), lambda b, j, t: (NP // DEC_SEQ + b, j)),
            pl.BlockSpec((PAST_LEN, heads * 256), lambda b, j, t: (TT // PAST_LEN + b, j)),
            pl.BlockSpec((DEC_SEQ, heads * 128), lambda b, j, t: (NP // DEC_SEQ + b, j)),
            pl.BlockSpec((PAST_LEN, heads * 128), lambda b, j, t: (TT // PAST_LEN + b, j)),
        ],
        out_specs=pl.BlockSpec((tq, heads * 128), lambda b, j, t: (b * nq + t, j)),
        out_shape=jax.ShapeDtypeStruct((NS, MLA_HEADS * MLA_V), BF16),
        compiler_params=_cparams(("arbitrary", "arbitrary", "arbitrary")),
        name="mla_attn_latent",
    )(q, k, k, v, v)


def _out_proj_kernel(ap_ref, as_ref, w_ref, xp_ref, xs_ref, gate_ref, o_ref, *, tm):
    def finish(a_ref, x_ref):
        o_ref[...] = x_ref[...] + gate_ref[...] * _dot(a_ref[...], w_ref[...])

    is_prompt = pl.program_id(0) < NP // tm
    pl.when(is_prompt)(lambda: finish(ap_ref, xp_ref))
    pl.when(jnp.logical_not(is_prompt))(lambda: finish(as_ref, xs_ref))


def _out_proj_call(a_prompt, a_latent, w, x_prompt, x_latent, x_split, mod):
    tm, tn = 512, 1024
    ap_spec, as_spec = _row_pair_specs(tm, D_MODEL, split=True)
    xp_spec, xs_spec = _row_pair_specs(tm, tn, split=x_split)
    return pl.pallas_call(
        functools.partial(_out_proj_kernel, tm=tm),
        grid=(TT // tm, D_MODEL // tn),
        in_specs=[
            ap_spec,
            as_spec,
            pl.BlockSpec((D_MODEL, tn), lambda i, j: (0, j)),
            xp_spec,
            xs_spec,
            _mod_spec(tm, 2, width=tn),
        ],
        out_specs=pl.BlockSpec((tm, tn), lambda i, j: (i, j)),
        out_shape=jax.ShapeDtypeStruct((TT, D_MODEL), F32),
        compiler_params=_cparams(("arbitrary", "arbitrary")),
        name="attn_out_proj",
    )(a_prompt, a_latent, w, x_prompt, x_latent, mod)


def _diff_qkv_kernel(h_ref, w_ref, qn_ref, kn_ref, cos_ref, sin_ref, q_ref, kp_ref, kl_ref, vp_ref, vl_ref,
                     *, tm, nq, sub):
    i = pl.program_id(0)
    j = pl.program_id(1)
    is_prompt = i < NP // tm
    cos, sin = _latent_tables(cos_ref, sin_ref, jnp.logical_not(is_prompt))
    n_sub = w_ref.shape[1] // sub

    def store_rows(prompt_ref, latent_ref, pieces):
        def fill(ref):
            for c0, y in pieces:
                ref[:, c0:c0 + y.shape[1]] = y
        pl.when(is_prompt)(lambda: fill(prompt_ref))
        pl.when(jnp.logical_not(is_prompt))(lambda: fill(latent_ref))

    def norm_rope(o, g, scale):
        return [_rope128(_rms(o[:, s * 128:(s + 1) * 128], g) * scale, cos, sin, DIFF_HEAD_DIM // 4)
                for s in range(sub // 128)]

    @pl.when(j < nq)
    def _():
        for t in range(n_sub):
            o = _dot(h_ref[...], w_ref[:, t * sub:(t + 1) * sub])
            for s, y in enumerate(norm_rope(o, qn_ref[...], DIFF_HEAD_DIM ** -0.5)):
                q_ref[:, t * sub + s * 128:t * sub + (s + 1) * 128] = y.astype(BF16)

    @pl.when((j >= nq) & (j < 2 * nq))
    def _():
        pieces = []
        for t in range(n_sub):
            o = _dot(h_ref[...], w_ref[:, t * sub:(t + 1) * sub])
            pieces += [(t * sub + s * 128, y) for s, y in enumerate(norm_rope(o, kn_ref[...], 1.0))]
        store_rows(kp_ref, kl_ref, pieces)

    @pl.when(j >= 2 * nq)
    def _():
        store_rows(vp_ref, vl_ref, [(0, _dot(h_ref[...], w_ref[...]))])


def _diff_qkv_call(h, w_qkv, q_norm, k_norm, cos, sin):
    tm, tn, sub = 512, 1024, 512
    nq = D_MODEL // tn
    npt = NP // tm

    def kv_specs(first):
        col = lambda j: jnp.clip(j - first, 0, nq - 1)
        return [pl.BlockSpec((tm, tn), lambda i, j: (jnp.minimum(i, npt - 1), jnp.where(i < npt, col(j), nq - 1))),
                pl.BlockSpec((tm, tn), lambda i, j: (jnp.maximum(i - npt, 0), jnp.where(i >= npt, col(j), 0)))]

    kv_shapes = [jax.ShapeDtypeStruct((NP, D_MODEL), F32), jax.ShapeDtypeStruct((NS, D_MODEL), F32)]
    return pl.pallas_call(
        functools.partial(_diff_qkv_kernel, tm=tm, nq=nq, sub=sub),
        grid=(TT // tm, 3 * nq),
        in_specs=[
            pl.BlockSpec((tm, D_MODEL), lambda i, j: (i, 0)),
            pl.BlockSpec((D_MODEL, tn), lambda i, j: (0, j)),
            pl.BlockSpec((1, DIFF_HEAD_DIM), lambda i, j: (0, 0)),
            pl.BlockSpec((1, DIFF_HEAD_DIM), lambda i, j: (0, 0)),
            pl.BlockSpec((tm, 128), lambda i, j: (_pos_block(i, tm), 0)),
            pl.BlockSpec((tm, 128), lambda i, j: (_pos_block(i, tm), 0)),
        ],
        out_specs=[pl.BlockSpec((tm, tn), lambda i, j: (i, jnp.minimum(j, nq - 1)))]
        + kv_specs(nq) + kv_specs(2 * nq),
        out_shape=[jax.ShapeDtypeStruct((TT, D_MODEL), BF16)] + kv_shapes + kv_shapes,
        compiler_params=_cparams(("arbitrary", "arbitrary")),
        name="diff_qkv",
    )(h, w_qkv, q_norm, k_norm, cos, sin)


def _diff_lambda(lq1_ref, lk1_ref, lq2_ref, lk2_ref, lam_init):
    e1 = jnp.exp(jnp.sum(lq1_ref[...] * lk1_ref[...], axis=-1, keepdims=True))
    e2 = jnp.exp(jnp.sum(lq2_ref[...] * lk2_ref[...], axis=-1, keepdims=True))
    return e1 - e2 + lam_init


def _softmax_rows(s):
    p = jnp.exp(s - jnp.max(s, axis=-1, keepdims=True))
    return p / jnp.sum(p, axis=-1, keepdims=True)


def _diff_attn_prompt_kernel(q_ref, k_ref, v_ref, lq1, lk1, lq2, lk2, on_ref, o_ref, *, heads, lam_init):
    lam = _diff_lambda(lq1, lk1, lq2, lk2, lam_init)
    for hh in range(heads):
        c0 = hh * 256
        p1 = _softmax_rows(_dot_nt(q_ref[:, c0:c0 + 128], k_ref[:, c0:c0 + 128].astype(BF16)))
        p2 = _softmax_rows(_dot_nt(q_ref[:, c0 + 128:c0 + 256], k_ref[:, c0 + 128:c0 + 256].astype(BF16)))
        a = (p1 - lam * p2).astype(BF16)
        o = _dot(a, v_ref[:, c0:c0 + 256].astype(BF16))
        o_ref[:, c0:c0 + 256] = (_rms(o, on_ref[...]) * (1.0 - lam_init)).astype(BF16)


def _vec_spec(n):
    return pl.BlockSpec((1, n), lambda *a: (0, 0))


def _diff_attn_prompt_call(q, k, v, lams, out_norm, lam_init):
    heads = 4
    w = heads * 256
    blk = pl.BlockSpec((SEQ, w), lambda b, j: (b, j))
    return pl.pallas_call(
        functools.partial(_diff_attn_prompt_kernel, heads=heads, lam_init=lam_init),
        grid=(BATCH, DIFF_HEADS // heads),
        in_specs=[blk, blk, blk] + [_vec_spec(DIFF_HEAD_DIM)] * 4 + [_vec_spec(DIFF_V_DIM)],
        out_specs=blk,
        out_shape=jax.ShapeDtypeStruct((NP, D_MODEL), BF16),
        compiler_params=_cparams(("arbitrary", "arbitrary")),
        name="diff_attn_prompt",
    )(q, k, v, *lams, out_norm)


def _softmax2(sa, sb):
    m = jnp.maximum(jnp.max(sa, axis=-1, keepdims=True), jnp.max(sb, axis=-1, keepdims=True))
    pa = jnp.exp(sa - m)
    pb = jnp.exp(sb - m)
    l = jnp.sum(pa, axis=-1, keepdims=True) + jnp.sum(pb, axis=-1, keepdims=True)
    return pa / l, pb / l


def _diff_attn_latent_kernel(q_ref, kl_ref, kc_ref, vl_ref, vc_ref, lq1, lk1, lq2, lk2, on_ref,
                             o_ref, *, heads, lam_init):
    lam = _diff_lambda(lq1, lk1, lq2, lk2, lam_init)
    for hh in range(heads):
        c0 = hh * 256
        q1 = q_ref[:, c0:c0 + 128]
        q2 = q_ref[:, c0 + 128:c0 + 256]
        p1l, p1c = _softmax2(_dot_nt(q1, kl_ref[:, c0:c0 + 128].astype(BF16)),
                             _dot_nt(q1, kc_ref[:, c0:c0 + 128].astype(BF16)))
        p2l, p2c = _softmax2(_dot_nt(q2, kl_ref[:, c0 + 128:c0 + 256].astype(BF16)),
                             _dot_nt(q2, kc_ref[:, c0 + 128:c0 + 256].astype(BF16)))
        o = _dot((p1l - lam * p2l).astype(BF16), vl_ref[:, c0:c0 + 256].astype(BF16))
        o = o + _dot((p1c - lam * p2c).astype(BF16), vc_ref[:, c0:c0 + 256].astype(BF16))
        o_ref[:, c0:c0 + 256] = (_rms(o, on_ref[...]) * (1.0 - lam_init)).astype(BF16)


def _diff_attn_latent_call(q, k, v, k_ctx, v_ctx, lams, out_norm, lam_init):
    heads, tq = 4, 256
    w = heads * 256
    nq = DEC_SEQ // tq
    lat = pl.BlockSpec((DEC_SEQ, w), lambda b, j, t: (b, j))
    ctx = pl.BlockSpec((PAST_LEN, w), lambda b, j, t: (b, j))
    return pl.pallas_call(
        functools.partial(_diff_attn_latent_kernel, heads=heads, lam_init=lam_init),
        grid=(DEC_BATCH, DIFF_HEADS // heads, nq),
        in_specs=[pl.BlockSpec((tq, w), lambda b, j, t: (NP // tq + b * nq + t, j)), lat, ctx, lat, ctx]
        + [_vec_spec(DIFF_HEAD_DIM)] * 4 + [_vec_spec(DIFF_V_DIM)],
        out_specs=pl.BlockSpec((tq, w), lambda b, j, t: (b * nq + t, j)),
        out_shape=jax.ShapeDtypeStruct((NS, D_MODEL), BF16),
        compiler_params=_cparams(("arbitrary", "arbitrary", "arbitrary")),
        name="diff_attn_latent",
    )(q, k, k_ctx, v, v_ctx, *lams, out_norm)


def _router_kernel(x_ref, g_ref, sh_ref, sc_ref, w_ref, b_ref, h_ref, ids_ref, gates_ref):
    h = _modulate(x_ref[...], g_ref[...], sh_ref[...], sc_ref[...])
    h_hi = h.astype(BF16)
    h_hi32 = h_hi.astype(F32)
    bits = lax.bitcast_convert_type(h_hi32, U32)
    h_ref[...] = (bits[:, :HALF_D] >> 16) | (bits[:, HALF_D:] & jnp.uint32(0xFFFF0000))

    h_lo = (h - h_hi32).astype(BF16)
    two = _dot(h_hi, w_ref[...])
    logits = two[:, :ROUTER_LANES] + two[:, ROUTER_LANES:] + _dot(h_lo, w_ref[:, :ROUTER_LANES]) + b_ref[...]

    lane = lax.broadcasted_iota(I32, logits.shape, 1).astype(F32)
    neg = -jnp.inf
    gl = jnp.where(lane < N_GROUPS, logits, neg)
    gmax = jnp.max(gl, axis=-1, keepdims=True)
    g_sel = jnp.min(jnp.where(gl == gmax, lane, ROUTER_LANES), axis=-1, keepdims=True)
    p_group = 1.0 / jnp.sum(jnp.exp(gl - gmax), axis=-1, keepdims=True)

    lo = N_GROUPS + g_sel * EXPERTS_PER_GROUP
    el = jnp.where(lane >= lo, jnp.where(lane < lo + EXPERTS_PER_GROUP, logits, neg), neg)
    v1 = jnp.max(el, axis=-1, keepdims=True)
    i1 = jnp.min(jnp.where(el == v1, lane, ROUTER_LANES), axis=-1, keepdims=True)
    el2 = jnp.where(lane == i1, neg, el)
    v2 = jnp.max(el2, axis=-1, keepdims=True)
    i2 = jnp.min(jnp.where(el2 == v2, lane, ROUTER_LANES), axis=-1, keepdims=True)
    e2 = jnp.exp(v2 - v1)
    w1 = p_group * (1.0 / (1.0 + e2))
    w2 = p_group * (e2 / (1.0 + e2))
    ids = jnp.where(lane == 0, i1 - N_GROUPS, jnp.where(lane == 1, i2 - N_GROUPS, 0.0))
    ids_ref[...] = ids.astype(I32)
    gates_ref[...] = jnp.where(lane == 0, w1, jnp.where(lane == 1, w2, 0.0))


def _router_call(x, g, mod, w_router2, b_router):
    tm = 256
    return pl.pallas_call(
        _router_kernel,
        grid=(TT // tm,),
        in_specs=[
            pl.BlockSpec((tm, D_MODEL), lambda i: (i, 0)),
            pl.BlockSpec((1, D_MODEL), lambda i: (0, 0)),
            _mod_spec(tm, 3),
            _mod_spec(tm, 4),
            pl.BlockSpec((D_MODEL, 2 * ROUTER_LANES), lambda i: (0, 0)),
            pl.BlockSpec((1, ROUTER_LANES), lambda i: (0, 0)),
        ],
        out_specs=[
            pl.BlockSpec((tm, HALF_D), lambda i: (i, 0)),
            pl.BlockSpec((tm, ROUTER_LANES), lambda i: (i, 0)),
            pl.BlockSpec((tm, ROUTER_LANES), lambda i: (i, 0)),
        ],
        out_shape=[
            jax.ShapeDtypeStruct((TT, HALF_D), U32),
            jax.ShapeDtypeStruct((TT, ROUTER_LANES), I32),
            jax.ShapeDtypeStruct((TT, ROUTER_LANES), F32),
        ],
        compiler_params=_cparams(("arbitrary",)),
        name="moe_router",
    )(x, g, mod, mod, w_router2, b_router)


def _moe_ffn_kernel(tok_ref, be_ref, nb_ref, h_hbm, wg_hbm, wu_hbm, wd_hbm, ys_hbm,
                    xrows, x_lo, x_hi, wg_buf, wu_buf, wd_buf, hid, stage,
                    row_sem, gu_sem, dn_sem, out_sem, *, layer):
    b = pl.program_id(0)
    nb = nb_ref[0]

    def row_copy(blk, r):
        return pltpu.make_async_copy(h_hbm.at[pl.ds(tok_ref[blk * MOE_ROWS + r], 1), :],
                                     xrows.at[pl.ds(r, 1), :], row_sem)

    def gu_slot(blk, c):
        return (blk * GU_CHUNKS + c) % GU_SLOTS

    def gu_copies(blk, c):
        e, s, cols = be_ref[blk], gu_slot(blk, c), pl.ds(c * GU_COLS, GU_COLS)
        return (pltpu.make_async_copy(wg_hbm.at[layer, e, :, cols], wg_buf.at[s], gu_sem.at[s]),
                pltpu.make_async_copy(wu_hbm.at[layer, e, :, cols], wu_buf.at[s], gu_sem.at[s]))

    def dn_copy(blk, c):
        s = c % DN_SLOTS
        return pltpu.make_async_copy(wd_hbm.at[layer, be_ref[blk], :, pl.ds(c * DN_COLS, DN_COLS)],
                                     wd_buf.at[s], dn_sem.at[s])

    def out_copy(blk, c, src_slot):
        return pltpu.make_async_copy(
            stage.at[src_slot], ys_hbm.at[pl.ds(blk * MOE_ROWS, MOE_ROWS), pl.ds(c * DN_COLS, DN_COLS)],
            out_sem.at[src_slot])

    def request_next_rows(first, count):
        for r in range(count):
            row_copy(b + 1, first + r).start()

    @pl.when(b == 0)
    def _():
        for c in range(2):
            for cp in gu_copies(0, c):
                cp.start()

        def start(r, carry):
            row_copy(0, r).start()
            return carry
        lax.fori_loop(0, MOE_ROWS, start, 0)

    @pl.when(b <= nb)
    def _():
        def wait(g, carry):
            for r in range(ROW_WAIT_GROUP):
                row_copy(b, g * ROW_WAIT_GROUP + r).wait()
            return carry
        lax.fori_loop(0, MOE_ROWS // ROW_WAIT_GROUP, wait, 0)

    @pl.when(b < nb)
    def _():
        words = xrows[...]
        x_lo[...] = lax.bitcast_convert_type(words << 16, F32).astype(BF16)
        x_hi[...] = lax.bitcast_convert_type(words & jnp.uint32(0xFFFF0000), F32).astype(BF16)

        for c in range(GU_CHUNKS):
            if c + 2 < GU_CHUNKS:
                for cp in gu_copies(b, c + 2):
                    cp.start()
            else:
                first_dn = 2 * (c + 2 - GU_CHUNKS)
                dn_copy(b, first_dn).start()
                dn_copy(b, first_dn + 1).start()
            request_next_rows(c * GU_PHASE_ROWS, GU_PHASE_ROWS)
            for cp in gu_copies(b, c):
                cp.wait()
            s = gu_slot(b, c)

            def proj(w_buf):
                w = w_buf[s].astype(BF16)
                return _dot(x_lo[...], w[:HALF_D, :]) + _dot(x_hi[...], w[HALF_D:, :])

            gate = proj(wg_buf)
            up = proj(wu_buf)
            hid[:, c * GU_COLS:(c + 1) * GU_COLS] = (gate * jax.nn.sigmoid(gate) * up).astype(BF16)

        for c in range(DN_CHUNKS):
            if 1 <= c and c + DN_SLOTS - 1 < DN_CHUNKS:
                dn_copy(b, c + DN_SLOTS - 1).start()
            if c in (DN_CHUNKS - 3, DN_CHUNKS - 2):
                nxt = c - (DN_CHUNKS - 3)

                @pl.when(b + 1 < nb)
                def _():
                    for cp in gu_copies(b + 1, nxt):
                        cp.start()
            request_next_rows(GU_CHUNKS * GU_PHASE_ROWS + c * DN_PHASE_ROWS, DN_PHASE_ROWS)
            dn_copy(b, c).wait()
            acc = _dot(hid[...], wd_buf[c % DN_SLOTS].astype(BF16))
            if c >= 2:
                out_copy(b, c - 2, c % 2).wait()
            else:
                pl.when(b > 0)(lambda: out_copy(b - 1, DN_CHUNKS - 2 + c, c % 2).wait())
            stage[c % 2] = acc
            out_copy(b, c, c % 2).start()

    @pl.when(b >= nb)
    def _():
        @pl.when(b == nb)
        def _():
            for c in (DN_CHUNKS - 2, DN_CHUNKS - 1):
                out_copy(b - 1, c, c % 2).wait()

        stage[0] = jnp.zeros(stage.shape[1:], F32)
        for c in range(DN_CHUNKS):
            out_copy(b, c, 0).start()
        for c in range(DN_CHUNKS):
            out_copy(b, c, 0).wait()


def _moe_ffn_call(layer, slot_tok, block_e, n_blocks, h_packed, w_gate, w_up, w_down):
    any_spec = pl.BlockSpec(memory_space=pl.ANY)
    return pl.pallas_call(
        functools.partial(_moe_ffn_kernel, layer=layer),
        grid_spec=pltpu.PrefetchScalarGridSpec(
            num_scalar_prefetch=3,
            grid=(N_BLOCKS,),
            in_specs=[any_spec] * 4,
            out_specs=any_spec,
            scratch_shapes=[
                pltpu.VMEM((MOE_ROWS, HALF_D), U32),
                pltpu.VMEM((MOE_ROWS, HALF_D), BF16),
                pltpu.VMEM((MOE_ROWS, HALF_D), BF16),
                pltpu.VMEM((GU_SLOTS, D_MODEL, GU_COLS), F32),
                pltpu.VMEM((GU_SLOTS, D_MODEL, GU_COLS), F32),
                pltpu.VMEM((DN_SLOTS, EXPERT_HIDDEN, DN_COLS), F32),
                pltpu.VMEM((MOE_ROWS, EXPERT_HIDDEN), BF16),
                pltpu.VMEM((2, MOE_ROWS, DN_COLS), F32),
                pltpu.SemaphoreType.DMA(()),
                pltpu.SemaphoreType.DMA((GU_SLOTS,)),
                pltpu.SemaphoreType.DMA((DN_SLOTS,)),
                pltpu.SemaphoreType.DMA((2,)),
            ],
        ),
        out_shape=jax.ShapeDtypeStruct((N_SLOTS, D_MODEL), F32),
        compiler_params=_cparams(("arbitrary",)),
        name="moe_experts",
    )(slot_tok, block_e, n_blocks, h_packed, w_gate, w_up, w_down)


def _combine_kernel(pos_ref, ys_hbm, x_ref, gates_ref, gate2_ref, *rest, tm, unroll, emit_next, split_out):
    if emit_next:
        ng_ref, nsh_ref, nsc_ref = rest[:3]
        rest = rest[3:]
    outs, (buf, sem) = rest[:-2], rest[-2:]
    i = pl.program_id(0)
    base = i * (tm * TOP_K)

    def copies(r):
        return [pltpu.make_async_copy(ys_hbm.at[pl.ds(pos_ref[base + TOP_K * r + k], 1), :],
                                      buf.at[k, pl.ds(r, 1), :], sem) for k in range(TOP_K)]

    def start(t, c):
        for u in range(unroll):
            for cp in copies(t * unroll + u):
                cp.start()
        return c

    def wait(t, c):
        for u in range(unroll):
            for cp in copies(t * unroll + u):
                cp.wait()
        return c

    lax.fori_loop(0, tm // unroll, start, 0)
    lax.fori_loop(0, tm // unroll, wait, 0)
    gates = gates_ref[...]
    y = gates[:, 0:1] * buf[0] + gates[:, 1:2] * buf[1]
    x = x_ref[...] + gate2_ref[...] * y
    if split_out:
        is_prompt = i < NP // tm

        @pl.when(is_prompt)
        def _():
            outs[0][...] = x

        @pl.when(jnp.logical_not(is_prompt))
        def _():
            outs[1][...] = x
    else:
        outs[0][...] = x
    if emit_next:
        outs[-1][...] = _modulate(x, ng_ref[...], nsh_ref[...], nsc_ref[...]).astype(BF16)


def _combine_call(pos, ys, x, gates, mod, next_norm=None, split_out=False):
    tm, unroll = 256, 8
    npt = NP // tm
    row = lambda i, pos: (i, 0)
    in_specs = [
        pl.BlockSpec(memory_space=pl.ANY),
        pl.BlockSpec((tm, D_MODEL), row),
        pl.BlockSpec((tm, ROUTER_LANES), row),
        pl.BlockSpec((None, 1, D_MODEL), lambda i, pos: (_cond_row(i * tm), 0, 5)),
    ]
    args = [ys, x, gates, mod]
    if next_norm is not None:
        g_next, mod_next = next_norm
        in_specs += [
            pl.BlockSpec((1, D_MODEL), lambda i, pos: (0, 0)),
            pl.BlockSpec((None, 1, D_MODEL), lambda i, pos: (_cond_row(i * tm), 0, 0)),
            pl.BlockSpec((None, 1, D_MODEL), lambda i, pos: (_cond_row(i * tm), 0, 1)),
        ]
        args += [g_next, mod_next, mod_next]
    if split_out:
        out_specs = [pl.BlockSpec((tm, D_MODEL), lambda i, pos: (jnp.minimum(i, npt - 1), 0)),
                     pl.BlockSpec((tm, D_MODEL), lambda i, pos: (jnp.maximum(i - npt, 0), 0))]
        out_shape = [jax.ShapeDtypeStruct((NP, D_MODEL), F32), jax.ShapeDtypeStruct((NS, D_MODEL), F32)]
    else:
        out_specs = [pl.BlockSpec((tm, D_MODEL), row)]
        out_shape = [jax.ShapeDtypeStruct((TT, D_MODEL), F32)]
    if next_norm is not None:
        out_specs.append(pl.BlockSpec((tm, D_MODEL), row))
        out_shape.append(jax.ShapeDtypeStruct((TT, D_MODEL), BF16))
    return pl.pallas_call(
        functools.partial(_combine_kernel, tm=tm, unroll=unroll, emit_next=next_norm is not None,
                          split_out=split_out),
        grid_spec=pltpu.PrefetchScalarGridSpec(
            num_scalar_prefetch=1,
            grid=(TT // tm,),
            in_specs=in_specs,
            out_specs=out_specs,
            scratch_shapes=[pltpu.VMEM((TOP_K, tm, D_MODEL), F32), pltpu.SemaphoreType.DMA(())],
        ),
        out_shape=out_shape,
        compiler_params=_cparams(("arbitrary",)),
        name="moe_combine",
    )(pos, *args)


def _dispatch_plan(ids):
    flat_e = ids[:, :TOP_K].reshape(N_ASSIGN)
    onehot = (flat_e[:, None] == jnp.arange(N_EXPERTS, dtype=I32)[None, :]).astype(I32)
    csum = jnp.cumsum(onehot, axis=0)
    rank = jnp.take_along_axis(csum, flat_e[:, None], axis=1)[:, 0] - 1
    counts = csum[-1]
    padded = (counts + MOE_ROWS - 1) // MOE_ROWS * MOE_ROWS
    pad_ends = jnp.cumsum(padded)
    pad_starts = pad_ends - padded
    dest = (pad_starts[flat_e] + rank).astype(I32)
    slot_tok = jnp.zeros((N_SLOTS,), I32).at[dest].set(jnp.arange(N_ASSIGN, dtype=I32) // TOP_K)
    n_blocks = (pad_ends[-1] // MOE_ROWS).astype(I32)
    blk = jnp.minimum(jnp.arange(N_BLOCKS, dtype=I32), n_blocks - 1)
    block_e = jnp.minimum(jnp.searchsorted(pad_ends, blk * MOE_ROWS, side="right"), N_EXPERTS - 1).astype(I32)
    return dest, slot_tok, block_e, n_blocks.reshape(1)


def _moe(layer, x, g, mod, w_router2, b_router, w_gate, w_up, w_down, **combine_kw):
    h_packed, ids, gates = _router_call(x, g, mod, w_router2, b_router)
    dest, slot_tok, block_e, n_blocks = _dispatch_plan(ids)
    ys = _moe_ffn_call(layer, slot_tok, block_e, n_blocks, h_packed, w_gate, w_up, w_down)
    return _combine_call(dest, ys, x, gates, mod, **combine_kw)


def _rope_tables(rot_dim):
    rows = DEC_SEQ // GRID_W
    nf = rot_dim // 4
    inv = jnp.exp(-math.log(ROPE_THETA) * jnp.arange(nf, dtype=F32) / nf)
    row_pos = jnp.repeat(jnp.arange(rows, dtype=F32), GRID_W)
    col_pos = jnp.tile(jnp.arange(GRID_W, dtype=F32), rows)
    ang = jnp.stack([row_pos[:, None] * inv, col_pos[:, None] * inv], axis=1)
    cos = jnp.broadcast_to(jnp.cos(ang)[:, :, None, :], (DEC_SEQ, 2, 2, nf)).reshape(DEC_SEQ, rot_dim)
    sgn = jnp.array([-1.0, 1.0], F32)[None, None, :, None]
    sin = (jnp.sin(ang)[:, :, None, :] * sgn).reshape(DEC_SEQ, rot_dim)
    pad = 128 - rot_dim
    if pad:
        cos = jnp.concatenate([cos, jnp.ones((DEC_SEQ, pad), F32)], axis=1)
        sin = jnp.concatenate([sin, jnp.zeros((DEC_SEQ, pad), F32)], axis=1)
    return cos, sin


def _pad_head_vec(v):
    return jnp.concatenate([v, jnp.zeros((MLA_HEAD_PAD - MLA_QK,), F32)]).reshape(1, MLA_HEAD_PAD)


def _router_weights(w_group, b_group, w_router, b_router):
    pad = ROUTER_LANES - N_GROUPS - N_EXPERTS
    w = jnp.concatenate([w_group, w_router, jnp.zeros((D_MODEL, pad), F32)], axis=1)
    w_hi = w.astype(BF16)
    w_lo = (w - w_hi.astype(F32)).astype(BF16)
    b = jnp.concatenate([b_group, b_router, jnp.zeros((pad,), F32)]).reshape(1, ROUTER_LANES)
    return jnp.concatenate([w_hi, w_lo], axis=1), b


def kernel(x_prompt, x_sample, cache_mla_ckv, cache_mla_krope, cache_diff_k, cache_diff_v, c, c_ctx, norm1_g, norm2_g, ada_w, ada_b, mla_w_in, mla_q_a_norm, mla_w_q_up, mla_kv_a_norm, mla_w_kv_up, mla_q_norm, mla_k_norm, mla_w_out, diff_w_qkv, diff_q_norm, diff_k_norm, diff_lambda_q1, diff_lambda_k1, diff_lambda_q2, diff_lambda_k2, diff_out_norm, diff_w_out, moe_w_group, moe_b_group, moe_w_router, moe_b_router, moe_w_gate, moe_w_up, moe_w_down):
    assert DEPTH == 2
    xp0 = x_prompt.reshape(NP, D_MODEL)
    xs0 = x_sample.reshape(NS, D_MODEL)
    cond = jnp.concatenate([c_ctx[None, :], c, jnp.zeros((N_COND - 1 - DEC_BATCH, D_MODEL), F32)], axis=0)
    mods = _ada_call(cond, ada_w, ada_b).reshape(DEPTH, N_COND, 1, 6 * D_MODEL)
    routers = [_router_weights(moe_w_group[i], moe_b_group[i], moe_w_router[i], moe_b_router[i])
               for i in range(DEPTH)]
    g2 = [norm2_g[i].reshape(1, D_MODEL) for i in range(DEPTH)]

    mod = mods[0]
    cos, sin = _rope_tables(MLA_ROPE)
    w_in = jnp.pad(mla_w_in[0], ((0, 0), (0, MLA_IN_PAD - MLA_IN))).astype(BF16)
    w_q = jnp.pad(mla_w_q_up[0].reshape(MLA_Q_LORA, MLA_HEADS, MLA_QK), ((0, 0), (0, 0), (0, MLA_HEAD_PAD - MLA_QK)))
    w_q = w_q.reshape(MLA_Q_LORA, MLA_HEADS * MLA_HEAD_PAD).astype(BF16)
    cq, ckv, kr = _mla_in_call(xp0, xs0, norm1_g[0].reshape(1, D_MODEL), mod, w_in,
                               mla_q_a_norm[0].reshape(1, -1), mla_kv_a_norm[0].reshape(1, -1))
    q = _q_up_call(cq, w_q, _pad_head_vec(mla_q_norm[0]), cos, sin)
    ckv_all = jnp.concatenate([ckv, cache_mla_ckv[:, 0].reshape(NCTX, MLA_KV_LORA)], axis=0)
    kr_ctx = jnp.pad(cache_mla_krope[:, 0].reshape(NCTX, MLA_ROPE), ((0, 0), (0, 128 - MLA_ROPE)))
    kr_all = jnp.concatenate([kr, kr_ctx], axis=0)
    k, v = _kv_up_call(ckv_all, kr_all, mla_w_kv_up[0].astype(BF16), _pad_head_vec(mla_k_norm[0]), cos, sin)
    o_p = _mla_attn_prompt_call(q, k, v)
    o_s = _mla_attn_latent_call(q, k, v)
    x = _out_proj_call(o_p, o_s, mla_w_out[0].astype(BF16), xp0, xs0, True, mod)
    new_ckv = ckv[:NP].reshape(BATCH, 1, SEQ, MLA_KV_LORA)
    new_krope = kr[:NP, :MLA_ROPE].reshape(BATCH, 1, SEQ, MLA_ROPE)
    x, h = _moe(0, x, g2[0], mod, *routers[0], moe_w_gate, moe_w_up, moe_w_down,
                next_norm=(norm1_g[1].reshape(1, D_MODEL), mods[1]))

    mod = mods[1]
    lam_init = 0.8 - 0.6 * math.exp(-0.3 * 1)
    cos, sin = _rope_tables(DIFF_HEAD_DIM)
    q, k_p, k_s, v_p, v_s = _diff_qkv_call(h, diff_w_qkv[0].astype(BF16), diff_q_norm[0].reshape(1, -1),
                                           diff_k_norm[0].reshape(1, -1), cos, sin)
    lams = [t[0].reshape(1, DIFF_HEAD_DIM) for t in
            (diff_lambda_q1, diff_lambda_k1, diff_lambda_q2, diff_lambda_k2)]
    out_norm = diff_out_norm[0].reshape(1, DIFF_V_DIM)
    o_p = _diff_attn_prompt_call(q, k_p, v_p, lams, out_norm, lam_init)
    o_s = _diff_attn_latent_call(q, k_s, v_s, cache_diff_k[:, 0].reshape(NCTX, D_MODEL),
                                 cache_diff_v[:, 0].reshape(NCTX, D_MODEL), lams, out_norm, lam_init)
    x = _out_proj_call(o_p, o_s, diff_w_out[0].astype(BF16), x, x, False, mod)
    new_k = k_p.reshape(BATCH, 1, SEQ, DIFF_HEADS, 2, DIFF_HEAD_DIM)
    new_v = v_p.reshape(BATCH, 1, SEQ, DIFF_HEADS, DIFF_V_DIM)
    y_p, y_s = _moe(1, x, g2[1], mod, *routers[1], moe_w_gate, moe_w_up, moe_w_down, split_out=True)

    return (y_p.reshape(BATCH, SEQ, D_MODEL), y_s.reshape(DEC_BATCH, DEC_SEQ, D_MODEL),
            new_ckv, new_krope, new_k, new_v)
```

```python
import functools
import math

import jax
import jax.numpy as jnp
from jax import lax
from jax.experimental import pallas as pl
from jax.experimental.pallas import tpu as pltpu

F32 = jnp.float32
BF16 = jnp.bfloat16
I32 = jnp.int32
U32 = jnp.uint32

D_MODEL = 4096
BATCH = 32
SEQ = 256
DEPTH = 2
DEC_BATCH = 4
DEC_SEQ = 1024
PAST_LEN = 256
GRID_W = 64
MLA_HEADS = 32
MLA_NOPE = 128
MLA_ROPE = 64
MLA_QK = MLA_NOPE + MLA_ROPE
MLA_V = 128
MLA_Q_LORA = 1024
MLA_KV_LORA = 512
MLA_IN = MLA_Q_LORA + MLA_KV_LORA + MLA_ROPE
MLA_IN_PAD = MLA_Q_LORA + MLA_KV_LORA + 128
MLA_HEAD_PAD = 256
DIFF_HEAD_DIM = 128
DIFF_HEADS = 16
DIFF_V_DIM = 256
N_GROUPS = 8
EXPERTS_PER_GROUP = 8
N_EXPERTS = 64
TOP_K = 2
EXPERT_HIDDEN = 1024
ROPE_THETA = 10000.0
NORM_EPS = 1e-6

NP = BATCH * SEQ
NS = DEC_BATCH * DEC_SEQ
TT = NP + NS
NCTX = DEC_BATCH * PAST_LEN
N_COND = 8
N_ASSIGN = TT * TOP_K
ROUTER_LANES = 128
HALF_D = D_MODEL // 2

MOE_ROWS = 512
GU_ROWS = 512
GU_CHUNKS = D_MODEL // GU_ROWS
GU_SLOTS = 3
DN_COLS = 512
DN_CHUNKS = D_MODEL // DN_COLS
DN_SLOTS = 4
GU_PHASE_ROWS = 32
DN_PHASE_ROWS = (MOE_ROWS - GU_CHUNKS * GU_PHASE_ROWS) // DN_CHUNKS
ROW_WAIT_GROUP = 64
N_BLOCKS = N_ASSIGN // MOE_ROWS + N_EXPERTS + 1
N_SLOTS = N_BLOCKS * MOE_ROWS

VMEM_LIMIT = 56 * 1024 * 1024


def _cparams(sem):
    return pltpu.CompilerParams(dimension_semantics=sem, vmem_limit_bytes=VMEM_LIMIT)


def _cond_row(row0):
    return jnp.where(row0 < NP, 0, 1 + (row0 - NP) // DEC_SEQ)


def _rms(x, g):
    return x * lax.rsqrt(jnp.mean(x * x, axis=-1, keepdims=True) + NORM_EPS) * g


def _modulate(x, g, shift, scale):
    return _rms(x, g) * (1.0 + scale) + shift


def _dot(a, b):
    return jnp.dot(a, b, preferred_element_type=F32)


def _dot_nt(a, b):
    return lax.dot_general(a, b, (((1,), (1,)), ((), ())), preferred_element_type=F32)


def _rope128(x, cos, sin_signed, half):
    lane = lax.broadcasted_iota(I32, x.shape, 1)
    first_half = ((lane // half) % 2) == 0
    partner = jnp.where(first_half, pltpu.roll(x, 128 - half, 1), pltpu.roll(x, half, 1))
    return x * cos + partner * sin_signed


def _latent_tables(cos_ref, sin_ref, is_latent):
    m = is_latent.astype(F32)
    return 1.0 + m * (cos_ref[...] - 1.0), m * sin_ref[...]


def _row_pair_specs(tm, tn, split):
    npt = NP // tm
    off = npt if split else 0
    last = D_MODEL // tn - 1

    def prompt_map(i, *a):
        return (jnp.minimum(i, npt - 1), jnp.where(i < npt, a[0], last) if last else 0)

    def latent_map(i, *a):
        return (jnp.maximum(i, npt) - off, jnp.where(i >= npt, a[0], 0) if last else 0)

    return pl.BlockSpec((tm, tn), prompt_map), pl.BlockSpec((tm, tn), latent_map)


def _ada_kernel(c_ref, w_ref, b_ref, o_ref):
    c = c_ref[...]
    a = (c * jax.nn.sigmoid(c)).astype(BF16)
    o_ref[...] = _dot(a, w_ref[...].astype(BF16)) + b_ref[...]


def _ada_call(cond, ada_w, ada_b):
    tn = 512
    n = 6 * D_MODEL
    return pl.pallas_call(
        _ada_kernel,
        grid=(DEPTH, n // tn),
        in_specs=[
            pl.BlockSpec((N_COND, D_MODEL), lambda l, j: (0, 0)),
            pl.BlockSpec((None, D_MODEL, tn), lambda l, j: (l, 0, j)),
            pl.BlockSpec((None, 1, tn), lambda l, j: (l, 0, j)),
        ],
        out_specs=pl.BlockSpec((None, N_COND, tn), lambda l, j: (l, 0, j)),
        out_shape=jax.ShapeDtypeStruct((DEPTH, N_COND, n), F32),
        compiler_params=_cparams(("arbitrary", "arbitrary")),
        name="ada_table",
    )(cond, ada_w, ada_b.reshape(DEPTH, 1, n))


def _mod_spec(tm, chunk, width=D_MODEL):
    per = D_MODEL // width
    return pl.BlockSpec((None, 1, width),
                        lambda i, *a: (_cond_row(i * tm), 0, chunk * per + (a[0] if per > 1 else 0)))


def _mla_in_kernel(xp_ref, xs_ref, g_ref, sh_ref, sc_ref, w_ref, qg_ref, kvg_ref, cq_ref, ckv_ref, kr_ref, *, tm):
    def run(x_ref):
        h = _modulate(x_ref[...], g_ref[...], sh_ref[...], sc_ref[...]).astype(BF16)
        p = _dot(h, w_ref[...])
        cq_ref[...] = _rms(p[:, :MLA_Q_LORA], qg_ref[...]).astype(BF16)
        ckv_ref[...] = _rms(p[:, MLA_Q_LORA:MLA_Q_LORA + MLA_KV_LORA], kvg_ref[...])
        kr_ref[...] = p[:, MLA_Q_LORA + MLA_KV_LORA:]

    is_prompt = pl.program_id(0) < NP // tm
    pl.when(is_prompt)(lambda: run(xp_ref))
    pl.when(jnp.logical_not(is_prompt))(lambda: run(xs_ref))


def _mla_in_call(x_prompt, x_latent, g, mod, w_in, q_a_norm, kv_a_norm):
    tm = 256
    xp_spec, xs_spec = _row_pair_specs(tm, D_MODEL, split=True)
    return pl.pallas_call(
        functools.partial(_mla_in_kernel, tm=tm),
        grid=(TT // tm,),
        in_specs=[
            xp_spec,
            xs_spec,
            pl.BlockSpec((1, D_MODEL), lambda i: (0, 0)),
            _mod_spec(tm, 0),
            _mod_spec(tm, 1),
            pl.BlockSpec((D_MODEL, MLA_IN_PAD), lambda i: (0, 0)),
            pl.BlockSpec((1, MLA_Q_LORA), lambda i: (0, 0)),
            pl.BlockSpec((1, MLA_KV_LORA), lambda i: (0, 0)),
        ],
        out_specs=[
            pl.BlockSpec((tm, MLA_Q_LORA), lambda i: (i, 0)),
            pl.BlockSpec((tm, MLA_KV_LORA), lambda i: (i, 0)),
            pl.BlockSpec((tm, 128), lambda i: (i, 0)),
        ],
        out_shape=[
            jax.ShapeDtypeStruct((TT, MLA_Q_LORA), BF16),
            jax.ShapeDtypeStruct((TT, MLA_KV_LORA), F32),
            jax.ShapeDtypeStruct((TT, 128), F32),
        ],
        compiler_params=_cparams(("arbitrary",)),
        name="mla_in",
    )(x_prompt, x_latent, g, mod, mod, w_in, q_a_norm, kv_a_norm)


def _q_up_kernel(c_ref, w_ref, g_ref, cos_ref, sin_ref, q_ref, *, tm, heads):
    i = pl.program_id(0)
    cos, sin = _latent_tables(cos_ref, sin_ref, i >= NP // tm)
    o = _dot(c_ref[...], w_ref[...])
    g = g_ref[...] * (MLA_QK ** -0.5)
    for hh in range(heads):
        s = o[:, hh * 256:(hh + 1) * 256]
        r = lax.rsqrt(jnp.sum(s * s, axis=-1, keepdims=True) / MLA_QK + NORM_EPS)
        lo = s[:, :128] * r * g[:, :128]
        hi = _rope128(s[:, 128:] * r * g[:, 128:], cos, sin, MLA_ROPE // 4)
        q_ref[:, hh * 256:hh * 256 + 128] = lo.astype(BF16)
        q_ref[:, hh * 256 + 128:(hh + 1) * 256] = hi.astype(BF16)


def _pos_block(i, tm):
    return jnp.clip(i - NP // tm, 0, NS // tm - 1) % (DEC_SEQ // tm)


def _q_up_call(cq, w_q, q_norm_pad, cos, sin):
    tm, heads = 512, 8
    tn = heads * MLA_HEAD_PAD
    return pl.pallas_call(
        functools.partial(_q_up_kernel, tm=tm, heads=heads),
        grid=(TT // tm, MLA_HEADS // heads),
        in_specs=[
            pl.BlockSpec((tm, MLA_Q_LORA), lambda i, j: (i, 0)),
            pl.BlockSpec((MLA_Q_LORA, tn), lambda i, j: (0, j)),
            pl.BlockSpec((1, MLA_HEAD_PAD), lambda i, j: (0, 0)),
            pl.BlockSpec((tm, 128), lambda i, j: (_pos_block(i, tm), 0)),
            pl.BlockSpec((tm, 128), lambda i, j: (_pos_block(i, tm), 0)),
        ],
        out_specs=pl.BlockSpec((tm, tn), lambda i, j: (i, j)),
        out_shape=jax.ShapeDtypeStruct((TT, MLA_HEADS * MLA_HEAD_PAD), BF16),
        compiler_params=_cparams(("arbitrary", "arbitrary")),
        name="mla_q_up",
    )(cq, w_q, q_norm_pad, cos, sin)


def _kv_up_kernel(c_ref, kr_ref, w_ref, g_ref, cos_ref, sin_ref, k_ref, v_ref, *, tm, heads):
    i = pl.program_id(0)
    cos, sin = _latent_tables(cos_ref, sin_ref, (i >= NP // tm) & (i < TT // tm))
    kv = _dot(c_ref[...].astype(BF16), w_ref[...])
    g = g_ref[...]
    kr = kr_ref[...]
    ss_rope = jnp.sum(kr * kr, axis=-1, keepdims=True)
    base = _rope128(kr * g[:, 128:], cos, sin, MLA_ROPE // 4)
    for hh in range(heads):
        kn = kv[:, hh * 256:hh * 256 + 128]
        r = lax.rsqrt((jnp.sum(kn * kn, axis=-1, keepdims=True) + ss_rope) / MLA_QK + NORM_EPS)
        k_ref[:, hh * 256:hh * 256 + 128] = (kn * r * g[:, :128]).astype(BF16)
        k_ref[:, hh * 256 + 128:(hh + 1) * 256] = (base * r).astype(BF16)
        v_ref[:, hh * 128:(hh + 1) * 128] = kv[:, hh * 256 + 128:(hh + 1) * 256].astype(BF16)


def _kv_up_call(ckv_all, kr_all, w_kv, k_norm_pad, cos, sin):
    tm, heads = 512, 8
    rows = TT + NCTX
    return pl.pallas_call(
        functools.partial(_kv_up_kernel, tm=tm, heads=heads),
        grid=(rows // tm, MLA_HEADS // heads),
        in_specs=[
            pl.BlockSpec((tm, MLA_KV_LORA), lambda i, j: (i, 0)),
            pl.BlockSpec((tm, 128), lambda i, j: (i, 0)),
            pl.BlockSpec((MLA_KV_LORA, heads * 256), lambda i, j: (0, j)),
            pl.BlockSpec((1, MLA_HEAD_PAD), lambda i, j: (0, 0)),
            pl.BlockSpec((tm, 128), lambda i, j: (_pos_block(i, tm), 0)),
            pl.BlockSpec((tm, 128), lambda i, j: (_pos_block(i, tm), 0)),
        ],
        out_specs=[
            pl.BlockSpec((tm, heads * MLA_HEAD_PAD), lambda i, j: (i, j)),
            pl.BlockSpec((tm, heads * MLA_V), lambda i, j: (i, j)),
        ],
        out_shape=[
            jax.ShapeDtypeStruct((rows, MLA_HEADS * MLA_HEAD_PAD), BF16),
            jax.ShapeDtypeStruct((rows, MLA_HEADS * MLA_V), BF16),
        ],
        compiler_params=_cparams(("arbitrary", "arbitrary")),
        name="mla_kv_up",
    )(ckv_all, kr_all, w_kv, k_norm_pad, cos, sin)


def _mla_attn_prompt_kernel(q_ref, k_ref, v_ref, o_ref, *, heads):
    for hh in range(heads):
        q = q_ref[:, hh * 256:(hh + 1) * 256]
        k = k_ref[:, hh * 256:(hh + 1) * 256]
        s = _dot_nt(q, k)
        p = jnp.exp(s - jnp.max(s, axis=-1, keepdims=True))
        l = jnp.sum(p, axis=-1, keepdims=True)
        o = _dot(p.astype(BF16), v_ref[:, hh * 128:(hh + 1) * 128]) / l
        o_ref[:, hh * 128:(hh + 1) * 128] = o.astype(BF16)


def _mla_attn_prompt_call(q, k, v):
    heads = 8
    return pl.pallas_call(
        functools.partial(_mla_attn_prompt_kernel, heads=heads),
        grid=(BATCH, MLA_HEADS // heads),
        in_specs=[
            pl.BlockSpec((SEQ, heads * 256), lambda b, j: (b, j)),
            pl.BlockSpec((SEQ, heads * 256), lambda b, j: (b, j)),
            pl.BlockSpec((SEQ, heads * 128), lambda b, j: (b, j)),
        ],
        out_specs=pl.BlockSpec((SEQ, heads * 128), lambda b, j: (b, j)),
        out_shape=jax.ShapeDtypeStruct((NP, MLA_HEADS * MLA_V), BF16),
        compiler_params=_cparams(("arbitrary", "arbitrary")),
        name="mla_attn_prompt",
    )(q, k, v)


def _mla_attn_latent_kernel(q_ref, kl_ref, kc_ref, vl_ref, vc_ref, o_ref, *, heads):
    for hh in range(heads):
        q = q_ref[:, hh * 256:(hh + 1) * 256]
        sl = _dot_nt(q, kl_ref[:, hh * 256:(hh + 1) * 256])
        sc = _dot_nt(q, kc_ref[:, hh * 256:(hh + 1) * 256])
        m = jnp.maximum(jnp.max(sl, axis=-1, keepdims=True), jnp.max(sc, axis=-1, keepdims=True))
        pl_ = jnp.exp(sl - m)
        pc = jnp.exp(sc - m)
        l = jnp.sum(pl_, axis=-1, keepdims=True) + jnp.sum(pc, axis=-1, keepdims=True)
        o = _dot(pl_.astype(BF16), vl_ref[:, hh * 128:(hh + 1) * 128])
        o = o + _dot(pc.astype(BF16), vc_ref[:, hh * 128:(hh + 1) * 128])
        o_ref[:, hh * 128:(hh + 1) * 128] = (o / l).astype(BF16)


def _mla_attn_latent_call(q, k, v):
    heads, tq = 8, 512
    nq = DEC_SEQ // tq
    return pl.pallas_call(
        functools.partial(_mla_attn_latent_kernel, heads=heads),
        grid=(DEC_BATCH, MLA_HEADS // heads, nq),
        in_specs=[
            pl.BlockSpec((tq, heads * 256), lambda b, j, t: (NP // tq + b * nq + t, j)),
            pl.BlockSpec((DEC_SEQ, heads * 256), lambda b, j, t: (NP // DEC_SEQ + b, j)),
            pl.BlockSpec((PAST_LEN, heads * 256), lambda b, j, t: (TT // PAST_LEN + b, j)),
            pl.BlockSpec((DEC_SEQ, heads * 128), lambda b, j, t: (NP // DEC_SEQ + b, j)),
            pl.BlockSpec((PAST_LEN, heads * 128), lambda b, j, t: (TT // PAST_LEN + b, j)),
        ],
        out_specs=pl.BlockSpec((tq, heads * 128), lambda b, j, t: (b * nq + t, j)),
        out_shape=jax.ShapeDtypeStruct((NS, MLA_HEADS * MLA_V), BF16),
        compiler_params=_cparams(("arbitrary", "arbitrary", "arbitrary")),
        name="mla_attn_latent",
    )(q, k, k, v, v)


def _out_proj_kernel(ap_ref, as_ref, w_ref, xp_ref, xs_ref, gate_ref, o_ref, *, tm):
    def finish(a_ref, x_ref):
        o_ref[...] = x_ref[...] + gate_ref[...] * _dot(a_ref[...], w_ref[...])

    is_prompt = pl.program_id(0) < NP // tm
    pl.when(is_prompt)(lambda: finish(ap_ref, xp_ref))
    pl.when(jnp.logical_not(is_prompt))(lambda: finish(as_ref, xs_ref))


def _out_proj_call(a_prompt, a_latent, w, x_prompt, x_latent, x_split, mod):
    tm, tn = 512, 1024
    ap_spec, as_spec = _row_pair_specs(tm, D_MODEL, split=True)
    xp_spec, xs_spec = _row_pair_specs(tm, tn, split=x_split)
    return pl.pallas_call(
        functools.partial(_out_proj_kernel, tm=tm),
        grid=(TT // tm, D_MODEL // tn),
        in_specs=[
            ap_spec,
            as_spec,
            pl.BlockSpec((D_MODEL, tn), lambda i, j: (0, j)),
            xp_spec,
            xs_spec,
            _mod_spec(tm, 2, width=tn),
        ],
        out_specs=pl.BlockSpec((tm, tn), lambda i, j: (i, j)),
        out_shape=jax.ShapeDtypeStruct((TT, D_MODEL), F32),
        compiler_params=_cparams(("arbitrary", "arbitrary")),
        name="attn_out_proj",
    )(a_prompt, a_latent, w, x_prompt, x_latent, mod)


def _diff_qkv_kernel(h_ref, w_ref, qn_ref, kn_ref, cos_ref, sin_ref, q_ref, kp_ref, kl_ref, vp_ref, vl_ref,
                     *, tm, nq, sub):
    i = pl.program_id(0)
    j = pl.program_id(1)
    is_prompt = i < NP // tm
    cos, sin = _latent_tables(cos_ref, sin_ref, jnp.logical_not(is_prompt))
    n_sub = w_ref.shape[1] // sub

    def store_rows(prompt_ref, latent_ref, pieces):
        def fill(ref):
            for c0, y in pieces:
                ref[:, c0:c0 + y.shape[1]] = y
        pl.when(is_prompt)(lambda: fill(prompt_ref))
        pl.when(jnp.logical_not(is_prompt))(lambda: fill(latent_ref))

    def norm_rope(o, g, scale):
        return [_rope128(_rms(o[:, s * 128:(s + 1) * 128], g) * scale, cos, sin, DIFF_HEAD_DIM // 4)
                for s in range(sub // 128)]

    @pl.when(j < nq)
    def _():
        for t in range(n_sub):
            o = _dot(h_ref[...], w_ref[:, t * sub:(t + 1) * sub])
            for s, y in enumerate(norm_rope(o, qn_ref[...], DIFF_HEAD_DIM ** -0.5)):
                q_ref[:, t * sub + s * 128:t * sub + (s + 1) * 128] = y.astype(BF16)

    @pl.when((j >= nq) & (j < 2 * nq))
    def _():
        pieces = []
        for t in range(n_sub):
            o = _dot(h_ref[...], w_ref[:, t * sub:(t + 1) * sub])
            pieces += [(t * sub + s * 128, y) for s, y in enumerate(norm_rope(o, kn_ref[...], 1.0))]
        store_rows(kp_ref, kl_ref, pieces)

    @pl.when(j >= 2 * nq)
    def _():
        store_rows(vp_ref, vl_ref, [(0, _dot(h_ref[...], w_ref[...]))])


def _diff_qkv_call(h, w_qkv, q_norm, k_norm, cos, sin):
    tm, tn, sub = 512, 1024, 256
    nq = D_MODEL // tn
    npt = NP // tm

    def kv_specs(first):
        col = lambda j: jnp.clip(j - first, 0, nq - 1)
        return [pl.BlockSpec((tm, tn), lambda i, j: (jnp.minimum(i, npt - 1), jnp.where(i < npt, col(j), nq - 1))),
                pl.BlockSpec((tm, tn), lambda i, j: (jnp.maximum(i - npt, 0), jnp.where(i >= npt, col(j), 0)))]

    kv_shapes = [jax.ShapeDtypeStruct((NP, D_MODEL), F32), jax.ShapeDtypeStruct((NS, D_MODEL), F32)]
    return pl.pallas_call(
        functools.partial(_diff_qkv_kernel, tm=tm, nq=nq, sub=sub),
        grid=(TT // tm, 3 * nq),
        in_specs=[
            pl.BlockSpec((tm, D_MODEL), lambda i, j: (i, 0)),
            pl.BlockSpec((D_MODEL, tn), lambda i, j: (0, j)),
            pl.BlockSpec((1, DIFF_HEAD_DIM), lambda i, j: (0, 0)),
            pl.BlockSpec((1, DIFF_HEAD_DIM), lambda i, j: (0, 0)),
            pl.BlockSpec((tm, 128), lambda i, j: (_pos_block(i, tm), 0)),
            pl.BlockSpec((tm, 128), lambda i, j: (_pos_block(i, tm), 0)),
        ],
        out_specs=[pl.BlockSpec((tm, tn), lambda i, j: (i, jnp.minimum(j, nq - 1)))]
        + kv_specs(nq) + kv_specs(2 * nq),
        out_shape=[jax.ShapeDtypeStruct((TT, D_MODEL), BF16)] + kv_shapes + kv_shapes,
        compiler_params=_cparams(("arbitrary", "arbitrary")),
        name="diff_qkv",
    )(h, w_qkv, q_norm, k_norm, cos, sin)


def _diff_lambda(lq1_ref, lk1_ref, lq2_ref, lk2_ref, lam_init):
    e1 = jnp.exp(jnp.sum(lq1_ref[...] * lk1_ref[...], axis=-1, keepdims=True))
    e2 = jnp.exp(jnp.sum(lq2_ref[...] * lk2_ref[...], axis=-1, keepdims=True))
    return e1 - e2 + lam_init


def _softmax_rows(s):
    p = jnp.exp(s - jnp.max(s, axis=-1, keepdims=True))
    return p / jnp.sum(p, axis=-1, keepdims=True)


def _diff_attn_prompt_kernel(q_ref, k_ref, v_ref, lq1, lk1, lq2, lk2, on_ref, o_ref, *, heads, lam_init):
    lam = _diff_lambda(lq1, lk1, lq2, lk2, lam_init)
    for hh in range(heads):
        c0 = hh * 256
        p1 = _softmax_rows(_dot_nt(q_ref[:, c0:c0 + 128], k_ref[:, c0:c0 + 128].astype(BF16)))
        p2 = _softmax_rows(_dot_nt(q_ref[:, c0 + 128:c0 + 256], k_ref[:, c0 + 128:c0 + 256].astype(BF16)))
        a = (p1 - lam * p2).astype(BF16)
        o = _dot(a, v_ref[:, c0:c0 + 256].astype(BF16))
        o_ref[:, c0:c0 + 256] = (_rms(o, on_ref[...]) * (1.0 - lam_init)).astype(BF16)


def _vec_spec(n):
    return pl.BlockSpec((1, n), lambda *a: (0, 0))


def _diff_attn_prompt_call(q, k, v, lams, out_norm, lam_init):
    heads = 4
    w = heads * 256
    blk = pl.BlockSpec((SEQ, w), lambda b, j: (b, j))
    return pl.pallas_call(
        functools.partial(_diff_attn_prompt_kernel, heads=heads, lam_init=lam_init),
        grid=(BATCH, DIFF_HEADS // heads),
        in_specs=[blk, blk, blk] + [_vec_spec(DIFF_HEAD_DIM)] * 4 + [_vec_spec(DIFF_V_DIM)],
        out_specs=blk,
        out_shape=jax.ShapeDtypeStruct((NP, D_MODEL), BF16),
        compiler_params=_cparams(("arbitrary", "arbitrary")),
        name="diff_attn_prompt",
    )(q, k, v, *lams, out_norm)


def _softmax2(sa, sb):
    m = jnp.maximum(jnp.max(sa, axis=-1, keepdims=True), jnp.max(sb, axis=-1, keepdims=True))
    pa = jnp.exp(sa - m)
    pb = jnp.exp(sb - m)
    l = jnp.sum(pa, axis=-1, keepdims=True) + jnp.sum(pb, axis=-1, keepdims=True)
    return pa / l, pb / l


def _diff_attn_latent_kernel(q_ref, kl_ref, kc_ref, vl_ref, vc_ref, lq1, lk1, lq2, lk2, on_ref,
                             o_ref, *, heads, lam_init):
    lam = _diff_lambda(lq1, lk1, lq2, lk2, lam_init)
    for hh in range(heads):
        c0 = hh * 256
        q1 = q_ref[:, c0:c0 + 128]
        q2 = q_ref[:, c0 + 128:c0 + 256]
        p1l, p1c = _softmax2(_dot_nt(q1, kl_ref[:, c0:c0 + 128].astype(BF16)),
                             _dot_nt(q1, kc_ref[:, c0:c0 + 128].astype(BF16)))
        p2l, p2c = _softmax2(_dot_nt(q2, kl_ref[:, c0 + 128:c0 + 256].astype(BF16)),
                             _dot_nt(q2, kc_ref[:, c0 + 128:c0 + 256].astype(BF16)))
        o = _dot((p1l - lam * p2l).astype(BF16), vl_ref[:, c0:c0 + 256].astype(BF16))
        o = o + _dot((p1c - lam * p2c).astype(BF16), vc_ref[:, c0:c0 + 256].astype(BF16))
        o_ref[:, c0:c0 + 256] = (_rms(o, on_ref[...]) * (1.0 - lam_init)).astype(BF16)


def _diff_attn_latent_call(q, k, v, k_ctx, v_ctx, lams, out_norm, lam_init):
    heads, tq = 4, 256
    w = heads * 256
    nq = DEC_SEQ // tq
    lat = pl.BlockSpec((DEC_SEQ, w), lambda b, j, t: (b, j))
    ctx = pl.BlockSpec((PAST_LEN, w), lambda b, j, t: (b, j))
    return pl.pallas_call(
        functools.partial(_diff_attn_latent_kernel, heads=heads, lam_init=lam_init),
        grid=(DEC_BATCH, DIFF_HEADS // heads, nq),
        in_specs=[pl.BlockSpec((tq, w), lambda b, j, t: (NP // tq + b * nq + t, j)), lat, ctx, lat, ctx]
        + [_vec_spec(DIFF_HEAD_DIM)] * 4 + [_vec_spec(DIFF_V_DIM)],
        out_specs=pl.BlockSpec((tq, w), lambda b, j, t: (b * nq + t, j)),
        out_shape=jax.ShapeDtypeStruct((NS, D_MODEL), BF16),
        compiler_params=_cparams(("arbitrary", "arbitrary", "arbitrary")),
        name="diff_attn_latent",
    )(q, k, k_ctx, v, v_ctx, *lams, out_norm)


def _router_kernel(x_ref, g_ref, sh_ref, sc_ref, w_ref, b_ref, h_ref, ids_ref, gates_ref):
    h = _modulate(x_ref[...], g_ref[...], sh_ref[...], sc_ref[...])
    h_hi = h.astype(BF16)
    h_hi32 = h_hi.astype(F32)
    bits = lax.bitcast_convert_type(h_hi32, U32)
    h_ref[...] = (bits[:, :HALF_D] >> 16) | (bits[:, HALF_D:] & jnp.uint32(0xFFFF0000))

    h_lo = (h - h_hi32).astype(BF16)
    two = _dot(h_hi, w_ref[...])
    logits = two[:, :ROUTER_LANES] + two[:, ROUTER_LANES:] + _dot(h_lo, w_ref[:, :ROUTER_LANES]) + b_ref[...]

    lane = lax.broadcasted_iota(I32, logits.shape, 1).astype(F32)
    neg = -jnp.inf
    gl = jnp.where(lane < N_GROUPS, logits, neg)
    gmax = jnp.max(gl, axis=-1, keepdims=True)
    g_sel = jnp.min(jnp.where(gl == gmax, lane, ROUTER_LANES), axis=-1, keepdims=True)
    p_group = 1.0 / jnp.sum(jnp.exp(gl - gmax), axis=-1, keepdims=True)

    lo = N_GROUPS + g_sel * EXPERTS_PER_GROUP
    el = jnp.where(lane >= lo, jnp.where(lane < lo + EXPERTS_PER_GROUP, logits, neg), neg)
    v1 = jnp.max(el, axis=-1, keepdims=True)
    i1 = jnp.min(jnp.where(el == v1, lane, ROUTER_LANES), axis=-1, keepdims=True)
    el2 = jnp.where(lane == i1, neg, el)
    v2 = jnp.max(el2, axis=-1, keepdims=True)
    i2 = jnp.min(jnp.where(el2 == v2, lane, ROUTER_LANES), axis=-1, keepdims=True)
    e2 = jnp.exp(v2 - v1)
    w1 = p_group * (1.0 / (1.0 + e2))
    w2 = p_group * (e2 / (1.0 + e2))
    ids = jnp.where(lane == 0, i1 - N_GROUPS, jnp.where(lane == 1, i2 - N_GROUPS, 0.0))
    ids_ref[...] = ids.astype(I32)
    gates_ref[...] = jnp.where(lane == 0, w1, jnp.where(lane == 1, w2, 0.0))


def _router_call(x, g, mod, w_router2, b_router):
    tm = 256
    return pl.pallas_call(
        _router_kernel,
        grid=(TT // tm,),
        in_specs=[
            pl.BlockSpec((tm, D_MODEL), lambda i: (i, 0)),
            pl.BlockSpec((1, D_MODEL), lambda i: (0, 0)),
            _mod_spec(tm, 3),
            _mod_spec(tm, 4),
            pl.BlockSpec((D_MODEL, 2 * ROUTER_LANES), lambda i: (0, 0)),
            pl.BlockSpec((1, ROUTER_LANES), lambda i: (0, 0)),
        ],
        out_specs=[
            pl.BlockSpec((tm, HALF_D), lambda i: (i, 0)),
            pl.BlockSpec((tm, ROUTER_LANES), lambda i: (i, 0)),
            pl.BlockSpec((tm, ROUTER_LANES), lambda i: (i, 0)),
        ],
        out_shape=[
            jax.ShapeDtypeStruct((TT, HALF_D), U32),
            jax.ShapeDtypeStruct((TT, ROUTER_LANES), I32),
            jax.ShapeDtypeStruct((TT, ROUTER_LANES), F32),
        ],
        compiler_params=_cparams(("arbitrary",)),
        name="moe_router",
    )(x, g, mod, mod, w_router2, b_router)


def _moe_ffn_kernel(tok_ref, be_ref, nb_ref, h_hbm, wg_hbm, wu_hbm, wd_hbm, ys_hbm,
                    xrows, x_lo, x_hi, wg_buf, wu_buf, wd_buf, gate_acc, up_acc, hid, stage,
                    row_sem, gu_sem, dn_sem, out_sem, *, layer):
    b = pl.program_id(0)
    nb = nb_ref[0]

    def row_copy(blk, r):
        return pltpu.make_async_copy(h_hbm.at[pl.ds(tok_ref[blk * MOE_ROWS + r], 1), :],
                                     xrows.at[pl.ds(r, 1), :], row_sem)

    def gu_slot(blk, c):
        return (blk * GU_CHUNKS + c) % GU_SLOTS

    def gu_copies(blk, c):
        e, s, rows = be_ref[blk], gu_slot(blk, c), pl.ds(c * GU_ROWS, GU_ROWS)
        return (pltpu.make_async_copy(wg_hbm.at[layer, e, rows, :], wg_buf.at[s], gu_sem.at[s]),
                pltpu.make_async_copy(wu_hbm.at[layer, e, rows, :], wu_buf.at[s], gu_sem.at[s]))

    def dn_copy(blk, c):
        s = c % DN_SLOTS
        return pltpu.make_async_copy(wd_hbm.at[layer, be_ref[blk], :, pl.ds(c * DN_COLS, DN_COLS)],
                                     wd_buf.at[s], dn_sem.at[s])

    def out_copy(blk, c, src_slot):
        return pltpu.make_async_copy(
            stage.at[src_slot], ys_hbm.at[pl.ds(blk * MOE_ROWS, MOE_ROWS), pl.ds(c * DN_COLS, DN_COLS)],
            out_sem.at[src_slot])

    def request_next_rows(first, count):
        for r in range(count):
            row_copy(b + 1, first + r).start()

    @pl.when(b == 0)
    def _():
        for c in range(2):
            for cp in gu_copies(0, c):
                cp.start()

        def start(r, carry):
            row_copy(0, r).start()
            return carry
        lax.fori_loop(0, MOE_ROWS, start, 0)

    @pl.when(b <= nb)
    def _():
        def wait(g, carry):
            for r in range(ROW_WAIT_GROUP):
                row_copy(b, g * ROW_WAIT_GROUP + r).wait()
            return carry
        lax.fori_loop(0, MOE_ROWS // ROW_WAIT_GROUP, wait, 0)

    @pl.when(b < nb)
    def _():
        words = xrows[...]
        x_lo[...] = lax.bitcast_convert_type(words << 16, F32).astype(BF16)
        x_hi[...] = lax.bitcast_convert_type(words & jnp.uint32(0xFFFF0000), F32).astype(BF16)

        for c in range(GU_CHUNKS):
            if c + 2 < GU_CHUNKS:
                for cp in gu_copies(b, c + 2):
                    cp.start()
            else:
                first_dn = 2 * (c + 2 - GU_CHUNKS)
                dn_copy(b, first_dn).start()
                dn_copy(b, first_dn + 1).start()
            request_next_rows(c * GU_PHASE_ROWS, GU_PHASE_ROWS)
            for cp in gu_copies(b, c):
                cp.wait()
            s = gu_slot(b, c)
            x_half = x_lo if c * GU_ROWS < HALF_D else x_hi
            k0 = (c * GU_ROWS) % HALF_D
            xk = x_half[:, k0:k0 + GU_ROWS]
            gate = _dot(xk, wg_buf[s].astype(BF16))
            up = _dot(xk, wu_buf[s].astype(BF16))
            if c > 0:
                gate = gate_acc[...] + gate
                up = up_acc[...] + up
            if c + 1 < GU_CHUNKS:
                gate_acc[...] = gate
                up_acc[...] = up
            else:
                hid[...] = (gate * jax.nn.sigmoid(gate) * up).astype(BF16)

        for c in range(DN_CHUNKS):
            if 1 <= c and c + DN_SLOTS - 1 < DN_CHUNKS:
                dn_copy(b, c + DN_SLOTS - 1).start()
            if c in (DN_CHUNKS - 3, DN_CHUNKS - 2):
                nxt = c - (DN_CHUNKS - 3)

                @pl.when(b + 1 < nb)
                def _():
                    for cp in gu_copies(b + 1, nxt):
                        cp.start()
            request_next_rows(GU_CHUNKS * GU_PHASE_ROWS + c * DN_PHASE_ROWS, DN_PHASE_ROWS)
            dn_copy(b, c).wait()
            acc = _dot(hid[...], wd_buf[c % DN_SLOTS].astype(BF16))
            if c >= 2:
                out_copy(b, c - 2, c % 2).wait()
            else:
                pl.when(b > 0)(lambda: out_copy(b - 1, DN_CHUNKS - 2 + c, c % 2).wait())
            stage[c % 2] = acc
            out_copy(b, c, c % 2).start()

    @pl.when(b >= nb)
    def _():
        @pl.when(b == nb)
        def _():
            for c in (DN_CHUNKS - 2, DN_CHUNKS - 1):
                out_copy(b - 1, c, c % 2).wait()

        stage[0] = jnp.zeros(stage.shape[1:], F32)
        for c in range(DN_CHUNKS):
            out_copy(b, c, 0).start()
        for c in range(DN_CHUNKS):
            out_copy(b, c, 0).wait()


def _moe_ffn_call(layer, slot_tok, block_e, n_blocks, h_packed, w_gate, w_up, w_down):
    any_spec = pl.BlockSpec(memory_space=pl.ANY)
    return pl.pallas_call(
        functools.partial(_moe_ffn_kernel, layer=layer),
        grid_spec=pltpu.PrefetchScalarGridSpec(
            num_scalar_prefetch=3,
            grid=(N_BLOCKS,),
            in_specs=[any_spec] * 4,
            out_specs=any_spec,
            scratch_shapes=[
                pltpu.VMEM((MOE_ROWS, HALF_D), U32),
                pltpu.VMEM((MOE_ROWS, HALF_D), BF16),
                pltpu.VMEM((MOE_ROWS, HALF_D), BF16),
                pltpu.VMEM((GU_SLOTS, GU_ROWS, EXPERT_HIDDEN), F32),
                pltpu.VMEM((GU_SLOTS, GU_ROWS, EXPERT_HIDDEN), F32),
                pltpu.VMEM((DN_SLOTS, EXPERT_HIDDEN, DN_COLS), F32),
                pltpu.VMEM((MOE_ROWS, EXPERT_HIDDEN), F32),
                pltpu.VMEM((MOE_ROWS, EXPERT_HIDDEN), F32),
                pltpu.VMEM((MOE_ROWS, EXPERT_HIDDEN), BF16),
                pltpu.VMEM((2, MOE_ROWS, DN_COLS), F32),
                pltpu.SemaphoreType.DMA(()),
                pltpu.SemaphoreType.DMA((GU_SLOTS,)),
                pltpu.SemaphoreType.DMA((DN_SLOTS,)),
                pltpu.SemaphoreType.DMA((2,)),
            ],
        ),
        out_shape=jax.ShapeDtypeStruct((N_SLOTS, D_MODEL), F32),
        compiler_params=_cparams(("arbitrary",)),
        name="moe_experts",
    )(slot_tok, block_e, n_blocks, h_packed, w_gate, w_up, w_down)


def _combine_kernel(pos_ref, ys_hbm, x_ref, gates_ref, gate2_ref, *rest, tm, unroll, emit_next, split_out):
    if emit_next:
        ng_ref, nsh_ref, nsc_ref = rest[:3]
        rest = rest[3:]
    outs, (buf, sem) = rest[:-2], rest[-2:]
    i = pl.program_id(0)
    base = i * (tm * TOP_K)

    def copies(r):
        return [pltpu.make_async_copy(ys_hbm.at[pl.ds(pos_ref[base + TOP_K * r + k], 1), :],
                                      buf.at[k, pl.ds(r, 1), :], sem) for k in range(TOP_K)]

    def start(t, c):
        for u in range(unroll):
            for cp in copies(t * unroll + u):
                cp.start()
        return c

    def wait(t, c):
        for u in range(unroll):
            for cp in copies(t * unroll + u):
                cp.wait()
        return c

    lax.fori_loop(0, tm // unroll, start, 0)
    lax.fori_loop(0, tm // unroll, wait, 0)
    gates = gates_ref[...]
    y = gates[:, 0:1] * buf[0] + gates[:, 1:2] * buf[1]
    x = x_ref[...] + gate2_ref[...] * y
    if split_out:
        is_prompt = i < NP // tm

        @pl.when(is_prompt)
        def _():
            outs[0][...] = x

        @pl.when(jnp.logical_not(is_prompt))
        def _():
            outs[1][...] = x
    else:
        outs[0][...] = x
    if emit_next:
        outs[-1][...] = _modulate(x, ng_ref[...], nsh_ref[...], nsc_ref[...]).astype(BF16)


def _combine_call(pos, ys, x, gates, mod, next_norm=None, split_out=False):
    tm, unroll = 256, 8
    npt = NP // tm
    row = lambda i, pos: (i, 0)
    in_specs = [
        pl.BlockSpec(memory_space=pl.ANY),
        pl.BlockSpec((tm, D_MODEL), row),
        pl.BlockSpec((tm, ROUTER_LANES), row),
        pl.BlockSpec((None, 1, D_MODEL), lambda i, pos: (_cond_row(i * tm), 0, 5)),
    ]
    args = [ys, x, gates, mod]
    if next_norm is not None:
        g_next, mod_next = next_norm
        in_specs += [
            pl.BlockSpec((1, D_MODEL), lambda i, pos: (0, 0)),
            pl.BlockSpec((None, 1, D_MODEL), lambda i, pos: (_cond_row(i * tm), 0, 0)),
            pl.BlockSpec((None, 1, D_MODEL), lambda i, pos: (_cond_row(i * tm), 0, 1)),
        ]
        args += [g_next, mod_next, mod_next]
    if split_out:
        out_specs = [pl.BlockSpec((tm, D_MODEL), lambda i, pos: (jnp.minimum(i, npt - 1), 0)),
                     pl.BlockSpec((tm, D_MODEL), lambda i, pos: (jnp.maximum(i - npt, 0), 0))]
        out_shape = [jax.ShapeDtypeStruct((NP, D_MODEL), F32), jax.ShapeDtypeStruct((NS, D_MODEL), F32)]
    else:
        out_specs = [pl.BlockSpec((tm, D_MODEL), row)]
        out_shape = [jax.ShapeDtypeStruct((TT, D_MODEL), F32)]
    if next_norm is not None:
        out_specs.append(pl.BlockSpec((tm, D_MODEL), row))
        out_shape.append(jax.ShapeDtypeStruct((TT, D_MODEL), BF16))
    return pl.pallas_call(
        functools.partial(_combine_kernel, tm=tm, unroll=unroll, emit_next=next_norm is not None,
                          split_out=split_out),
        grid_spec=pltpu.PrefetchScalarGridSpec(
            num_scalar_prefetch=1,
            grid=(TT // tm,),
            in_specs=in_specs,
            out_specs=out_specs,
            scratch_shapes=[pltpu.VMEM((TOP_K, tm, D_MODEL), F32), pltpu.SemaphoreType.DMA(())],
        ),
        out_shape=out_shape,
        compiler_params=_cparams(("arbitrary",)),
        name="moe_combine",
    )(pos, *args)


def _dispatch_plan(ids):
    flat_e = ids[:, :TOP_K].reshape(N_ASSIGN)
    onehot = (flat_e[:, None] == jnp.arange(N_EXPERTS, dtype=I32)[None, :]).astype(I32)
    csum = jnp.cumsum(onehot, axis=0)
    rank = jnp.take_along_axis(csum, flat_e[:, None], axis=1)[:, 0] - 1
    counts = csum[-1]
    padded = (counts + MOE_ROWS - 1) // MOE_ROWS * MOE_ROWS
    pad_ends = jnp.cumsum(padded)
    pad_starts = pad_ends - padded
    dest = (pad_starts[flat_e] + rank).astype(I32)
    slot_tok = jnp.zeros((N_SLOTS,), I32).at[dest].set(jnp.arange(N_ASSIGN, dtype=I32) // TOP_K)
    n_blocks = (pad_ends[-1] // MOE_ROWS).astype(I32)
    blk = jnp.minimum(jnp.arange(N_BLOCKS, dtype=I32), n_blocks - 1)
    block_e = jnp.minimum(jnp.searchsorted(pad_ends, blk * MOE_ROWS, side="right"), N_EXPERTS - 1).astype(I32)
    return dest, slot_tok, block_e, n_blocks.reshape(1)


def _moe(layer, x, g, mod, w_router2, b_router, w_gate, w_up, w_down, **combine_kw):
    h_packed, ids, gates = _router_call(x, g, mod, w_router2, b_router)
    dest, slot_tok, block_e, n_blocks = _dispatch_plan(ids)
    ys = _moe_ffn_call(layer, slot_tok, block_e, n_blocks, h_packed, w_gate, w_up, w_down)
    return _combine_call(dest, ys, x, gates, mod, **combine_kw)


def _rope_tables(rot_dim):
    rows = DEC_SEQ // GRID_W
    nf = rot_dim // 4
    inv = jnp.exp(-math.log(ROPE_THETA) * jnp.arange(nf, dtype=F32) / nf)
    row_pos = jnp.repeat(jnp.arange(rows, dtype=F32), GRID_W)
    col_pos = jnp.tile(jnp.arange(GRID_W, dtype=F32), rows)
    ang = jnp.stack([row_pos[:, None] * inv, col_pos[:, None] * inv], axis=1)
    cos = jnp.broadcast_to(jnp.cos(ang)[:, :, None, :], (DEC_SEQ, 2, 2, nf)).reshape(DEC_SEQ, rot_dim)
    sgn = jnp.array([-1.0, 1.0], F32)[None, None, :, None]
    sin = (jnp.sin(ang)[:, :, None, :] * sgn).reshape(DEC_SEQ, rot_dim)
    pad = 128 - rot_dim
    if pad:
        cos = jnp.concatenate([cos, jnp.ones((DEC_SEQ, pad), F32)], axis=1)
        sin = jnp.concatenate([sin, jnp.zeros((DEC_SEQ, pad), F32)], axis=1)
    return cos, sin


def _pad_head_vec(v):
    return jnp.concatenate([v, jnp.zeros((MLA_HEAD_PAD - MLA_QK,), F32)]).reshape(1, MLA_HEAD_PAD)


def _router_weights(w_group, b_group, w_router, b_router):
    pad = ROUTER_LANES - N_GROUPS - N_EXPERTS
    w = jnp.concatenate([w_group, w_router, jnp.zeros((D_MODEL, pad), F32)], axis=1)
    w_hi = w.astype(BF16)
    w_lo = (w - w_hi.astype(F32)).astype(BF16)
    b = jnp.concatenate([b_group, b_router, jnp.zeros((pad,), F32)]).reshape(1, ROUTER_LANES)
    return jnp.concatenate([w_hi, w_lo], axis=1), b


def kernel(x_prompt, x_sample, cache_mla_ckv, cache_mla_krope, cache_diff_k, cache_diff_v, c, c_ctx, norm1_g, norm2_g, ada_w, ada_b, mla_w_in, mla_q_a_norm, mla_w_q_up, mla_kv_a_norm, mla_w_kv_up, mla_q_norm, mla_k_norm, mla_w_out, diff_w_qkv, diff_q_norm, diff_k_norm, diff_lambda_q1, diff_lambda_k1, diff_lambda_q2, diff_lambda_k2, diff_out_norm, diff_w_out, moe_w_group, moe_b_group, moe_w_router, moe_b_router, moe_w_gate, moe_w_up, moe_w_down):
    assert DEPTH == 2
    xp0 = x_prompt.reshape(NP, D_MODEL)
    xs0 = x_sample.reshape(NS, D_MODEL)
    cond = jnp.concatenate([c_ctx[None, :], c, jnp.zeros((N_COND - 1 - DEC_BATCH, D_MODEL), F32)], axis=0)
    mods = _ada_call(cond, ada_w, ada_b).reshape(DEPTH, N_COND, 1, 6 * D_MODEL)
    routers = [_router_weights(moe_w_group[i], moe_b_group[i], moe_w_router[i], moe_b_router[i])
               for i in range(DEPTH)]
    g2 = [norm2_g[i].reshape(1, D_MODEL) for i in range(DEPTH)]

    mod = mods[0]
    cos, sin = _rope_tables(MLA_ROPE)
    w_in = jnp.pad(mla_w_in[0], ((0, 0), (0, MLA_IN_PAD - MLA_IN))).astype(BF16)
    w_q = jnp.pad(mla_w_q_up[0].reshape(MLA_Q_LORA, MLA_HEADS, MLA_QK), ((0, 0), (0, 0), (0, MLA_HEAD_PAD - MLA_QK)))
    w_q = w_q.reshape(MLA_Q_LORA, MLA_HEADS * MLA_HEAD_PAD).astype(BF16)
    cq, ckv, kr = _mla_in_call(xp0, xs0, norm1_g[0].reshape(1, D_MODEL), mod, w_in,
                               mla_q_a_norm[0].reshape(1, -1), mla_kv_a_norm[0].reshape(1, -1))
    q = _q_up_call(cq, w_q, _pad_head_vec(mla_q_norm[0]), cos, sin)
    ckv_all = jnp.concatenate([ckv, cache_mla_ckv[:, 0].reshape(NCTX, MLA_KV_LORA)], axis=0)
    kr_ctx = jnp.pad(cache_mla_krope[:, 0].reshape(NCTX, MLA_ROPE), ((0, 0), (0, 128 - MLA_ROPE)))
    kr_all = jnp.concatenate([kr, kr_ctx], axis=0)
    k, v = _kv_up_call(ckv_all, kr_all, mla_w_kv_up[0].astype(BF16), _pad_head_vec(mla_k_norm[0]), cos, sin)
    o_p = _mla_attn_prompt_call(q, k, v)
    o_s = _mla_attn_latent_call(q, k, v)
    x = _out_proj_call(o_p, o_s, mla_w_out[0].astype(BF16), xp0, xs0, True, mod)
    new_ckv = ckv[:NP].reshape(BATCH, 1, SEQ, MLA_KV_LORA)
    new_krope = kr[:NP, :MLA_ROPE].reshape(BATCH, 1, SEQ, MLA_ROPE)
    x, h = _moe(0, x, g2[0], mod, *routers[0], moe_w_gate, moe_w_up, moe_w_down,
                next_norm=(norm1_g[1].reshape(1, D_MODEL), mods[1]))

    mod = mods[1]
    lam_init = 0.8 - 0.6 * math.exp(-0.3 * 1)
    cos, sin = _rope_tables(DIFF_HEAD_DIM)
    q, k_p, k_s, v_p, v_s = _diff_qkv_call(h, diff_w_qkv[0].astype(BF16), diff_q_norm[0].reshape(1, -1),
                                           diff_k_norm[0].reshape(1, -1), cos, sin)
    lams = [t[0].reshape(1, DIFF_HEAD_DIM) for t in
            (diff_lambda_q1, diff_lambda_k1, diff_lambda_q2, diff_lambda_k2)]
    out_norm = diff_out_norm[0].reshape(1, DIFF_V_DIM)
    o_p = _diff_attn_prompt_call(q, k_p, v_p, lams, out_norm, lam_init)
    o_s = _diff_attn_latent_call(q, k_s, v_s, cache_diff_k[:, 0].reshape(NCTX, D_MODEL),
                                 cache_diff_v[:, 0].reshape(NCTX, D_MODEL), lams, out_norm, lam_init)
    x = _out_proj_call(o_p, o_s, diff_w_out[0].astype(BF16), x, x, False, mod)
    new_k = k_p.reshape(BATCH, 1, SEQ, DIFF_HEADS, 2, DIFF_HEAD_DIM)
    new_v = v_p.reshape(BATCH, 1, SEQ, DIFF_HEADS, DIFF_V_DIM)
    y_p, y_s = _moe(1, x, g2[1], mod, *routers[1], moe_w_gate, moe_w_up, moe_w_down, split_out=True)

    return (y_p.reshape(BATCH, SEQ, D_MODEL), y_s.reshape(DEC_BATCH, DEC_SEQ, D_MODEL),
            new_ckv, new_krope, new_k, new_v)
```

```python
import functools
import math

import jax
import jax.numpy as jnp
from jax import lax
from jax.experimental import pallas as pl
from jax.experimental.pallas import tpu as pltpu

F32 = jnp.float32
BF16 = jnp.bfloat16
I32 = jnp.int32
U32 = jnp.uint32

D_MODEL = 4096
BATCH = 32
SEQ = 256
DEPTH = 2
DEC_BATCH = 4
DEC_SEQ = 1024
PAST_LEN = 256
GRID_W = 64
MLA_HEADS = 32
MLA_NOPE = 128
MLA_ROPE = 64
MLA_QK = MLA_NOPE + MLA_ROPE
MLA_V = 128
MLA_Q_LORA = 1024
MLA_KV_LORA = 512
MLA_IN = MLA_Q_LORA + MLA_KV_LORA + MLA_ROPE
MLA_IN_PAD = MLA_Q_LORA + MLA_KV_LORA + 128
MLA_HEAD_PAD = 256
DIFF_HEAD_DIM = 128
DIFF_HEADS = 16
DIFF_V_DIM = 256
N_GROUPS = 8
EXPERTS_PER_GROUP = 8
N_EXPERTS = 64
TOP_K = 2
EXPERT_HIDDEN = 1024
ROPE_THETA = 10000.0
NORM_EPS = 1e-6

NP = BATCH * SEQ
NS = DEC_BATCH * DEC_SEQ
TT = NP + NS
NCTX = DEC_BATCH * PAST_LEN
N_COND = 8
N_ASSIGN = TT * TOP_K
ROUTER_LANES = 128
HALF_D = D_MODEL // 2
ROW_TILES = HALF_D // 128

MOE_ROWS = 512
GU_ROWS = 512
GU_CHUNKS = D_MODEL // GU_ROWS
GU_SLOTS = 3
DN_COLS = 512
DN_CHUNKS = D_MODEL // DN_COLS
DN_SLOTS = 4
GU_PHASE_ROWS = 32
DN_PHASE_ROWS = (MOE_ROWS - GU_CHUNKS * GU_PHASE_ROWS) // DN_CHUNKS
ROW_WAIT_GROUP = 64
N_BLOCKS = N_ASSIGN // MOE_ROWS + N_EXPERTS + 1
N_SLOTS = N_BLOCKS * MOE_ROWS

VMEM_LIMIT = 56 * 1024 * 1024


def _cparams(sem):
    return pltpu.CompilerParams(dimension_semantics=sem, vmem_limit_bytes=VMEM_LIMIT)


def _cond_row(row0):
    return jnp.where(row0 < NP, 0, 1 + (row0 - NP) // DEC_SEQ)


def _rms(x, g):
    return x * lax.rsqrt(jnp.mean(x * x, axis=-1, keepdims=True) + NORM_EPS) * g


def _modulate(x, g, shift, scale):
    return _rms(x, g) * (1.0 + scale) + shift


def _dot(a, b):
    return jnp.dot(a, b, preferred_element_type=F32)


def _dot_nt(a, b):
    return lax.dot_general(a, b, (((1,), (1,)), ((), ())), preferred_element_type=F32)


def _rope128(x, cos, sin_signed, half):
    lane = lax.broadcasted_iota(I32, x.shape, 1)
    first_half = ((lane // half) % 2) == 0
    partner = jnp.where(first_half, pltpu.roll(x, 128 - half, 1), pltpu.roll(x, half, 1))
    return x * cos + partner * sin_signed


def _latent_tables(cos_ref, sin_ref, is_latent):
    m = is_latent.astype(F32)
    return 1.0 + m * (cos_ref[...] - 1.0), m * sin_ref[...]


def _row_pair_specs(tm, tn, split):
    npt = NP // tm
    off = npt if split else 0
    last = D_MODEL // tn - 1

    def prompt_map(i, *a):
        return (jnp.minimum(i, npt - 1), jnp.where(i < npt, a[0], last) if last else 0)

    def latent_map(i, *a):
        return (jnp.maximum(i, npt) - off, jnp.where(i >= npt, a[0], 0) if last else 0)

    return pl.BlockSpec((tm, tn), prompt_map), pl.BlockSpec((tm, tn), latent_map)


def _ada_kernel(c_ref, w_ref, b_ref, o_ref):
    c = c_ref[...]
    a = (c * jax.nn.sigmoid(c)).astype(BF16)
    o_ref[...] = _dot(a, w_ref[...].astype(BF16)) + b_ref[...]


def _ada_call(cond, ada_w, ada_b):
    tn = 512
    n = 6 * D_MODEL
    return pl.pallas_call(
        _ada_kernel,
        grid=(DEPTH, n // tn),
        in_specs=[
            pl.BlockSpec((N_COND, D_MODEL), lambda l, j: (0, 0)),
            pl.BlockSpec((None, D_MODEL, tn), lambda l, j: (l, 0, j)),
            pl.BlockSpec((None, 1, tn), lambda l, j: (l, 0, j)),
        ],
        out_specs=pl.BlockSpec((None, N_COND, tn), lambda l, j: (l, 0, j)),
        out_shape=jax.ShapeDtypeStruct((DEPTH, N_COND, n), F32),
        compiler_params=_cparams(("arbitrary", "arbitrary")),
        name="ada_table",
    )(cond, ada_w, ada_b.reshape(DEPTH, 1, n))


def _mod_spec(tm, chunk, width=D_MODEL):
    per = D_MODEL // width
    return pl.BlockSpec((None, 1, width),
                        lambda i, *a: (_cond_row(i * tm), 0, chunk * per + (a[0] if per > 1 else 0)))


def _mla_in_kernel(xp_ref, xs_ref, g_ref, sh_ref, sc_ref, w_ref, qg_ref, kvg_ref, cq_ref, ckv_ref, kr_ref, *, tm):
    def run(x_ref):
        h = _modulate(x_ref[...], g_ref[...], sh_ref[...], sc_ref[...]).astype(BF16)
        p = _dot(h, w_ref[...])
        cq_ref[...] = _rms(p[:, :MLA_Q_LORA], qg_ref[...]).astype(BF16)
        ckv_ref[...] = _rms(p[:, MLA_Q_LORA:MLA_Q_LORA + MLA_KV_LORA], kvg_ref[...])
        kr_ref[...] = p[:, MLA_Q_LORA + MLA_KV_LORA:]

    is_prompt = pl.program_id(0) < NP // tm
    pl.when(is_prompt)(lambda: run(xp_ref))
    pl.when(jnp.logical_not(is_prompt))(lambda: run(xs_ref))


def _mla_in_call(x_prompt, x_latent, g, mod, w_in, q_a_norm, kv_a_norm):
    tm = 256
    xp_spec, xs_spec = _row_pair_specs(tm, D_MODEL, split=True)
    return pl.pallas_call(
        functools.partial(_mla_in_kernel, tm=tm),
        grid=(TT // tm,),
        in_specs=[
            xp_spec,
            xs_spec,
            pl.BlockSpec((1, D_MODEL), lambda i: (0, 0)),
            _mod_spec(tm, 0),
            _mod_spec(tm, 1),
            pl.BlockSpec((D_MODEL, MLA_IN_PAD), lambda i: (0, 0)),
            pl.BlockSpec((1, MLA_Q_LORA), lambda i: (0, 0)),
            pl.BlockSpec((1, MLA_KV_LORA), lambda i: (0, 0)),
        ],
        out_specs=[
            pl.BlockSpec((tm, MLA_Q_LORA), lambda i: (i, 0)),
            pl.BlockSpec((tm, MLA_KV_LORA), lambda i: (i, 0)),
            pl.BlockSpec((tm, 128), lambda i: (i, 0)),
        ],
        out_shape=[
            jax.ShapeDtypeStruct((TT, MLA_Q_LORA), BF16),
            jax.ShapeDtypeStruct((TT, MLA_KV_LORA), F32),
            jax.ShapeDtypeStruct((TT, 128), F32),
        ],
        compiler_params=_cparams(("arbitrary",)),
        name="mla_in",
    )(x_prompt, x_latent, g, mod, mod, w_in, q_a_norm, kv_a_norm)


def _q_up_kernel(c_ref, w_ref, g_ref, cos_ref, sin_ref, q_ref, *, tm, heads):
    i = pl.program_id(0)
    cos, sin = _latent_tables(cos_ref, sin_ref, i >= NP // tm)
    o = _dot(c_ref[...], w_ref[...])
    g = g_ref[...] * (MLA_QK ** -0.5)
    for hh in range(heads):
        s = o[:, hh * 256:(hh + 1) * 256]
        r = lax.rsqrt(jnp.sum(s * s, axis=-1, keepdims=True) / MLA_QK + NORM_EPS)
        lo = s[:, :128] * r * g[:, :128]
        hi = _rope128(s[:, 128:] * r * g[:, 128:], cos, sin, MLA_ROPE // 4)
        q_ref[:, hh * 256:hh * 256 + 128] = lo.astype(BF16)
        q_ref[:, hh * 256 + 128:(hh + 1) * 256] = hi.astype(BF16)


def _pos_block(i, tm):
    return jnp.clip(i - NP // tm, 0, NS // tm - 1) % (DEC_SEQ // tm)


def _q_up_call(cq, w_q, q_norm_pad, cos, sin):
    tm, heads = 512, 8
    tn = heads * MLA_HEAD_PAD
    return pl.pallas_call(
        functools.partial(_q_up_kernel, tm=tm, heads=heads),
        grid=(TT // tm, MLA_HEADS // heads),
        in_specs=[
            pl.BlockSpec((tm, MLA_Q_LORA), lambda i, j: (i, 0)),
            pl.BlockSpec((MLA_Q_LORA, tn), lambda i, j: (0, j)),
            pl.BlockSpec((1, MLA_HEAD_PAD), lambda i, j: (0, 0)),
            pl.BlockSpec((tm, 128), lambda i, j: (_pos_block(i, tm), 0)),
            pl.BlockSpec((tm, 128), lambda i, j: (_pos_block(i, tm), 0)),
        ],
        out_specs=pl.BlockSpec((tm, tn), lambda i, j: (i, j)),
        out_shape=jax.ShapeDtypeStruct((TT, MLA_HEADS * MLA_HEAD_PAD), BF16),
        compiler_params=_cparams(("arbitrary", "arbitrary")),
        name="mla_q_up",
    )(cq, w_q, q_norm_pad, cos, sin)


def _kv_up_kernel(c_ref, kr_ref, w_ref, g_ref, cos_ref, sin_ref, k_ref, v_ref, *, tm, heads):
    i = pl.program_id(0)
    cos, sin = _latent_tables(cos_ref, sin_ref, (i >= NP // tm) & (i < TT // tm))
    kv = _dot(c_ref[...].astype(BF16), w_ref[...])
    g = g_ref[...]
    kr = kr_ref[...]
    ss_rope = jnp.sum(kr * kr, axis=-1, keepdims=True)
    base = _rope128(kr * g[:, 128:], cos, sin, MLA_ROPE // 4)
    for hh in range(heads):
        kn = kv[:, hh * 256:hh * 256 + 128]
        r = lax.rsqrt((jnp.sum(kn * kn, axis=-1, keepdims=True) + ss_rope) / MLA_QK + NORM_EPS)
        k_ref[:, hh * 256:hh * 256 + 128] = (kn * r * g[:, :128]).astype(BF16)
        k_ref[:, hh * 256 + 128:(hh + 1) * 256] = (base * r).astype(BF16)
        v_ref[:, hh * 128:(hh + 1) * 128] = kv[:, hh * 256 + 128:(hh + 1) * 256].astype(BF16)


def _kv_up_call(ckv_all, kr_all, w_kv, k_norm_pad, cos, sin):
    tm, heads = 512, 8
    rows = TT + NCTX
    return pl.pallas_call(
        functools.partial(_kv_up_kernel, tm=tm, heads=heads),
        grid=(rows // tm, MLA_HEADS // heads),
        in_specs=[
            pl.BlockSpec((tm, MLA_KV_LORA), lambda i, j: (i, 0)),
            pl.BlockSpec((tm, 128), lambda i, j: (i, 0)),
            pl.BlockSpec((MLA_KV_LORA, heads * 256), lambda i, j: (0, j)),
            pl.BlockSpec((1, MLA_HEAD_PAD), lambda i, j: (0, 0)),
            pl.BlockSpec((tm, 128), lambda i, j: (_pos_block(i, tm), 0)),
            pl.BlockSpec((tm, 128), lambda i, j: (_pos_block(i, tm), 0)),
        ],
        out_specs=[
            pl.BlockSpec((tm, heads * MLA_HEAD_PAD), lambda i, j: (i, j)),
            pl.BlockSpec((tm, heads * MLA_V), lambda i, j: (i, j)),
        ],
        out_shape=[
            jax.ShapeDtypeStruct((rows, MLA_HEADS * MLA_HEAD_PAD), BF16),
            jax.ShapeDtypeStruct((rows, MLA_HEADS * MLA_V), BF16),
        ],
        compiler_params=_cparams(("arbitrary", "arbitrary")),
        name="mla_kv_up",
    )(ckv_all, kr_all, w_kv, k_norm_pad, cos, sin)


def _mla_attn_prompt_kernel(q_ref, k_ref, v_ref, o_ref, *, heads):
    for hh in range(heads):
        q = q_ref[:, hh * 256:(hh + 1) * 256]
        k = k_ref[:, hh * 256:(hh + 1) * 256]
        s = _dot_nt(q, k)
        p = jnp.exp(s - jnp.max(s, axis=-1, keepdims=True))
        l = jnp.sum(p, axis=-1, keepdims=True)
        o = _dot(p.astype(BF16), v_ref[:, hh * 128:(hh + 1) * 128]) / l
        o_ref[:, hh * 128:(hh + 1) * 128] = o.astype(BF16)


def _mla_attn_prompt_call(q, k, v):
    heads = 8
    return pl.pallas_call(
        functools.partial(_mla_attn_prompt_kernel, heads=heads),
        grid=(BATCH, MLA_HEADS // heads),
        in_specs=[
            pl.BlockSpec((SEQ, heads * 256), lambda b, j: (b, j)),
            pl.BlockSpec((SEQ, heads * 256), lambda b, j: (b, j)),
            pl.BlockSpec((SEQ, heads * 128), lambda b, j: (b, j)),
        ],
        out_specs=pl.BlockSpec((SEQ, heads * 128), lambda b, j: (b, j)),
        out_shape=jax.ShapeDtypeStruct((NP, MLA_HEADS * MLA_V), BF16),
        compiler_params=_cparams(("arbitrary", "arbitrary")),
        name="mla_attn_prompt",
    )(q, k, v)


def _mla_attn_latent_kernel(q_ref, kl_ref, kc_ref, vl_ref, vc_ref, o_ref, *, heads):
    for hh in range(heads):
        q = q_ref[:, hh * 256:(hh + 1) * 256]
        sl = _dot_nt(q, kl_ref[:, hh * 256:(hh + 1) * 256])
        sc = _dot_nt(q, kc_ref[:, hh * 256:(hh + 1) * 256])
        m = jnp.maximum(jnp.max(sl, axis=-1, keepdims=True), jnp.max(sc, axis=-1, keepdims=True))
        pl_ = jnp.exp(sl - m)
        pc = jnp.exp(sc - m)
        l = jnp.sum(pl_, axis=-1, keepdims=True) + jnp.sum(pc, axis=-1, keepdims=True)
        o = _dot(pl_.astype(BF16), vl_ref[:, hh * 128:(hh + 1) * 128])
        o = o + _dot(pc.astype(BF16), vc_ref[:, hh * 128:(hh + 1) * 128])
        o_ref[:, hh * 128:(hh + 1) * 128] = (o / l).astype(BF16)


def _mla_attn_latent_call(q, k, v):
    heads, tq = 8, 512
    nq = DEC_SEQ // tq
    return pl.pallas_call(
        functools.partial(_mla_attn_latent_kernel, heads=heads),
        grid=(DEC_BATCH, MLA_HEADS // heads, nq),
        in_specs=[
            pl.BlockSpec((tq, heads * 256), lambda b, j, t: (NP // tq + b * nq + t, j)),
            pl.BlockSpec((DEC_SEQ, heads * 256), lambda b, j, t: (NP // DEC_SEQ + b, j)),
            pl.BlockSpec((PAST_LEN, heads * 256), lambda b, j, t: (TT // PAST_LEN + b, j)),
            pl.BlockSpec((DEC_SEQ, heads * 128), lambda b, j, t: (NP // DEC_SEQ + b, j)),
            pl.BlockSpec((PAST_LEN, heads * 128), lambda b, j, t: (TT // PAST_LEN + b, j)),
        ],
        out_specs=pl.BlockSpec((tq, heads * 128), lambda b, j, t: (b * nq + t, j)),
        out_shape=jax.ShapeDtypeStruct((NS, MLA_HEADS * MLA_V), BF16),
        compiler_params=_cparams(("arbitrary", "arbitrary", "arbitrary")),
        name="mla_attn_latent",
    )(q, k, k, v, v)


def _out_proj_kernel(ap_ref, as_ref, w_ref, xp_ref, xs_ref, gate_ref, o_ref, *, tm):
    def finish(a_ref, x_ref):
        o_ref[...] = x_ref[...] + gate_ref[...] * _dot(a_ref[...], w_ref[...])

    is_prompt = pl.program_id(0) < NP // tm
    pl.when(is_prompt)(lambda: finish(ap_ref, xp_ref))
    pl.when(jnp.logical_not(is_prompt))(lambda: finish(as_ref, xs_ref))


def _out_proj_call(a_prompt, a_latent, w, x_prompt, x_latent, x_split, mod):
    tm, tn = 512, 1024
    ap_spec, as_spec = _row_pair_specs(tm, D_MODEL, split=True)
    xp_spec, xs_spec = _row_pair_specs(tm, tn, split=x_split)
    return pl.pallas_call(
        functools.partial(_out_proj_kernel, tm=tm),
        grid=(TT // tm, D_MODEL // tn),
        in_specs=[
            ap_spec,
            as_spec,
            pl.BlockSpec((D_MODEL, tn), lambda i, j: (0, j)),
            xp_spec,
            xs_spec,
            _mod_spec(tm, 2, width=tn),
        ],
        out_specs=pl.BlockSpec((tm, tn), lambda i, j: (i, j)),
        out_shape=jax.ShapeDtypeStruct((TT, D_MODEL), F32),
        compiler_params=_cparams(("arbitrary", "arbitrary")),
        name="attn_out_proj",
    )(a_prompt, a_latent, w, x_prompt, x_latent, mod)


def _diff_qkv_kernel(h_ref, w_ref, qn_ref, kn_ref, cos_ref, sin_ref, q_ref, kp_ref, kl_ref, vp_ref, vl_ref,
                     *, tm, nq, sub):
    i = pl.program_id(0)
    j = pl.program_id(1)
    is_prompt = i < NP // tm
    cos, sin = _latent_tables(cos_ref, sin_ref, jnp.logical_not(is_prompt))
    n_sub = w_ref.shape[1] // sub

    def store_rows(prompt_ref, latent_ref, pieces):
        def fill(ref):
            for c0, y in pieces:
                ref[:, c0:c0 + y.shape[1]] = y
        pl.when(is_prompt)(lambda: fill(prompt_ref))
        pl.when(jnp.logical_not(is_prompt))(lambda: fill(latent_ref))

    def norm_rope(o, g, scale):
        return [_rope128(_rms(o[:, s * 128:(s + 1) * 128], g) * scale, cos, sin, DIFF_HEAD_DIM // 4)
                for s in range(sub // 128)]

    @pl.when(j < nq)
    def _():
        for t in range(n_sub):
            o = _dot(h_ref[...], w_ref[:, t * sub:(t + 1) * sub])
            for s, y in enumerate(norm_rope(o, qn_ref[...], DIFF_HEAD_DIM ** -0.5)):
                q_ref[:, t * sub + s * 128:t * sub + (s + 1) * 128] = y.astype(BF16)

    @pl.when((j >= nq) & (j < 2 * nq))
    def _():
        pieces = []
        for t in range(n_sub):
            o = _dot(h_ref[...], w_ref[:, t * sub:(t + 1) * sub])
            pieces += [(t * sub + s * 128, y) for s, y in enumerate(norm_rope(o, kn_ref[...], 1.0))]
        store_rows(kp_ref, kl_ref, pieces)

    @pl.when(j >= 2 * nq)
    def _():
        store_rows(vp_ref, vl_ref, [(0, _dot(h_ref[...], w_ref[...]))])


def _diff_qkv_call(h, w_qkv, q_norm, k_norm, cos, sin):
    tm, tn, sub = 512, 1024, 256
    nq = D_MODEL // tn
    npt = NP // tm

    def kv_specs(first):
        col = lambda j: jnp.clip(j - first, 0, nq - 1)
        return [pl.BlockSpec((tm, tn), lambda i, j: (jnp.minimum(i, npt - 1), jnp.where(i < npt, col(j), nq - 1))),
                pl.BlockSpec((tm, tn), lambda i, j: (jnp.maximum(i - npt, 0), jnp.where(i >= npt, col(j), 0)))]

    kv_shapes = [jax.ShapeDtypeStruct((NP, D_MODEL), F32), jax.ShapeDtypeStruct((NS, D_MODEL), F32)]
    return pl.pallas_call(
        functools.partial(_diff_qkv_kernel, tm=tm, nq=nq, sub=sub),
        grid=(TT // tm, 3 * nq),
        in_specs=[
            pl.BlockSpec((tm, D_MODEL), lambda i, j: (i, 0)),
            pl.BlockSpec((D_MODEL, tn), lambda i, j: (0, j)),
            pl.BlockSpec((1, DIFF_HEAD_DIM), lambda i, j: (0, 0)),
            pl.BlockSpec((1, DIFF_HEAD_DIM), lambda i, j: (0, 0)),
            pl.BlockSpec((tm, 128), lambda i, j: (_pos_block(i, tm), 0)),
            pl.BlockSpec((tm, 128), lambda i, j: (_pos_block(i, tm), 0)),
        ],
        out_specs=[pl.BlockSpec((tm, tn), lambda i, j: (i, jnp.minimum(j, nq - 1)))]
        + kv_specs(nq) + kv_specs(2 * nq),
        out_shape=[jax.ShapeDtypeStruct((TT, D_MODEL), BF16)] + kv_shapes + kv_shapes,
        compiler_params=_cparams(("arbitrary", "arbitrary")),
        name="diff_qkv",
    )(h, w_qkv, q_norm, k_norm, cos, sin)


def _diff_lambda(lq1_ref, lk1_ref, lq2_ref, lk2_ref, lam_init):
    e1 = jnp.exp(jnp.sum(lq1_ref[...] * lk1_ref[...], axis=-1, keepdims=True))
    e2 = jnp.exp(jnp.sum(lq2_ref[...] * lk2_ref[...], axis=-1, keepdims=True))
    return e1 - e2 + lam_init


def _softmax_rows(s):
    p = jnp.exp(s - jnp.max(s, axis=-1, keepdims=True))
    return p / jnp.sum(p, axis=-1, keepdims=True)


def _diff_attn_prompt_kernel(q_ref, k_ref, v_ref, lq1, lk1, lq2, lk2, on_ref, o_ref, *, heads, lam_init):
    lam = _diff_lambda(lq1, lk1, lq2, lk2, lam_init)
    for hh in range(heads):
        c0 = hh * 256
        p1 = _softmax_rows(_dot_nt(q_ref[:, c0:c0 + 128], k_ref[:, c0:c0 + 128].astype(BF16)))
        p2 = _softmax_rows(_dot_nt(q_ref[:, c0 + 128:c0 + 256], k_ref[:, c0 + 128:c0 + 256].astype(BF16)))
        a = (p1 - lam * p2).astype(BF16)
        o = _dot(a, v_ref[:, c0:c0 + 256].astype(BF16))
        o_ref[:, c0:c0 + 256] = (_rms(o, on_ref[...]) * (1.0 - lam_init)).astype(BF16)


def _vec_spec(n):
    return pl.BlockSpec((1, n), lambda *a: (0, 0))


def _diff_attn_prompt_call(q, k, v, lams, out_norm, lam_init):
    heads = 4
    w = heads * 256
    blk = pl.BlockSpec((SEQ, w), lambda b, j: (b, j))
    return pl.pallas_call(
        functools.partial(_diff_attn_prompt_kernel, heads=heads, lam_init=lam_init),
        grid=(BATCH, DIFF_HEADS // heads),
        in_specs=[blk, blk, blk] + [_vec_spec(DIFF_HEAD_DIM)] * 4 + [_vec_spec(DIFF_V_DIM)],
        out_specs=blk,
        out_shape=jax.ShapeDtypeStruct((NP, D_MODEL), BF16),
        compiler_params=_cparams(("arbitrary", "arbitrary")),
        name="diff_attn_prompt",
    )(q, k, v, *lams, out_norm)


def _softmax2(sa, sb):
    m = jnp.maximum(jnp.max(sa, axis=-1, keepdims=True), jnp.max(sb, axis=-1, keepdims=True))
    pa = jnp.exp(sa - m)
    pb = jnp.exp(sb - m)
    l = jnp.sum(pa, axis=-1, keepdims=True) + jnp.sum(pb, axis=-1, keepdims=True)
    return pa / l, pb / l


def _diff_attn_latent_kernel(q_ref, kl_ref, kc_ref, vl_ref, vc_ref, lq1, lk1, lq2, lk2, on_ref,
                             o_ref, *, heads, lam_init):
    lam = _diff_lambda(lq1, lk1, lq2, lk2, lam_init)
    for hh in range(heads):
        c0 = hh * 256
        q1 = q_ref[:, c0:c0 + 128]
        q2 = q_ref[:, c0 + 128:c0 + 256]
        p1l, p1c = _softmax2(_dot_nt(q1, kl_ref[:, c0:c0 + 128].astype(BF16)),
                             _dot_nt(q1, kc_ref[:, c0:c0 + 128].astype(BF16)))
        p2l, p2c = _softmax2(_dot_nt(q2, kl_ref[:, c0 + 128:c0 + 256].astype(BF16)),
                             _dot_nt(q2, kc_ref[:, c0 + 128:c0 + 256].astype(BF16)))
        o = _dot((p1l - lam * p2l).astype(BF16), vl_ref[:, c0:c0 + 256].astype(BF16))
        o = o + _dot((p1c - lam * p2c).astype(BF16), vc_ref[:, c0:c0 + 256].astype(BF16))
        o_ref[:, c0:c0 + 256] = (_rms(o, on_ref[...]) * (1.0 - lam_init)).astype(BF16)


def _diff_attn_latent_call(q, k, v, k_ctx, v_ctx, lams, out_norm, lam_init):
    heads, tq = 4, 256
    w = heads * 256
    nq = DEC_SEQ // tq
    lat = pl.BlockSpec((DEC_SEQ, w), lambda b, j, t: (b, j))
    ctx = pl.BlockSpec((PAST_LEN, w), lambda b, j, t: (b, j))
    return pl.pallas_call(
        functools.partial(_diff_attn_latent_kernel, heads=heads, lam_init=lam_init),
        grid=(DEC_BATCH, DIFF_HEADS // heads, nq),
        in_specs=[pl.BlockSpec((tq, w), lambda b, j, t: (NP // tq + b * nq + t, j)), lat, ctx, lat, ctx]
        + [_vec_spec(DIFF_HEAD_DIM)] * 4 + [_vec_spec(DIFF_V_DIM)],
        out_specs=pl.BlockSpec((tq, w), lambda b, j, t: (b * nq + t, j)),
        out_shape=jax.ShapeDtypeStruct((NS, D_MODEL), BF16),
        compiler_params=_cparams(("arbitrary", "arbitrary", "arbitrary")),
        name="diff_attn_latent",
    )(q, k, k_ctx, v, v_ctx, *lams, out_norm)


def _router_kernel(x_ref, g_ref, sh_ref, sc_ref, w_ref, b_ref, h_ref, ids_ref, gates_ref):
    h = _modulate(x_ref[...], g_ref[...], sh_ref[...], sc_ref[...])
    h_hi = h.astype(BF16)
    h_hi32 = h_hi.astype(F32)
    bits = lax.bitcast_convert_type(h_hi32, U32)
    h_ref[...] = (bits[:, :HALF_D] >> 16) | (bits[:, HALF_D:] & jnp.uint32(0xFFFF0000))

    h_lo = (h - h_hi32).astype(BF16)
    two = _dot(h_hi, w_ref[...])
    logits = two[:, :ROUTER_LANES] + two[:, ROUTER_LANES:] + _dot(h_lo, w_ref[:, :ROUTER_LANES]) + b_ref[...]

    lane = lax.broadcasted_iota(I32, logits.shape, 1).astype(F32)
    neg = -jnp.inf
    gl = jnp.where(lane < N_GROUPS, logits, neg)
    gmax = jnp.max(gl, axis=-1, keepdims=True)
    g_sel = jnp.min(jnp.where(gl == gmax, lane, ROUTER_LANES), axis=-1, keepdims=True)
    p_group = 1.0 / jnp.sum(jnp.exp(gl - gmax), axis=-1, keepdims=True)

    lo = N_GROUPS + g_sel * EXPERTS_PER_GROUP
    el = jnp.where(lane >= lo, jnp.where(lane < lo + EXPERTS_PER_GROUP, logits, neg), neg)
    v1 = jnp.max(el, axis=-1, keepdims=True)
    i1 = jnp.min(jnp.where(el == v1, lane, ROUTER_LANES), axis=-1, keepdims=True)
    el2 = jnp.where(lane == i1, neg, el)
    v2 = jnp.max(el2, axis=-1, keepdims=True)
    i2 = jnp.min(jnp.where(el2 == v2, lane, ROUTER_LANES), axis=-1, keepdims=True)
    e2 = jnp.exp(v2 - v1)
    w1 = p_group * (1.0 / (1.0 + e2))
    w2 = p_group * (e2 / (1.0 + e2))
    ids = jnp.where(lane == 0, i1 - N_GROUPS, jnp.where(lane == 1, i2 - N_GROUPS, 0.0))
    ids_ref[...] = ids.astype(I32)
    gates_ref[...] = jnp.where(lane == 0, w1, jnp.where(lane == 1, w2, 0.0))


def _router_call(x, g, mod, w_router2, b_router):
    tm = 256
    return pl.pallas_call(
        _router_kernel,
        grid=(TT // tm,),
        in_specs=[
            pl.BlockSpec((tm, D_MODEL), lambda i: (i, 0)),
            pl.BlockSpec((1, D_MODEL), lambda i: (0, 0)),
            _mod_spec(tm, 3),
            _mod_spec(tm, 4),
            pl.BlockSpec((D_MODEL, 2 * ROUTER_LANES), lambda i: (0, 0)),
            pl.BlockSpec((1, ROUTER_LANES), lambda i: (0, 0)),
        ],
        out_specs=[
            pl.BlockSpec((tm, HALF_D), lambda i: (i, 0)),
            pl.BlockSpec((tm, ROUTER_LANES), lambda i: (i, 0)),
            pl.BlockSpec((tm, ROUTER_LANES), lambda i: (i, 0)),
        ],
        out_shape=[
            jax.ShapeDtypeStruct((TT, HALF_D), U32),
            jax.ShapeDtypeStruct((TT, ROUTER_LANES), I32),
            jax.ShapeDtypeStruct((TT, ROUTER_LANES), F32),
        ],
        compiler_params=_cparams(("arbitrary",)),
        name="moe_router",
    )(x, g, mod, mod, w_router2, b_router)


def _moe_ffn_kernel(tok_ref, be_ref, nb_ref, h_hbm, wg_hbm, wu_hbm, wd_hbm, ys_hbm,
                    xrows, x_lo, x_hi, wg_buf, wu_buf, wd_buf, gate_acc, up_acc, hid, stage,
                    row_sem, gu_sem, dn_sem, out_sem, *, layer):
    b = pl.program_id(0)
    nb = nb_ref[0]

    def row_copy(blk, r):
        return pltpu.make_async_copy(h_hbm.at[tok_ref[blk * MOE_ROWS + r]],
                                     xrows.at[pl.ds(r * ROW_TILES, ROW_TILES), :], row_sem)

    def gu_slot(blk, c):
        return (blk * GU_CHUNKS + c) % GU_SLOTS

    def gu_copies(blk, c):
        e, s, rows = be_ref[blk], gu_slot(blk, c), pl.ds(c * GU_ROWS, GU_ROWS)
        return (pltpu.make_async_copy(wg_hbm.at[layer, e, rows, :], wg_buf.at[s], gu_sem.at[s]),
                pltpu.make_async_copy(wu_hbm.at[layer, e, rows, :], wu_buf.at[s], gu_sem.at[s]))

    def dn_copy(blk, c):
        s = c % DN_SLOTS
        return pltpu.make_async_copy(wd_hbm.at[layer, be_ref[blk], :, pl.ds(c * DN_COLS, DN_COLS)],
                                     wd_buf.at[s], dn_sem.at[s])

    def out_copy(blk, c, src_slot):
        return pltpu.make_async_copy(
            stage.at[src_slot], ys_hbm.at[pl.ds(blk * MOE_ROWS, MOE_ROWS), pl.ds(c * DN_COLS, DN_COLS)],
            out_sem.at[src_slot])

    def request_next_rows(first, count):
        for r in range(count):
            row_copy(b + 1, first + r).start()

    @pl.when(b == 0)
    def _():
        for c in range(2):
            for cp in gu_copies(0, c):
                cp.start()

        def start(r, carry):
            row_copy(0, r).start()
            return carry
        lax.fori_loop(0, MOE_ROWS, start, 0)

    @pl.when(b <= nb)
    def _():
        def wait(g, carry):
            for r in range(ROW_WAIT_GROUP):
                row_copy(b, g * ROW_WAIT_GROUP + r).wait()
            return carry
        lax.fori_loop(0, MOE_ROWS // ROW_WAIT_GROUP, wait, 0)

    @pl.when(b < nb)
    def _():
        for q in range(ROW_TILES):
            words = xrows[pl.ds(q, MOE_ROWS, stride=ROW_TILES), :]
            x_lo[:, q * 128:(q + 1) * 128] = lax.bitcast_convert_type(words << 16, F32).astype(BF16)
            x_hi[:, q * 128:(q + 1) * 128] = lax.bitcast_convert_type(
                words & jnp.uint32(0xFFFF0000), F32).astype(BF16)

        for c in range(GU_CHUNKS):
            if c + 2 < GU_CHUNKS:
                for cp in gu_copies(b, c + 2):
                    cp.start()
            else:
                first_dn = 2 * (c + 2 - GU_CHUNKS)
                dn_copy(b, first_dn).start()
                dn_copy(b, first_dn + 1).start()
            request_next_rows(c * GU_PHASE_ROWS, GU_PHASE_ROWS)
            for cp in gu_copies(b, c):
                cp.wait()
            s = gu_slot(b, c)
            x_half = x_lo if c * GU_ROWS < HALF_D else x_hi
            k0 = (c * GU_ROWS) % HALF_D
            xk = x_half[:, k0:k0 + GU_ROWS]
            gate = _dot(xk, wg_buf[s].astype(BF16))
            up = _dot(xk, wu_buf[s].astype(BF16))
            if c > 0:
                gate = gate_acc[...] + gate
                up = up_acc[...] + up
            if c + 1 < GU_CHUNKS:
                gate_acc[...] = gate
                up_acc[...] = up
            else:
                hid[...] = (gate * jax.nn.sigmoid(gate) * up).astype(BF16)

        for c in range(DN_CHUNKS):
            if 1 <= c and c + DN_SLOTS - 1 < DN_CHUNKS:
                dn_copy(b, c + DN_SLOTS - 1).start()
            if c in (DN_CHUNKS - 3, DN_CHUNKS - 2):
                nxt = c - (DN_CHUNKS - 3)

                @pl.when(b + 1 < nb)
                def _():
                    for cp in gu_copies(b + 1, nxt):
                        cp.start()
            request_next_rows(GU_CHUNKS * GU_PHASE_ROWS + c * DN_PHASE_ROWS, DN_PHASE_ROWS)
            dn_copy(b, c).wait()
            acc = _dot(hid[...], wd_buf[c % DN_SLOTS].astype(BF16))
            if c >= 2:
                out_copy(b, c - 2, c % 2).wait()
            else:
                pl.when(b > 0)(lambda: out_copy(b - 1, DN_CHUNKS - 2 + c, c % 2).wait())
            stage[c % 2] = acc
            out_copy(b, c, c % 2).start()

    @pl.when(b >= nb)
    def _():
        @pl.when(b == nb)
        def _():
            for c in (DN_CHUNKS - 2, DN_CHUNKS - 1):
                out_copy(b - 1, c, c % 2).wait()

        stage[0] = jnp.zeros(stage.shape[1:], F32)
        for c in range(DN_CHUNKS):
            out_copy(b, c, 0).start()
        for c in range(DN_CHUNKS):
            out_copy(b, c, 0).wait()


def _moe_ffn_call(layer, slot_tok, block_e, n_blocks, h_packed, w_gate, w_up, w_down):
    any_spec = pl.BlockSpec(memory_space=pl.ANY)
    return pl.pallas_call(
        functools.partial(_moe_ffn_kernel, layer=layer),
        grid_spec=pltpu.PrefetchScalarGridSpec(
            num_scalar_prefetch=3,
            grid=(N_BLOCKS,),
            in_specs=[any_spec] * 4,
            out_specs=any_spec,
            scratch_shapes=[
                pltpu.VMEM((MOE_ROWS * ROW_TILES, 128), U32),
                pltpu.VMEM((MOE_ROWS, HALF_D), BF16),
                pltpu.VMEM((MOE_ROWS, HALF_D), BF16),
                pltpu.VMEM((GU_SLOTS, GU_ROWS, EXPERT_HIDDEN), F32),
                pltpu.VMEM((GU_SLOTS, GU_ROWS, EXPERT_HIDDEN), F32),
                pltpu.VMEM((DN_SLOTS, EXPERT_HIDDEN, DN_COLS), F32),
                pltpu.VMEM((MOE_ROWS, EXPERT_HIDDEN), F32),
                pltpu.VMEM((MOE_ROWS, EXPERT_HIDDEN), F32),
                pltpu.VMEM((MOE_ROWS, EXPERT_HIDDEN), BF16),
                pltpu.VMEM((2, MOE_ROWS, DN_COLS), F32),
                pltpu.SemaphoreType.DMA(()),
                pltpu.SemaphoreType.DMA((GU_SLOTS,)),
                pltpu.SemaphoreType.DMA((DN_SLOTS,)),
                pltpu.SemaphoreType.DMA((2,)),
            ],
        ),
        out_shape=jax.ShapeDtypeStruct((N_SLOTS, D_MODEL), F32),
        compiler_params=_cparams(("arbitrary",)),
        name="moe_experts",
    )(slot_tok, block_e, n_blocks, h_packed.reshape(TT, ROW_TILES, 128), w_gate, w_up, w_down)


def _combine_kernel(pos_ref, ys_hbm, x_ref, gates_ref, gate2_ref, *rest, tm, unroll, emit_next, split_out):
    if emit_next:
        ng_ref, nsh_ref, nsc_ref = rest[:3]
        rest = rest[3:]
    outs, (buf, sem) = rest[:-2], rest[-2:]
    i = pl.program_id(0)
    base = i * (tm * TOP_K)

    def copies(r):
        return [pltpu.make_async_copy(ys_hbm.at[pl.ds(pos_ref[base + TOP_K * r + k], 1), :],
                                      buf.at[k, pl.ds(r, 1), :], sem) for k in range(TOP_K)]

    def start(t, c):
        for u in range(unroll):
            for cp in copies(t * unroll + u):
                cp.start()
        return c

    def wait(t, c):
        for u in range(unroll):
            for cp in copies(t * unroll + u):
                cp.wait()
        return c

    lax.fori_loop(0, tm // unroll, start, 0)
    lax.fori_loop(0, tm // unroll, wait, 0)
    gates = gates_ref[...]
    y = gates[:, 0:1] * buf[0] + gates[:, 1:2] * buf[1]
    x = x_ref[...] + gate2_ref[...] * y
    if split_out:
        is_prompt = i < NP // tm

        @pl.when(is_prompt)
        def _():
            outs[0][...] = x

        @pl.when(jnp.logical_not(is_prompt))
        def _():
            outs[1][...] = x
    else:
        outs[0][...] = x
    if emit_next:
        outs[-1][...] = _modulate(x, ng_ref[...], nsh_ref[...], nsc_ref[...]).astype(BF16)


def _combine_call(pos, ys, x, gates, mod, next_norm=None, split_out=False):
    tm, unroll = 256, 8
    npt = NP // tm
    row = lambda i, pos: (i, 0)
    in_specs = [
        pl.BlockSpec(memory_space=pl.ANY),
        pl.BlockSpec((tm, D_MODEL), row),
        pl.BlockSpec((tm, ROUTER_LANES), row),
        pl.BlockSpec((None, 1, D_MODEL), lambda i, pos: (_cond_row(i * tm), 0, 5)),
    ]
    args = [ys, x, gates, mod]
    if next_norm is not None:
        g_next, mod_next = next_norm
        in_specs += [
            pl.BlockSpec((1, D_MODEL), lambda i, pos: (0, 0)),
            pl.BlockSpec((None, 1, D_MODEL), lambda i, pos: (_cond_row(i * tm), 0, 0)),
            pl.BlockSpec((None, 1, D_MODEL), lambda i, pos: (_cond_row(i * tm), 0, 1)),
        ]
        args += [g_next, mod_next, mod_next]
    if split_out:
        out_specs = [pl.BlockSpec((tm, D_MODEL), lambda i, pos: (jnp.minimum(i, npt - 1), 0)),
                     pl.BlockSpec((tm, D_MODEL), lambda i, pos: (jnp.maximum(i - npt, 0), 0))]
        out_shape = [jax.ShapeDtypeStruct((NP, D_MODEL), F32), jax.ShapeDtypeStruct((NS, D_MODEL), F32)]
    else:
        out_specs = [pl.BlockSpec((tm, D_MODEL), row)]
        out_shape = [jax.ShapeDtypeStruct((TT, D_MODEL), F32)]
    if next_norm is not None:
        out_specs.append(pl.BlockSpec((tm, D_MODEL), row))
        out_shape.append(jax.ShapeDtypeStruct((TT, D_MODEL), BF16))
    return pl.pallas_call(
        functools.partial(_combine_kernel, tm=tm, unroll=unroll, emit_next=next_norm is not None,
                          split_out=split_out),
        grid_spec=pltpu.PrefetchScalarGridSpec(
            num_scalar_prefetch=1,
            grid=(TT // tm,),
            in_specs=in_specs,
            out_specs=out_specs,
            scratch_shapes=[pltpu.VMEM((TOP_K, tm, D_MODEL), F32), pltpu.SemaphoreType.DMA(())],
        ),
        out_shape=out_shape,
        compiler_params=_cparams(("arbitrary",)),
        name="moe_combine",
    )(pos, *args)


def _dispatch_plan(ids):
    flat_e = ids[:, :TOP_K].reshape(N_ASSIGN)
    onehot = (flat_e[:, None] == jnp.arange(N_EXPERTS, dtype=I32)[None, :]).astype(I32)
    csum = jnp.cumsum(onehot, axis=0)
    rank = jnp.take_along_axis(csum, flat_e[:, None], axis=1)[:, 0] - 1
    counts = csum[-1]
    padded = (counts + MOE_ROWS - 1) // MOE_ROWS * MOE_ROWS
    pad_ends = jnp.cumsum(padded)
    pad_starts = pad_ends - padded
    dest = (pad_starts[flat_e] + rank).astype(I32)
    slot_tok = jnp.zeros((N_SLOTS,), I32).at[dest].set(jnp.arange(N_ASSIGN, dtype=I32) // TOP_K)
    n_blocks = (pad_ends[-1] // MOE_ROWS).astype(I32)
    blk = jnp.minimum(jnp.arange(N_BLOCKS, dtype=I32), n_blocks - 1)
    block_e = jnp.minimum(jnp.searchsorted(pad_ends, blk * MOE_ROWS, side="right"), N_EXPERTS - 1).astype(I32)
    return dest, slot_tok, block_e, n_blocks.reshape(1)


def _moe(layer, x, g, mod, w_router2, b_router, w_gate, w_up, w_down, **combine_kw):
    h_packed, ids, gates = _router_call(x, g, mod, w_router2, b_router)
    dest, slot_tok, block_e, n_blocks = _dispatch_plan(ids)
    ys = _moe_ffn_call(layer, slot_tok, block_e, n_blocks, h_packed, w_gate, w_up, w_down)
    return _combine_call(dest, ys, x, gates, mod, **combine_kw)


def _rope_tables(rot_dim):
    rows = DEC_SEQ // GRID_W
    nf = rot_dim // 4
    inv = jnp.exp(-math.log(ROPE_THETA) * jnp.arange(nf, dtype=F32) / nf)
    row_pos = jnp.repeat(jnp.arange(rows, dtype=F32), GRID_W)
    col_pos = jnp.tile(jnp.arange(GRID_W, dtype=F32), rows)
    ang = jnp.stack([row_pos[:, None] * inv, col_pos[:, None] * inv], axis=1)
    cos = jnp.broadcast_to(jnp.cos(ang)[:, :, None, :], (DEC_SEQ, 2, 2, nf)).reshape(DEC_SEQ, rot_dim)
    sgn = jnp.array([-1.0, 1.0], F32)[None, None, :, None]
    sin = (jnp.sin(ang)[:, :, None, :] * sgn).reshape(DEC_SEQ, rot_dim)
    pad = 128 - rot_dim
    if pad:
        cos = jnp.concatenate([cos, jnp.ones((DEC_SEQ, pad), F32)], axis=1)
        sin = jnp.concatenate([sin, jnp.zeros((DEC_SEQ, pad), F32)], axis=1)
    return cos, sin


def _pad_head_vec(v):
    return jnp.concatenate([v, jnp.zeros((MLA_HEAD_PAD - MLA_QK,), F32)]).reshape(1, MLA_HEAD_PAD)


def _router_weights(w_group, b_group, w_router, b_router):
    pad = ROUTER_LANES - N_GROUPS - N_EXPERTS
    w = jnp.concatenate([w_group, w_router, jnp.zeros((D_MODEL, pad), F32)], axis=1)
    w_hi = w.astype(BF16)
    w_lo = (w - w_hi.astype(F32)).astype(BF16)
    b = jnp.concatenate([b_group, b_router, jnp.zeros((pad,), F32)]).reshape(1, ROUTER_LANES)
    return jnp.concatenate([w_hi, w_lo], axis=1), b


def kernel(x_prompt, x_sample, cache_mla_ckv, cache_mla_krope, cache_diff_k, cache_diff_v, c, c_ctx, norm1_g, norm2_g, ada_w, ada_b, mla_w_in, mla_q_a_norm, mla_w_q_up, mla_kv_a_norm, mla_w_kv_up, mla_q_norm, mla_k_norm, mla_w_out, diff_w_qkv, diff_q_norm, diff_k_norm, diff_lambda_q1, diff_lambda_k1, diff_lambda_q2, diff_lambda_k2, diff_out_norm, diff_w_out, moe_w_group, moe_b_group, moe_w_router, moe_b_router, moe_w_gate, moe_w_up, moe_w_down):
    assert DEPTH == 2
    xp0 = x_prompt.reshape(NP, D_MODEL)
    xs0 = x_sample.reshape(NS, D_MODEL)
    cond = jnp.concatenate([c_ctx[None, :], c, jnp.zeros((N_COND - 1 - DEC_BATCH, D_MODEL), F32)], axis=0)
    mods = _ada_call(cond, ada_w, ada_b).reshape(DEPTH, N_COND, 1, 6 * D_MODEL)
    routers = [_router_weights(moe_w_group[i], moe_b_group[i], moe_w_router[i], moe_b_router[i])
               for i in range(DEPTH)]
    g2 = [norm2_g[i].reshape(1, D_MODEL) for i in range(DEPTH)]

    mod = mods[0]
    cos, sin = _rope_tables(MLA_ROPE)
    w_in = jnp.pad(mla_w_in[0], ((0, 0), (0, MLA_IN_PAD - MLA_IN))).astype(BF16)
    w_q = jnp.pad(mla_w_q_up[0].reshape(MLA_Q_LORA, MLA_HEADS, MLA_QK), ((0, 0), (0, 0), (0, MLA_HEAD_PAD - MLA_QK)))
    w_q = w_q.reshape(MLA_Q_LORA, MLA_HEADS * MLA_HEAD_PAD).astype(BF16)
    cq, ckv, kr = _mla_in_call(xp0, xs0, norm1_g[0].reshape(1, D_MODEL), mod, w_in,
                               mla_q_a_norm[0].reshape(1, -1), mla_kv_a_norm[0].reshape(1, -1))
    q = _q_up_call(cq, w_q, _pad_head_vec(mla_q_norm[0]), cos, sin)
    ckv_all = jnp.concatenate([ckv, cache_mla_ckv[:, 0].reshape(NCTX, MLA_KV_LORA)], axis=0)
    kr_ctx = jnp.pad(cache_mla_krope[:, 0].reshape(NCTX, MLA_ROPE), ((0, 0), (0, 128 - MLA_ROPE)))
    kr_all = jnp.concatenate([kr, kr_ctx], axis=0)
    k, v = _kv_up_call(ckv_all, kr_all, mla_w_kv_up[0].astype(BF16), _pad_head_vec(mla_k_norm[0]), cos, sin)
    o_p = _mla_attn_prompt_call(q, k, v)
    o_s = _mla_attn_latent_call(q, k, v)
    x = _out_proj_call(o_p, o_s, mla_w_out[0].astype(BF16), xp0, xs0, True, mod)
    new_ckv = ckv[:NP].reshape(BATCH, 1, SEQ, MLA_KV_LORA)
    new_krope = kr[:NP, :MLA_ROPE].reshape(BATCH, 1, SEQ, MLA_ROPE)
    x, h = _moe(0, x, g2[0], mod, *routers[0], moe_w_gate, moe_w_up, moe_w_down,
                next_norm=(norm1_g[1].reshape(1, D_MODEL), mods[1]))

    mod = mods[1]
    lam_init = 0.8 - 0.6 * math.exp(-0.3 * 1)
    cos, sin = _rope_tables(DIFF_HEAD_DIM)
    q, k_p, k_s, v_p, v_s = _diff_qkv_call(h, diff_w_qkv[0].astype(BF16), diff_q_norm[0].reshape(1, -1),
                                           diff_k_norm[0].reshape(1, -1), cos, sin)
    lams = [t[0].reshape(1, DIFF_HEAD_DIM) for t in
            (diff_lambda_q1, diff_lambda_k1, diff_lambda_q2, diff_lambda_k2)]
    out_norm = diff_out_norm[0].reshape(1, DIFF_V_DIM)
    o_p = _diff_attn_prompt_call(q, k_p, v_p, lams, out_norm, lam_init)
    o_s = _diff_attn_latent_call(q, k_s, v_s, cache_diff_k[:, 0].reshape(NCTX, D_MODEL),
                                 cache_diff_v[:, 0].reshape(NCTX, D_MODEL), lams, out_norm, lam_init)
    x = _out_proj_call(o_p, o_s, diff_w_out[0].astype(BF16), x, x, False, mod)
    new_k = k_p.reshape(BATCH, 1, SEQ, DIFF_HEADS, 2, DIFF_HEAD_DIM)
    new_v = v_p.reshape(BATCH, 1, SEQ, DIFF_HEADS, DIFF_V_DIM)
    y_p, y_s = _moe(1, x, g2[1], mod, *routers[1], moe_w_gate, moe_w_up, moe_w_down, split_out=True)

    return (y_p.reshape(BATCH, SEQ, D_MODEL), y_s.reshape(DEC_BATCH, DEC_SEQ, D_MODEL),
            new_ckv, new_krope, new_k, new_v)
```

```python
import functools
import math

import jax
import jax.numpy as jnp
from jax import lax
from jax.experimental import pallas as pl
from jax.experimental.pallas import tpu as pltpu

F32 = jnp.float32
BF16 = jnp.bfloat16
I32 = jnp.int32
U32 = jnp.uint32

D_MODEL = 4096
BATCH = 32
SEQ = 256
DEPTH = 2
DEC_BATCH = 4
DEC_SEQ = 1024
PAST_LEN = 256
GRID_W = 64
MLA_HEADS = 32
MLA_NOPE = 128
MLA_ROPE = 64
MLA_QK = MLA_NOPE + MLA_ROPE
MLA_V = 128
MLA_Q_LORA = 1024
MLA_KV_LORA = 512
MLA_IN = MLA_Q_LORA + MLA_KV_LORA + MLA_ROPE
MLA_IN_PAD = MLA_Q_LORA + MLA_KV_LORA + 128
MLA_HEAD_PAD = 256
DIFF_HEAD_DIM = 128
DIFF_HEADS = 16
DIFF_V_DIM = 256
N_GROUPS = 8
EXPERTS_PER_GROUP = 8
N_EXPERTS = 64
TOP_K = 2
EXPERT_HIDDEN = 1024
ROPE_THETA = 10000.0
NORM_EPS = 1e-6

NP = BATCH * SEQ
NS = DEC_BATCH * DEC_SEQ
TT = NP + NS
NCTX = DEC_BATCH * PAST_LEN
N_COND = 8
N_ASSIGN = TT * TOP_K
ROUTER_LANES = 128
HALF_D = D_MODEL // 2
ROW_TILES = HALF_D // 128

MOE_ROWS = 512
PIECE_ROWS = 512
PIECE_COLS = EXPERT_HIDDEN
GU_CHUNKS = D_MODEL // PIECE_ROWS
DN_PHASES = D_MODEL // PIECE_COLS
N_PHASES = GU_CHUNKS + DN_PHASES
LOOKAHEAD = 4
RING_SLOTS = 2 * (LOOKAHEAD + 1)
OUT_COLS = 512
OUT_CHUNKS = D_MODEL // OUT_COLS
OUT_SLOTS = 4
GU_PHASE_ROWS = 48
DN_PHASE_ROWS = (MOE_ROWS - GU_CHUNKS * GU_PHASE_ROWS) // DN_PHASES
ROW_WAIT_GROUP = 64
N_BLOCKS = N_ASSIGN // MOE_ROWS + N_EXPERTS + 1
N_SLOTS = N_BLOCKS * MOE_ROWS

VMEM_LIMIT = 56 * 1024 * 1024


def _cparams(sem):
    return pltpu.CompilerParams(dimension_semantics=sem, vmem_limit_bytes=VMEM_LIMIT)


def _cond_row(row0):
    return jnp.where(row0 < NP, 0, 1 + (row0 - NP) // DEC_SEQ)


def _rms(x, g):
    return x * lax.rsqrt(jnp.mean(x * x, axis=-1, keepdims=True) + NORM_EPS) * g


def _modulate(x, g, shift, scale):
    return _rms(x, g) * (1.0 + scale) + shift


def _dot(a, b):
    return jnp.dot(a, b, preferred_element_type=F32)


def _dot_nt(a, b):
    return lax.dot_general(a, b, (((1,), (1,)), ((), ())), preferred_element_type=F32)


def _rope128(x, cos, sin_signed, half):
    lane = lax.broadcasted_iota(I32, x.shape, 1)
    first_half = ((lane // half) % 2) == 0
    partner = jnp.where(first_half, pltpu.roll(x, 128 - half, 1), pltpu.roll(x, half, 1))
    return x * cos + partner * sin_signed


def _latent_tables(cos_ref, sin_ref, is_latent):
    m = is_latent.astype(F32)
    return 1.0 + m * (cos_ref[...] - 1.0), m * sin_ref[...]


def _row_pair_specs(tm, tn, split):
    npt = NP // tm
    off = npt if split else 0
    last = D_MODEL // tn - 1

    def prompt_map(i, *a):
        return (jnp.minimum(i, npt - 1), jnp.where(i < npt, a[0], last) if last else 0)

    def latent_map(i, *a):
        return (jnp.maximum(i, npt) - off, jnp.where(i >= npt, a[0], 0) if last else 0)

    return pl.BlockSpec((tm, tn), prompt_map), pl.BlockSpec((tm, tn), latent_map)


def _ada_kernel(c_ref, w_ref, b_ref, o_ref):
    c = c_ref[...]
    a = (c * jax.nn.sigmoid(c)).astype(BF16)
    o_ref[...] = _dot(a, w_ref[...].astype(BF16)) + b_ref[...]


def _ada_call(cond, ada_w, ada_b):
    tn = 512
    n = 6 * D_MODEL
    return pl.pallas_call(
        _ada_kernel,
        grid=(DEPTH, n // tn),
        in_specs=[
            pl.BlockSpec((N_COND, D_MODEL), lambda l, j: (0, 0)),
            pl.BlockSpec((None, D_MODEL, tn), lambda l, j: (l, 0, j)),
            pl.BlockSpec((None, 1, tn), lambda l, j: (l, 0, j)),
        ],
        out_specs=pl.BlockSpec((None, N_COND, tn), lambda l, j: (l, 0, j)),
        out_shape=jax.ShapeDtypeStruct((DEPTH, N_COND, n), F32),
        compiler_params=_cparams(("arbitrary", "arbitrary")),
        name="ada_table",
    )(cond, ada_w, ada_b.reshape(DEPTH, 1, n))


def _mod_spec(tm, chunk, width=D_MODEL):
    per = D_MODEL // width
    return pl.BlockSpec((None, 1, width),
                        lambda i, *a: (_cond_row(i * tm), 0, chunk * per + (a[0] if per > 1 else 0)))


def _mla_in_kernel(xp_ref, xs_ref, g_ref, sh_ref, sc_ref, w_ref, qg_ref, kvg_ref, cq_ref, ckv_ref, kr_ref, *, tm):
    def run(x_ref):
        h = _modulate(x_ref[...], g_ref[...], sh_ref[...], sc_ref[...]).astype(BF16)
        p = _dot(h, w_ref[...])
        cq_ref[...] = _rms(p[:, :MLA_Q_LORA], qg_ref[...]).astype(BF16)
        ckv_ref[...] = _rms(p[:, MLA_Q_LORA:MLA_Q_LORA + MLA_KV_LORA], kvg_ref[...])
        kr_ref[...] = p[:, MLA_Q_LORA + MLA_KV_LORA:]

    is_prompt = pl.program_id(0) < NP // tm
    pl.when(is_prompt)(lambda: run(xp_ref))
    pl.when(jnp.logical_not(is_prompt))(lambda: run(xs_ref))


def _mla_in_call(x_prompt, x_latent, g, mod, w_in, q_a_norm, kv_a_norm):
    tm = 256
    xp_spec, xs_spec = _row_pair_specs(tm, D_MODEL, split=True)
    return pl.pallas_call(
        functools.partial(_mla_in_kernel, tm=tm),
        grid=(TT // tm,),
        in_specs=[
            xp_spec,
            xs_spec,
            pl.BlockSpec((1, D_MODEL), lambda i: (0, 0)),
            _mod_spec(tm, 0),
            _mod_spec(tm, 1),
            pl.BlockSpec((D_MODEL, MLA_IN_PAD), lambda i: (0, 0)),
            pl.BlockSpec((1, MLA_Q_LORA), lambda i: (0, 0)),
            pl.BlockSpec((1, MLA_KV_LORA), lambda i: (0, 0)),
        ],
        out_specs=[
            pl.BlockSpec((tm, MLA_Q_LORA), lambda i: (i, 0)),
            pl.BlockSpec((tm, MLA_KV_LORA), lambda i: (i, 0)),
            pl.BlockSpec((tm, 128), lambda i: (i, 0)),
        ],
        out_shape=[
            jax.ShapeDtypeStruct((TT, MLA_Q_LORA), BF16),
            jax.ShapeDtypeStruct((TT, MLA_KV_LORA), F32),
            jax.ShapeDtypeStruct((TT, 128), F32),
        ],
        compiler_params=_cparams(("arbitrary",)),
        name="mla_in",
    )(x_prompt, x_latent, g, mod, mod, w_in, q_a_norm, kv_a_norm)


def _q_up_kernel(c_ref, w_ref, g_ref, cos_ref, sin_ref, q_ref, *, tm, heads):
    i = pl.program_id(0)
    cos, sin = _latent_tables(cos_ref, sin_ref, i >= NP // tm)
    o = _dot(c_ref[...], w_ref[...])
    g = g_ref[...] * (MLA_QK ** -0.5)
    for hh in range(heads):
        s = o[:, hh * 256:(hh + 1) * 256]
        r = lax.rsqrt(jnp.sum(s * s, axis=-1, keepdims=True) / MLA_QK + NORM_EPS)
        lo = s[:, :128] * r * g[:, :128]
        hi = _rope128(s[:, 128:] * r * g[:, 128:], cos, sin, MLA_ROPE // 4)
        q_ref[:, hh * 256:hh * 256 + 128] = lo.astype(BF16)
        q_ref[:, hh * 256 + 128:(hh + 1) * 256] = hi.astype(BF16)


def _pos_block(i, tm):
    return jnp.clip(i - NP // tm, 0, NS // tm - 1) % (DEC_SEQ // tm)


def _q_up_call(cq, w_q, q_norm_pad, cos, sin):
    tm, heads = 512, 8
    tn = heads * MLA_HEAD_PAD
    return pl.pallas_call(
        functools.partial(_q_up_kernel, tm=tm, heads=heads),
        grid=(TT // tm, MLA_HEADS // heads),
        in_specs=[
            pl.BlockSpec((tm, MLA_Q_LORA), lambda i, j: (i, 0)),
            pl.BlockSpec((MLA_Q_LORA, tn), lambda i, j: (0, j)),
            pl.BlockSpec((1, MLA_HEAD_PAD), lambda i, j: (0, 0)),
            pl.BlockSpec((tm, 128), lambda i, j: (_pos_block(i, tm), 0)),
            pl.BlockSpec((tm, 128), lambda i, j: (_pos_block(i, tm), 0)),
        ],
        out_specs=pl.BlockSpec((tm, tn), lambda i, j: (i, j)),
        out_shape=jax.ShapeDtypeStruct((TT, MLA_HEADS * MLA_HEAD_PAD), BF16),
        compiler_params=_cparams(("arbitrary", "arbitrary")),
        name="mla_q_up",
    )(cq, w_q, q_norm_pad, cos, sin)


def _kv_up_kernel(c_ref, kr_ref, w_ref, g_ref, cos_ref, sin_ref, k_ref, v_ref, *, tm, heads):
    i = pl.program_id(0)
    cos, sin = _latent_tables(cos_ref, sin_ref, (i >= NP // tm) & (i < TT // tm))
    kv = _dot(c_ref[...].astype(BF16), w_ref[...])
    g = g_ref[...]
    kr = kr_ref[...]
    ss_rope = jnp.sum(kr * kr, axis=-1, keepdims=True)
    base = _rope128(kr * g[:, 128:], cos, sin, MLA_ROPE // 4)
    for hh in range(heads):
        kn = kv[:, hh * 256:hh * 256 + 128]
        r = lax.rsqrt((jnp.sum(kn * kn, axis=-1, keepdims=True) + ss_rope) / MLA_QK + NORM_EPS)
        k_ref[:, hh * 256:hh * 256 + 128] = (kn * r * g[:, :128]).astype(BF16)
        k_ref[:, hh * 256 + 128:(hh + 1) * 256] = (base * r).astype(BF16)
        v_ref[:, hh * 128:(hh + 1) * 128] = kv[:, hh * 256 + 128:(hh + 1) * 256].astype(BF16)


def _kv_up_call(ckv_all, kr_all, w_kv, k_norm_pad, cos, sin):
    tm, heads = 512, 8
    rows = TT + NCTX
    return pl.pallas_call(
        functools.partial(_kv_up_kernel, tm=tm, heads=heads),
        grid=(rows // tm, MLA_HEADS // heads),
        in_specs=[
            pl.BlockSpec((tm, MLA_KV_LORA), lambda i, j: (i, 0)),
            pl.BlockSpec((tm, 128), lambda i, j: (i, 0)),
            pl.BlockSpec((MLA_KV_LORA, heads * 256), lambda i, j: (0, j)),
            pl.BlockSpec((1, MLA_HEAD_PAD), lambda i, j: (0, 0)),
            pl.BlockSpec((tm, 128), lambda i, j: (_pos_block(i, tm), 0)),
            pl.BlockSpec((tm, 128), lambda i, j: (_pos_block(i, tm), 0)),
        ],
        out_specs=[
            pl.BlockSpec((tm, heads * MLA_HEAD_PAD), lambda i, j: (i, j)),
            pl.BlockSpec((tm, heads * MLA_V), lambda i, j: (i, j)),
        ],
        out_shape=[
            jax.ShapeDtypeStruct((rows, MLA_HEADS * MLA_HEAD_PAD), BF16),
            jax.ShapeDtypeStruct((rows, MLA_HEADS * MLA_V), BF16),
        ],
        compiler_params=_cparams(("arbitrary", "arbitrary")),
        name="mla_kv_up",
    )(ckv_all, kr_all, w_kv, k_norm_pad, cos, sin)


def _mla_attn_prompt_kernel(q_ref, k_ref, v_ref, o_ref, *, heads):
    for hh in range(heads):
        q = q_ref[:, hh * 256:(hh + 1) * 256]
        k = k_ref[:, hh * 256:(hh + 1) * 256]
        s = _dot_nt(q, k)
        p = jnp.exp(s - jnp.max(s, axis=-1, keepdims=True))
        l = jnp.sum(p, axis=-1, keepdims=True)
        o = _dot(p.astype(BF16), v_ref[:, hh * 128:(hh + 1) * 128]) / l
        o_ref[:, hh * 128:(hh + 1) * 128] = o.astype(BF16)


def _mla_attn_prompt_call(q, k, v):
    heads = 8
    return pl.pallas_call(
        functools.partial(_mla_attn_prompt_kernel, heads=heads),
        grid=(BATCH, MLA_HEADS // heads),
        in_specs=[
            pl.BlockSpec((SEQ, heads * 256), lambda b, j: (b, j)),
            pl.BlockSpec((SEQ, heads * 256), lambda b, j: (b, j)),
            pl.BlockSpec((SEQ, heads * 128), lambda b, j: (b, j)),
        ],
        out_specs=pl.BlockSpec((SEQ, heads * 128), lambda b, j: (b, j)),
        out_shape=jax.ShapeDtypeStruct((NP, MLA_HEADS * MLA_V), BF16),
        compiler_params=_cparams(("arbitrary", "arbitrary")),
        name="mla_attn_prompt",
    )(q, k, v)


def _mla_attn_latent_kernel(q_ref, kl_ref, kc_ref, vl_ref, vc_ref, o_ref, *, heads):
    for hh in range(heads):
        q = q_ref[:, hh * 256:(hh + 1) * 256]
        sl = _dot_nt(q, kl_ref[:, hh * 256:(hh + 1) * 256])
        sc = _dot_nt(q, kc_ref[:, hh * 256:(hh + 1) * 256])
        m = jnp.maximum(jnp.max(sl, axis=-1, keepdims=True), jnp.max(sc, axis=-1, keepdims=True))
        pl_ = jnp.exp(sl - m)
        pc = jnp.exp(sc - m)
        l = jnp.sum(pl_, axis=-1, keepdims=True) + jnp.sum(pc, axis=-1, keepdims=True)
        o = _dot(pl_.astype(BF16), vl_ref[:, hh * 128:(hh + 1) * 128])
        o = o + _dot(pc.astype(BF16), vc_ref[:, hh * 128:(hh + 1) * 128])
        o_ref[:, hh * 128:(hh + 1) * 128] = (o / l).astype(BF16)


def _mla_attn_latent_call(q, k, v):
    heads, tq = 8, 512
    nq = DEC_SEQ // tq
    return pl.pallas_call(
        functools.partial(_mla_attn_latent_kernel, heads=heads),
        grid=(DEC_BATCH, MLA_HEADS // heads, nq),
        in_specs=[
            pl.BlockSpec((tq, heads * 256), lambda b, j, t: (NP // tq + b * nq + t, j)),
            pl.BlockSpec((DEC_SEQ, heads * 256), lambda b, j, t: (NP // DEC_SEQ + b, j)),
            pl.BlockSpec((PAST_LEN, heads * 256), lambda b, j, t: (TT // PAST_LEN + b, j)),
            pl.BlockSpec((DEC_SEQ, heads * 128), lambda b, j, t: (NP // DEC_SEQ + b, j)),
            pl.BlockSpec((PAST_LEN, heads * 128), lambda b, j, t: (TT // PAST_LEN + b, j)),
        ],
        out_specs=pl.BlockSpec((tq, heads * 128), lambda b, j, t: (b * nq + t, j)),
        out_shape=jax.ShapeDtypeStruct((NS, MLA_HEADS * MLA_V), BF16),
        compiler_params=_cparams(("arbitrary", "arbitrary", "arbitrary")),
        name="mla_attn_latent",
    )(q, k, k, v, v)


def _out_proj_kernel(ap_ref, as_ref, w_ref, xp_ref, xs_ref, gate_ref, o_ref, *, tm):
    def finish(a_ref, x_ref):
        o_ref[...] = x_ref[...] + gate_ref[...] * _dot(a_ref[...], w_ref[...])

    is_prompt = pl.program_id(0) < NP // tm
    pl.when(is_prompt)(lambda: finish(ap_ref, xp_ref))
    pl.when(jnp.logical_not(is_prompt))(lambda: finish(as_ref, xs_ref))


def _out_proj_call(a_prompt, a_latent, w, x_prompt, x_latent, x_split, mod):
    tm, tn = 512, 1024
    ap_spec, as_spec = _row_pair_specs(tm, D_MODEL, split=True)
    xp_spec, xs_spec = _row_pair_specs(tm, tn, split=x_split)
    return pl.pallas_call(
        functools.partial(_out_proj_kernel, tm=tm),
        grid=(TT // tm, D_MODEL // tn),
        in_specs=[
            ap_spec,
            as_spec,
            pl.BlockSpec((D_MODEL, tn), lambda i, j: (0, j)),
            xp_spec,
            xs_spec,
            _mod_spec(tm, 2, width=tn),
        ],
        out_specs=pl.BlockSpec((tm, tn), lambda i, j: (i, j)),
        out_shape=jax.ShapeDtypeStruct((TT, D_MODEL), F32),
        compiler_params=_cparams(("arbitrary", "arbitrary")),
        name="attn_out_proj",
    )(a_prompt, a_latent, w, x_prompt, x_latent, mod)


def _diff_qkv_kernel(h_ref, w_ref, qn_ref, kn_ref, cos_ref, sin_ref, q_ref, kp_ref, kl_ref, vp_ref, vl_ref,
                     *, tm, nq, sub):
    i = pl.program_id(0)
    j = pl.program_id(1)
    is_prompt = i < NP // tm
    cos, sin = _latent_tables(cos_ref, sin_ref, jnp.logical_not(is_prompt))
    n_sub = w_ref.shape[1] // sub

    def store_rows(prompt_ref, latent_ref, pieces):
        def fill(ref):
            for c0, y in pieces:
                ref[:, c0:c0 + y.shape[1]] = y
        pl.when(is_prompt)(lambda: fill(prompt_ref))
        pl.when(jnp.logical_not(is_prompt))(lambda: fill(latent_ref))

    def norm_rope(o, g, scale):
        return [_rope128(_rms(o[:, s * 128:(s + 1) * 128], g) * scale, cos, sin, DIFF_HEAD_DIM // 4)
                for s in range(sub // 128)]

    @pl.when(j < nq)
    def _():
        for t in range(n_sub):
            o = _dot(h_ref[...], w_ref[:, t * sub:(t + 1) * sub])
            for s, y in enumerate(norm_rope(o, qn_ref[...], DIFF_HEAD_DIM ** -0.5)):
                q_ref[:, t * sub + s * 128:t * sub + (s + 1) * 128] = y.astype(BF16)

    @pl.when((j >= nq) & (j < 2 * nq))
    def _():
        pieces = []
        for t in range(n_sub):
            o = _dot(h_ref[...], w_ref[:, t * sub:(t + 1) * sub])
            pieces += [(t * sub + s * 128, y) for s, y in enumerate(norm_rope(o, kn_ref[...], 1.0))]
        store_rows(kp_ref, kl_ref, pieces)

    @pl.when(j >= 2 * nq)
    def _():
        store_rows(vp_ref, vl_ref, [(0, _dot(h_ref[...], w_ref[...]))])


def _diff_qkv_call(h, w_qkv, q_norm, k_norm, cos, sin):
    tm, tn, sub = 512, 1024, 256
    nq = D_MODEL // tn
    npt = NP // tm

    def kv_specs(first):
        col = lambda j: jnp.clip(j - first, 0, nq - 1)
        return [pl.BlockSpec((tm, tn), lambda i, j: (jnp.minimum(i, npt - 1), jnp.where(i < npt, col(j), nq - 1))),
                pl.BlockSpec((tm, tn), lambda i, j: (jnp.maximum(i - npt, 0), jnp.where(i >= npt, col(j), 0)))]

    kv_shapes = [jax.ShapeDtypeStruct((NP, D_MODEL), F32), jax.ShapeDtypeStruct((NS, D_MODEL), F32)]
    return pl.pallas_call(
        functools.partial(_diff_qkv_kernel, tm=tm, nq=nq, sub=sub),
        grid=(TT // tm, 3 * nq),
        in_specs=[
            pl.BlockSpec((tm, D_MODEL), lambda i, j: (i, 0)),
            pl.BlockSpec((D_MODEL, tn), lambda i, j: (0, j)),
            pl.BlockSpec((1, DIFF_HEAD_DIM), lambda i, j: (0, 0)),
            pl.BlockSpec((1, DIFF_HEAD_DIM), lambda i, j: (0, 0)),
            pl.BlockSpec((tm, 128), lambda i, j: (_pos_block(i, tm), 0)),
            pl.BlockSpec((tm, 128), lambda i, j: (_pos_block(i, tm), 0)),
        ],
        out_specs=[pl.BlockSpec((tm, tn), lambda i, j: (i, jnp.minimum(j, nq - 1)))]
        + kv_specs(nq) + kv_specs(2 * nq),
        out_shape=[jax.ShapeDtypeStruct((TT, D_MODEL), BF16)] + kv_shapes + kv_shapes,
        compiler_params=_cparams(("arbitrary", "arbitrary")),
        name="diff_qkv",
    )(h, w_qkv, q_norm, k_norm, cos, sin)


def _diff_lambda(lq1_ref, lk1_ref, lq2_ref, lk2_ref, lam_init):
    e1 = jnp.exp(jnp.sum(lq1_ref[...] * lk1_ref[...], axis=-1, keepdims=True))
    e2 = jnp.exp(jnp.sum(lq2_ref[...] * lk2_ref[...], axis=-1, keepdims=True))
    return e1 - e2 + lam_init


def _softmax_rows(s):
    p = jnp.exp(s - jnp.max(s, axis=-1, keepdims=True))
    return p / jnp.sum(p, axis=-1, keepdims=True)


def _diff_attn_prompt_kernel(q_ref, k_ref, v_ref, lq1, lk1, lq2, lk2, on_ref, o_ref, *, heads, lam_init):
    lam = _diff_lambda(lq1, lk1, lq2, lk2, lam_init)
    for hh in range(heads):
        c0 = hh * 256
        p1 = _softmax_rows(_dot_nt(q_ref[:, c0:c0 + 128], k_ref[:, c0:c0 + 128].astype(BF16)))
        p2 = _softmax_rows(_dot_nt(q_ref[:, c0 + 128:c0 + 256], k_ref[:, c0 + 128:c0 + 256].astype(BF16)))
        a = (p1 - lam * p2).astype(BF16)
        o = _dot(a, v_ref[:, c0:c0 + 256].astype(BF16))
        o_ref[:, c0:c0 + 256] = (_rms(o, on_ref[...]) * (1.0 - lam_init)).astype(BF16)


def _vec_spec(n):
    return pl.BlockSpec((1, n), lambda *a: (0, 0))


def _diff_attn_prompt_call(q, k, v, lams, out_norm, lam_init):
    heads = 4
    w = heads * 256
    blk = pl.BlockSpec((SEQ, w), lambda b, j: (b, j))
    return pl.pallas_call(
        functools.partial(_diff_attn_prompt_kernel, heads=heads, lam_init=lam_init),
        grid=(BATCH, DIFF_HEADS // heads),
        in_specs=[blk, blk, blk] + [_vec_spec(DIFF_HEAD_DIM)] * 4 + [_vec_spec(DIFF_V_DIM)],
        out_specs=blk,
        out_shape=jax.ShapeDtypeStruct((NP, D_MODEL), BF16),
        compiler_params=_cparams(("arbitrary", "arbitrary")),
        name="diff_attn_prompt",
    )(q, k, v, *lams, out_norm)


def _softmax2(sa, sb):
    m = jnp.maximum(jnp.max(sa, axis=-1, keepdims=True), jnp.max(sb, axis=-1, keepdims=True))
    pa = jnp.exp(sa - m)
    pb = jnp.exp(sb - m)
    l = jnp.sum(pa, axis=-1, keepdims=True) + jnp.sum(pb, axis=-1, keepdims=True)
    return pa / l, pb / l


def _diff_attn_latent_kernel(q_ref, kl_ref, kc_ref, vl_ref, vc_ref, lq1, lk1, lq2, lk2, on_ref,
                             o_ref, *, heads, lam_init):
    lam = _diff_lambda(lq1, lk1, lq2, lk2, lam_init)
    for hh in range(heads):
        c0 = hh * 256
        q1 = q_ref[:, c0:c0 + 128]
        q2 = q_ref[:, c0 + 128:c0 + 256]
        p1l, p1c = _softmax2(_dot_nt(q1, kl_ref[:, c0:c0 + 128].astype(BF16)),
                             _dot_nt(q1, kc_ref[:, c0:c0 + 128].astype(BF16)))
        p2l, p2c = _softmax2(_dot_nt(q2, kl_ref[:, c0 + 128:c0 + 256].astype(BF16)),
                             _dot_nt(q2, kc_ref[:, c0 + 128:c0 + 256].astype(BF16)))
        o = _dot((p1l - lam * p2l).astype(BF16), vl_ref[:, c0:c0 + 256].astype(BF16))
        o = o + _dot((p1c - lam * p2c).astype(BF16), vc_ref[:, c0:c0 + 256].astype(BF16))
        o_ref[:, c0:c0 + 256] = (_rms(o, on_ref[...]) * (1.0 - lam_init)).astype(BF16)


def _diff_attn_latent_call(q, k, v, k_ctx, v_ctx, lams, out_norm, lam_init):
    heads, tq = 4, 256
    w = heads * 256
    nq = DEC_SEQ // tq
    lat = pl.BlockSpec((DEC_SEQ, w), lambda b, j, t: (b, j))
    ctx = pl.BlockSpec((PAST_LEN, w), lambda b, j, t: (b, j))
    return pl.pallas_call(
        functools.partial(_diff_attn_latent_kernel, heads=heads, lam_init=lam_init),
        grid=(DEC_BATCH, DIFF_HEADS // heads, nq),
        in_specs=[pl.BlockSpec((tq, w), lambda b, j, t: (NP // tq + b * nq + t, j)), lat, ctx, lat, ctx]
        + [_vec_spec(DIFF_HEAD_DIM)] * 4 + [_vec_spec(DIFF_V_DIM)],
        out_specs=pl.BlockSpec((tq, w), lambda b, j, t: (b * nq + t, j)),
        out_shape=jax.ShapeDtypeStruct((NS, D_MODEL), BF16),
        compiler_params=_cparams(("arbitrary", "arbitrary", "arbitrary")),
        name="diff_attn_latent",
    )(q, k, k_ctx, v, v_ctx, *lams, out_norm)


def _router_kernel(x_ref, g_ref, sh_ref, sc_ref, w_ref, b_ref, h_ref, ids_ref, gates_ref):
    h = _modulate(x_ref[...], g_ref[...], sh_ref[...], sc_ref[...])
    h_hi = h.astype(BF16)
    h_hi32 = h_hi.astype(F32)
    bits = lax.bitcast_convert_type(h_hi32, U32)
    h_ref[...] = (bits[:, :HALF_D] >> 16) | (bits[:, HALF_D:] & jnp.uint32(0xFFFF0000))

    h_lo = (h - h_hi32).astype(BF16)
    two = _dot(h_hi, w_ref[...])
    logits = two[:, :ROUTER_LANES] + two[:, ROUTER_LANES:] + _dot(h_lo, w_ref[:, :ROUTER_LANES]) + b_ref[...]

    lane = lax.broadcasted_iota(I32, logits.shape, 1).astype(F32)
    neg = -jnp.inf
    gl = jnp.where(lane < N_GROUPS, logits, neg)
    gmax = jnp.max(gl, axis=-1, keepdims=True)
    g_sel = jnp.min(jnp.where(gl == gmax, lane, ROUTER_LANES), axis=-1, keepdims=True)
    p_group = 1.0 / jnp.sum(jnp.exp(gl - gmax), axis=-1, keepdims=True)

    lo = N_GROUPS + g_sel * EXPERTS_PER_GROUP
    el = jnp.where(lane >= lo, jnp.where(lane < lo + EXPERTS_PER_GROUP, logits, neg), neg)
    v1 = jnp.max(el, axis=-1, keepdims=True)
    i1 = jnp.min(jnp.where(el == v1, lane, ROUTER_LANES), axis=-1, keepdims=True)
    el2 = jnp.where(lane == i1, neg, el)
    v2 = jnp.max(el2, axis=-1, keepdims=True)
    i2 = jnp.min(jnp.where(el2 == v2, lane, ROUTER_LANES), axis=-1, keepdims=True)
    e2 = jnp.exp(v2 - v1)
    w1 = p_group * (1.0 / (1.0 + e2))
    w2 = p_group * (e2 / (1.0 + e2))
    ids = jnp.where(lane == 0, i1 - N_GROUPS, jnp.where(lane == 1, i2 - N_GROUPS, 0.0))
    ids_ref[...] = ids.astype(I32)
    gates_ref[...] = jnp.where(lane == 0, w1, jnp.where(lane == 1, w2, 0.0))


def _router_call(x, g, mod, w_router2, b_router):
    tm = 256
    return pl.pallas_call(
        _router_kernel,
        grid=(TT // tm,),
        in_specs=[
            pl.BlockSpec((tm, D_MODEL), lambda i: (i, 0)),
            pl.BlockSpec((1, D_MODEL), lambda i: (0, 0)),
            _mod_spec(tm, 3),
            _mod_spec(tm, 4),
            pl.BlockSpec((D_MODEL, 2 * ROUTER_LANES), lambda i: (0, 0)),
            pl.BlockSpec((1, ROUTER_LANES), lambda i: (0, 0)),
        ],
        out_specs=[
            pl.BlockSpec((tm, HALF_D), lambda i: (i, 0)),
            pl.BlockSpec((tm, ROUTER_LANES), lambda i: (i, 0)),
            pl.BlockSpec((tm, ROUTER_LANES), lambda i: (i, 0)),
        ],
        out_shape=[
            jax.ShapeDtypeStruct((TT, HALF_D), U32),
            jax.ShapeDtypeStruct((TT, ROUTER_LANES), I32),
            jax.ShapeDtypeStruct((TT, ROUTER_LANES), F32),
        ],
        compiler_params=_cparams(("arbitrary",)),
        name="moe_router",
    )(x, g, mod, mod, w_router2, b_router)


def _moe_ffn_kernel(tok_ref, be_ref, nb_ref, h_hbm, wg_hbm, wu_hbm, wd_hbm, ys_hbm,
                    xrows, x_lo, x_hi, ring, gate_acc, up_acc, hid, stage,
                    row_sem, ring_sem, out_sem, *, layer):
    b = pl.program_id(0)
    nb = nb_ref[0]

    def row_copy(blk, r):
        return pltpu.make_async_copy(h_hbm.at[tok_ref[blk * MOE_ROWS + r]],
                                     xrows.at[pl.ds(r * ROW_TILES, ROW_TILES), :], row_sem)

    def piece_slot(blk, t, k):
        return (blk * (2 * N_PHASES) + 2 * t + k) % RING_SLOTS

    def piece_copy(blk, t, k):
        e, s = be_ref[blk], piece_slot(blk, t, k)
        if t < GU_CHUNKS:
            src = (wg_hbm, wu_hbm)[k].at[layer, e, pl.ds(t * PIECE_ROWS, PIECE_ROWS), :]
        else:
            src = wd_hbm.at[layer, e, pl.ds(k * PIECE_ROWS, PIECE_ROWS),
                            pl.ds((t - GU_CHUNKS) * PIECE_COLS, PIECE_COLS)]
        return pltpu.make_async_copy(src, ring.at[s], ring_sem.at[s])

    def request_phase(blk, t):
        for k in range(2):
            piece_copy(blk, t, k).start()

    def await_phase(blk, t):
        for k in range(2):
            piece_copy(blk, t, k).wait()

    def out_copy(blk, c, src_slot):
        return pltpu.make_async_copy(
            stage.at[src_slot], ys_hbm.at[pl.ds(blk * MOE_ROWS, MOE_ROWS), pl.ds(c * OUT_COLS, OUT_COLS)],
            out_sem.at[src_slot])

    def request_next_rows(first, count):
        for r in range(count):
            row_copy(b + 1, first + r).start()

    def request_ahead(t):
        ahead = t + LOOKAHEAD
        if ahead < N_PHASES:
            request_phase(b, ahead)
        else:
            request_phase(b + 1, ahead - N_PHASES)

    @pl.when(b == 0)
    def _():
        for t in range(LOOKAHEAD):
            request_phase(0, t)

        def start(r, carry):
            row_copy(0, r).start()
            return carry
        lax.fori_loop(0, MOE_ROWS, start, 0)

    @pl.when(b <= nb)
    def _():
        def wait(g, carry):
            for r in range(ROW_WAIT_GROUP):
                row_copy(b, g * ROW_WAIT_GROUP + r).wait()
            return carry
        lax.fori_loop(0, MOE_ROWS // ROW_WAIT_GROUP, wait, 0)

    @pl.when(b < nb)
    def _():
        for q in range(ROW_TILES):
            words = xrows[pl.ds(q, MOE_ROWS, stride=ROW_TILES), :]
            x_lo[:, q * 128:(q + 1) * 128] = lax.bitcast_convert_type(words << 16, F32).astype(BF16)
            x_hi[:, q * 128:(q + 1) * 128] = lax.bitcast_convert_type(
                words & jnp.uint32(0xFFFF0000), F32).astype(BF16)

        for t in range(GU_CHUNKS):
            request_ahead(t)
            request_next_rows(t * GU_PHASE_ROWS, GU_PHASE_ROWS)
            await_phase(b, t)
            x_half = x_lo if t * PIECE_ROWS < HALF_D else x_hi
            k0 = (t * PIECE_ROWS) % HALF_D
            xk = x_half[:, k0:k0 + PIECE_ROWS]
            gate = _dot(xk, ring[piece_slot(b, t, 0)].astype(BF16))
            up = _dot(xk, ring[piece_slot(b, t, 1)].astype(BF16))
            if t > 0:
                gate = gate_acc[...] + gate
                up = up_acc[...] + up
            if t + 1 < GU_CHUNKS:
                gate_acc[...] = gate
                up_acc[...] = up
            else:
                hid[...] = (gate * jax.nn.sigmoid(gate) * up).astype(BF16)

        for n in range(DN_PHASES):
            t = GU_CHUNKS + n
            request_ahead(t)
            request_next_rows(GU_CHUNKS * GU_PHASE_ROWS + n * DN_PHASE_ROWS, DN_PHASE_ROWS)
            await_phase(b, t)
            acc = (_dot(hid[:, :PIECE_ROWS], ring[piece_slot(b, t, 0)].astype(BF16))
                   + _dot(hid[:, PIECE_ROWS:], ring[piece_slot(b, t, 1)].astype(BF16)))
            for half in range(PIECE_COLS // OUT_COLS):
                c = n * (PIECE_COLS // OUT_COLS) + half
                slot = c % OUT_SLOTS
                if c >= OUT_SLOTS:
                    out_copy(b, c - OUT_SLOTS, slot).wait()
                else:
                    prev = OUT_CHUNKS - OUT_SLOTS + c
                    pl.when(b > 0)(lambda prev=prev, slot=slot: out_copy(b - 1, prev, slot).wait())
                stage[slot] = acc[:, half * OUT_COLS:(half + 1) * OUT_COLS]
                out_copy(b, c, slot).start()

    @pl.when(b >= nb)
    def _():
        @pl.when(b == nb)
        def _():
            for c in range(OUT_CHUNKS - OUT_SLOTS, OUT_CHUNKS):
                out_copy(b - 1, c, c % OUT_SLOTS).wait()
            for t in range(LOOKAHEAD):
                await_phase(b, t)

        stage[0] = jnp.zeros(stage.shape[1:], F32)
        for c in range(OUT_CHUNKS):
            out_copy(b, c, 0).start()
        for c in range(OUT_CHUNKS):
            out_copy(b, c, 0).wait()


def _moe_ffn_call(layer, slot_tok, block_e, n_blocks, h_packed, w_gate, w_up, w_down):
    any_spec = pl.BlockSpec(memory_space=pl.ANY)
    return pl.pallas_call(
        functools.partial(_moe_ffn_kernel, layer=layer),
        grid_spec=pltpu.PrefetchScalarGridSpec(
            num_scalar_prefetch=3,
            grid=(N_BLOCKS,),
            in_specs=[any_spec] * 4,
            out_specs=any_spec,
            scratch_shapes=[
                pltpu.VMEM((MOE_ROWS * ROW_TILES, 128), U32),
                pltpu.VMEM((MOE_ROWS, HALF_D), BF16),
                pltpu.VMEM((MOE_ROWS, HALF_D), BF16),
                pltpu.VMEM((RING_SLOTS, PIECE_ROWS, PIECE_COLS), F32),
                pltpu.VMEM((MOE_ROWS, EXPERT_HIDDEN), F32),
                pltpu.VMEM((MOE_ROWS, EXPERT_HIDDEN), F32),
                pltpu.VMEM((MOE_ROWS, EXPERT_HIDDEN), BF16),
                pltpu.VMEM((OUT_SLOTS, MOE_ROWS, OUT_COLS), F32),
                pltpu.SemaphoreType.DMA(()),
                pltpu.SemaphoreType.DMA((RING_SLOTS,)),
                pltpu.SemaphoreType.DMA((OUT_SLOTS,)),
            ],
        ),
        out_shape=jax.ShapeDtypeStruct((N_SLOTS, D_MODEL), F32),
        compiler_params=_cparams(("arbitrary",)),
        name="moe_experts",
    )(slot_tok, block_e, n_blocks, h_packed.reshape(TT, ROW_TILES, 128), w_gate, w_up, w_down)


def _combine_kernel(pos_ref, ys_hbm, x_ref, gates_ref, gate2_ref, *rest, tm, unroll, emit_next, split_out):
    if emit_next:
        ng_ref, nsh_ref, nsc_ref = rest[:3]
        rest = rest[3:]
    outs, (buf, sem) = rest[:-2], rest[-2:]
    i = pl.program_id(0)
    base = i * (tm * TOP_K)

    def copies(r):
        return [pltpu.make_async_copy(ys_hbm.at[pl.ds(pos_ref[base + TOP_K * r + k], 1), :],
                                      buf.at[k, pl.ds(r, 1), :], sem) for k in range(TOP_K)]

    def start(t, c):
        for u in range(unroll):
            for cp in copies(t * unroll + u):
                cp.start()
        return c

    def wait(t, c):
        for u in range(unroll):
            for cp in copies(t * unroll + u):
                cp.wait()
        return c

    lax.fori_loop(0, tm // unroll, start, 0)
    lax.fori_loop(0, tm // unroll, wait, 0)
    gates = gates_ref[...]
    y = gates[:, 0:1] * buf[0] + gates[:, 1:2] * buf[1]
    x = x_ref[...] + gate2_ref[...] * y
    if split_out:
        is_prompt = i < NP // tm

        @pl.when(is_prompt)
        def _():
            outs[0][...] = x

        @pl.when(jnp.logical_not(is_prompt))
        def _():
            outs[1][...] = x
    else:
        outs[0][...] = x
    if emit_next:
        outs[-1][...] = _modulate(x, ng_ref[...], nsh_ref[...], nsc_ref[...]).astype(BF16)


def _combine_call(pos, ys, x, gates, mod, next_norm=None, split_out=False):
    tm, unroll = 256, 8
    npt = NP // tm
    row = lambda i, pos: (i, 0)
    in_specs = [
        pl.BlockSpec(memory_space=pl.ANY),
        pl.BlockSpec((tm, D_MODEL), row),
        pl.BlockSpec((tm, ROUTER_LANES), row),
        pl.BlockSpec((None, 1, D_MODEL), lambda i, pos: (_cond_row(i * tm), 0, 5)),
    ]
    args = [ys, x, gates, mod]
    if next_norm is not None:
        g_next, mod_next = next_norm
        in_specs += [
            pl.BlockSpec((1, D_MODEL), lambda i, pos: (0, 0)),
            pl.BlockSpec((None, 1, D_MODEL), lambda i, pos: (_cond_row(i * tm), 0, 0)),
            pl.BlockSpec((None, 1, D_MODEL), lambda i, pos: (_cond_row(i * tm), 0, 1)),
        ]
        args += [g_next, mod_next, mod_next]
    if split_out:
        out_specs = [pl.BlockSpec((tm, D_MODEL), lambda i, pos: (jnp.minimum(i, npt - 1), 0)),
                     pl.BlockSpec((tm, D_MODEL), lambda i, pos: (jnp.maximum(i - npt, 0), 0))]
        out_shape = [jax.ShapeDtypeStruct((NP, D_MODEL), F32), jax.ShapeDtypeStruct((NS, D_MODEL), F32)]
    else:
        out_specs = [pl.BlockSpec((tm, D_MODEL), row)]
        out_shape = [jax.ShapeDtypeStruct((TT, D_MODEL), F32)]
    if next_norm is not None:
        out_specs.append(pl.BlockSpec((tm, D_MODEL), row))
        out_shape.append(jax.ShapeDtypeStruct((TT, D_MODEL), BF16))
    return pl.pallas_call(
        functools.partial(_combine_kernel, tm=tm, unroll=unroll, emit_next=next_norm is not None,
                          split_out=split_out),
        grid_spec=pltpu.PrefetchScalarGridSpec(
            num_scalar_prefetch=1,
            grid=(TT // tm,),
            in_specs=in_specs,
            out_specs=out_specs,
            scratch_shapes=[pltpu.VMEM((TOP_K, tm, D_MODEL), F32), pltpu.SemaphoreType.DMA(())],
        ),
        out_shape=out_shape,
        compiler_params=_cparams(("arbitrary",)),
        name="moe_combine",
    )(pos, *args)


def _dispatch_plan(ids):
    flat_e = ids[:, :TOP_K].reshape(N_ASSIGN)
    onehot = (flat_e[:, None] == jnp.arange(N_EXPERTS, dtype=I32)[None, :]).astype(I32)
    csum = jnp.cumsum(onehot, axis=0)
    rank = jnp.take_along_axis(csum, flat_e[:, None], axis=1)[:, 0] - 1
    counts = csum[-1]
    padded = (counts + MOE_ROWS - 1) // MOE_ROWS * MOE_ROWS
    pad_ends = jnp.cumsum(padded)
    pad_starts = pad_ends - padded
    dest = (pad_starts[flat_e] + rank).astype(I32)
    slot_tok = jnp.zeros((N_SLOTS,), I32).at[dest].set(jnp.arange(N_ASSIGN, dtype=I32) // TOP_K)
    n_blocks = (pad_ends[-1] // MOE_ROWS).astype(I32)
    blk = jnp.minimum(jnp.arange(N_BLOCKS, dtype=I32), n_blocks - 1)
    block_e = jnp.minimum(jnp.searchsorted(pad_ends, blk * MOE_ROWS, side="right"), N_EXPERTS - 1).astype(I32)
    return dest, slot_tok, block_e, n_blocks.reshape(1)


def _moe(layer, x, g, mod, w_router2, b_router, w_gate, w_up, w_down, **combine_kw):
    h_packed, ids, gates = _router_call(x, g, mod, w_router2, b_router)
    dest, slot_tok, block_e, n_blocks = _dispatch_plan(ids)
    ys = _moe_ffn_call(layer, slot_tok, block_e, n_blocks, h_packed, w_gate, w_up, w_down)
    return _combine_call(dest, ys, x, gates, mod, **combine_kw)


def _rope_tables(rot_dim):
    rows = DEC_SEQ // GRID_W
    nf = rot_dim // 4
    inv = jnp.exp(-math.log(ROPE_THETA) * jnp.arange(nf, dtype=F32) / nf)
    row_pos = jnp.repeat(jnp.arange(rows, dtype=F32), GRID_W)
    col_pos = jnp.tile(jnp.arange(GRID_W, dtype=F32), rows)
    ang = jnp.stack([row_pos[:, None] * inv, col_pos[:, None] * inv], axis=1)
    cos = jnp.broadcast_to(jnp.cos(ang)[:, :, None, :], (DEC_SEQ, 2, 2, nf)).reshape(DEC_SEQ, rot_dim)
    sgn = jnp.array([-1.0, 1.0], F32)[None, None, :, None]
    sin = (jnp.sin(ang)[:, :, None, :] * sgn).reshape(DEC_SEQ, rot_dim)
    pad = 128 - rot_dim
    if pad:
        cos = jnp.concatenate([cos, jnp.ones((DEC_SEQ, pad), F32)], axis=1)
        sin = jnp.concatenate([sin, jnp.zeros((DEC_SEQ, pad), F32)], axis=1)
    return cos, sin


def _pad_head_vec(v):
    return jnp.concatenate([v, jnp.zeros((MLA_HEAD_PAD - MLA_QK,), F32)]).reshape(1, MLA_HEAD_PAD)


def _router_weights(w_group, b_group, w_router, b_router):
    pad = ROUTER_LANES - N_GROUPS - N_EXPERTS
    w = jnp.concatenate([w_group, w_router, jnp.zeros((D_MODEL, pad), F32)], axis=1)
    w_hi = w.astype(BF16)
    w_lo = (w - w_hi.astype(F32)).astype(BF16)
    b = jnp.concatenate([b_group, b_router, jnp.zeros((pad,), F32)]).reshape(1, ROUTER_LANES)
    return jnp.concatenate([w_hi, w_lo], axis=1), b


def kernel(x_prompt, x_sample, cache_mla_ckv, cache_mla_krope, cache_diff_k, cache_diff_v, c, c_ctx, norm1_g, norm2_g, ada_w, ada_b, mla_w_in, mla_q_a_norm, mla_w_q_up, mla_kv_a_norm, mla_w_kv_up, mla_q_norm, mla_k_norm, mla_w_out, diff_w_qkv, diff_q_norm, diff_k_norm, diff_lambda_q1, diff_lambda_k1, diff_lambda_q2, diff_lambda_k2, diff_out_norm, diff_w_out, moe_w_group, moe_b_group, moe_w_router, moe_b_router, moe_w_gate, moe_w_up, moe_w_down):
    assert DEPTH == 2
    xp0 = x_prompt.reshape(NP, D_MODEL)
    xs0 = x_sample.reshape(NS, D_MODEL)
    cond = jnp.concatenate([c_ctx[None, :], c, jnp.zeros((N_COND - 1 - DEC_BATCH, D_MODEL), F32)], axis=0)
    mods = _ada_call(cond, ada_w, ada_b).reshape(DEPTH, N_COND, 1, 6 * D_MODEL)
    routers = [_router_weights(moe_w_group[i], moe_b_group[i], moe_w_router[i], moe_b_router[i])
               for i in range(DEPTH)]
    g2 = [norm2_g[i].reshape(1, D_MODEL) for i in range(DEPTH)]

    mod = mods[0]
    cos, sin = _rope_tables(MLA_ROPE)
    w_in = jnp.pad(mla_w_in[0], ((0, 0), (0, MLA_IN_PAD - MLA_IN))).astype(BF16)
    w_q = jnp.pad(mla_w_q_up[0].reshape(MLA_Q_LORA, MLA_HEADS, MLA_QK), ((0, 0), (0, 0), (0, MLA_HEAD_PAD - MLA_QK)))
    w_q = w_q.reshape(MLA_Q_LORA, MLA_HEADS * MLA_HEAD_PAD).astype(BF16)
    cq, ckv, kr = _mla_in_call(xp0, xs0, norm1_g[0].reshape(1, D_MODEL), mod, w_in,
                               mla_q_a_norm[0].reshape(1, -1), mla_kv_a_norm[0].reshape(1, -1))
    q = _q_up_call(cq, w_q, _pad_head_vec(mla_q_norm[0]), cos, sin)
    ckv_all = jnp.concatenate([ckv, cache_mla_ckv[:, 0].reshape(NCTX, MLA_KV_LORA)], axis=0)
    kr_ctx = jnp.pad(cache_mla_krope[:, 0].reshape(NCTX, MLA_ROPE), ((0, 0), (0, 128 - MLA_ROPE)))
    kr_all = jnp.concatenate([kr, kr_ctx], axis=0)
    k, v = _kv_up_call(ckv_all, kr_all, mla_w_kv_up[0].astype(BF16), _pad_head_vec(mla_k_norm[0]), cos, sin)
    o_p = _mla_attn_prompt_call(q, k, v)
    o_s = _mla_attn_latent_call(q, k, v)
    x = _out_proj_call(o_p, o_s, mla_w_out[0].astype(BF16), xp0, xs0, True, mod)
    new_ckv = ckv[:NP].reshape(BATCH, 1, SEQ, MLA_KV_LORA)
    new_krope = kr[:NP, :MLA_ROPE].reshape(BATCH, 1, SEQ, MLA_ROPE)
    x, h = _moe(0, x, g2[0], mod, *routers[0], moe_w_gate, moe_w_up, moe_w_down,
                next_norm=(norm1_g[1].reshape(1, D_MODEL), mods[1]))

    mod = mods[1]
    lam_init = 0.8 - 0.6 * math.exp(-0.3 * 1)
    cos, sin = _rope_tables(DIFF_HEAD_DIM)
    q, k_p, k_s, v_p, v_s = _diff_qkv_call(h, diff_w_qkv[0].astype(BF16), diff_q_norm[0].reshape(1, -1),
                                           diff_k_norm[0].reshape(1, -1), cos, sin)
    lams = [t[0].reshape(1, DIFF_HEAD_DIM) for t in
            (diff_lambda_q1, diff_lambda_k1, diff_lambda_q2, diff_lambda_k2)]
    out_norm = diff_out_norm[0].reshape(1, DIFF_V_DIM)
    o_p = _diff_attn_prompt_call(q, k_p, v_p, lams, out_norm, lam_init)
    o_s = _diff_attn_latent_call(q, k_s, v_s, cache_diff_k[:, 0].reshape(NCTX, D_MODEL),
                                 cache_diff_v[:, 0].reshape(NCTX, D_MODEL), lams, out_norm, lam_init)
    x = _out_proj_call(o_p, o_s, diff_w_out[0].astype(BF16), x, x, False, mod)
    new_k = k_p.reshape(BATCH, 1, SEQ, DIFF_HEADS, 2, DIFF_HEAD_DIM)
    new_v = v_p.reshape(BATCH, 1, SEQ, DIFF_HEADS, DIFF_V_DIM)
    y_p, y_s = _moe(1, x, g2[1], mod, *routers[1], moe_w_gate, moe_w_up, moe_w_down, split_out=True)

    return (y_p.reshape(BATCH, SEQ, D_MODEL), y_s.reshape(DEC_BATCH, DEC_SEQ, D_MODEL),
            new_ckv, new_krope, new_k, new_v)
```

```python
import functools
import math

import jax
import jax.numpy as jnp
from jax import lax
from jax.experimental import pallas as pl
from jax.experimental.pallas import tpu as pltpu

F32 = jnp.float32
BF16 = jnp.bfloat16
I32 = jnp.int32
U32 = jnp.uint32

D_MODEL = 4096
BATCH = 32
SEQ = 256
DEPTH = 2
DEC_BATCH = 4
DEC_SEQ = 1024
PAST_LEN = 256
GRID_W = 64
MLA_HEADS = 32
MLA_NOPE = 128
MLA_ROPE = 64
MLA_QK = MLA_NOPE + MLA_ROPE
MLA_V = 128
MLA_Q_LORA = 1024
MLA_KV_LORA = 512
MLA_IN = MLA_Q_LORA + MLA_KV_LORA + MLA_ROPE
MLA_IN_PAD = MLA_Q_LORA + MLA_KV_LORA + 128
MLA_HEAD_PAD = 256
DIFF_HEAD_DIM = 128
DIFF_HEADS = 16
DIFF_V_DIM = 256
N_GROUPS = 8
EXPERTS_PER_GROUP = 8
N_EXPERTS = 64
TOP_K = 2
EXPERT_HIDDEN = 1024
ROPE_THETA = 10000.0
NORM_EPS = 1e-6

NP = BATCH * SEQ
NS = DEC_BATCH * DEC_SEQ
TT = NP + NS
NCTX = DEC_BATCH * PAST_LEN
N_COND = 8
N_ASSIGN = TT * TOP_K
ROUTER_LANES = 128
HALF_D = D_MODEL // 2
ROW_TILES = HALF_D // 128

MOE_ROWS = 512
PIECE_ROWS = 512
PIECE_COLS = EXPERT_HIDDEN
GU_CHUNKS = D_MODEL // PIECE_ROWS
DN_PHASES = D_MODEL // PIECE_COLS
N_PHASES = GU_CHUNKS + DN_PHASES
LOOKAHEAD = 4
RING_SLOTS = 2 * (LOOKAHEAD + 1)
OUT_COLS = 512
OUT_CHUNKS = D_MODEL // OUT_COLS
OUT_SLOTS = 4
GU_PHASE_ROWS = 48
DN_PHASE_ROWS = (MOE_ROWS - GU_CHUNKS * GU_PHASE_ROWS) // DN_PHASES
ROW_WAIT_GROUP = 64
N_BLOCKS = N_ASSIGN // MOE_ROWS + N_EXPERTS + 1
N_SLOTS = N_BLOCKS * MOE_ROWS

VMEM_LIMIT = 56 * 1024 * 1024


def _cparams(sem):
    return pltpu.CompilerParams(dimension_semantics=sem, vmem_limit_bytes=VMEM_LIMIT)


def _cond_row(row0):
    return jnp.where(row0 < NP, 0, 1 + (row0 - NP) // DEC_SEQ)


def _rms(x, g):
    return x * lax.rsqrt(jnp.mean(x * x, axis=-1, keepdims=True) + NORM_EPS) * g


def _modulate(x, g, shift, scale):
    return _rms(x, g) * (1.0 + scale) + shift


def _dot(a, b):
    return jnp.dot(a, b, preferred_element_type=F32)


def _dot_nt(a, b):
    return lax.dot_general(a, b, (((1,), (1,)), ((), ())), preferred_element_type=F32)


def _rope128(x, cos, sin_signed, half):
    lane = lax.broadcasted_iota(I32, x.shape, 1)
    first_half = ((lane // half) % 2) == 0
    partner = jnp.where(first_half, pltpu.roll(x, 128 - half, 1), pltpu.roll(x, half, 1))
    return x * cos + partner * sin_signed


def _latent_tables(cos_ref, sin_ref, is_latent):
    m = is_latent.astype(F32)
    return 1.0 + m * (cos_ref[...] - 1.0), m * sin_ref[...]


def _row_pair_specs(tm, tn, split):
    npt = NP // tm
    off = npt if split else 0
    last = D_MODEL // tn - 1

    def prompt_map(i, *a):
        return (jnp.minimum(i, npt - 1), jnp.where(i < npt, a[0], last) if last else 0)

    def latent_map(i, *a):
        return (jnp.maximum(i, npt) - off, jnp.where(i >= npt, a[0], 0) if last else 0)

    return pl.BlockSpec((tm, tn), prompt_map), pl.BlockSpec((tm, tn), latent_map)


def _ada_kernel(c_ref, w_ref, b_ref, o_ref):
    c = c_ref[...]
    a = (c * jax.nn.sigmoid(c)).astype(BF16)
    o_ref[...] = _dot(a, w_ref[...].astype(BF16)) + b_ref[...]


def _ada_call(cond, ada_w, ada_b):
    tn = 512
    n = 6 * D_MODEL
    return pl.pallas_call(
        _ada_kernel,
        grid=(DEPTH, n // tn),
        in_specs=[
            pl.BlockSpec((N_COND, D_MODEL), lambda l, j: (0, 0)),
            pl.BlockSpec((None, D_MODEL, tn), lambda l, j: (l, 0, j)),
            pl.BlockSpec((None, 1, tn), lambda l, j: (l, 0, j)),
        ],
        out_specs=pl.BlockSpec((None, N_COND, tn), lambda l, j: (l, 0, j)),
        out_shape=jax.ShapeDtypeStruct((DEPTH, N_COND, n), F32),
        compiler_params=_cparams(("arbitrary", "arbitrary")),
        name="ada_table",
    )(cond, ada_w, ada_b.reshape(DEPTH, 1, n))


def _mod_spec(tm, chunk, width=D_MODEL):
    per = D_MODEL // width
    return pl.BlockSpec((None, 1, width),
                        lambda i, *a: (_cond_row(i * tm), 0, chunk * per + (a[0] if per > 1 else 0)))


def _mla_in_kernel(xp_ref, xs_ref, g_ref, sh_ref, sc_ref, w_ref, qg_ref, kvg_ref, cq_ref, ckv_ref, kr_ref, *, tm):
    def run(x_ref):
        h = _modulate(x_ref[...], g_ref[...], sh_ref[...], sc_ref[...]).astype(BF16)
        p = _dot(h, w_ref[...])
        cq_ref[...] = _rms(p[:, :MLA_Q_LORA], qg_ref[...]).astype(BF16)
        ckv_ref[...] = _rms(p[:, MLA_Q_LORA:MLA_Q_LORA + MLA_KV_LORA], kvg_ref[...])
        kr_ref[...] = p[:, MLA_Q_LORA + MLA_KV_LORA:]

    is_prompt = pl.program_id(0) < NP // tm
    pl.when(is_prompt)(lambda: run(xp_ref))
    pl.when(jnp.logical_not(is_prompt))(lambda: run(xs_ref))


def _mla_in_call(x_prompt, x_latent, g, mod, w_in, q_a_norm, kv_a_norm):
    tm = 256
    xp_spec, xs_spec = _row_pair_specs(tm, D_MODEL, split=True)
    return pl.pallas_call(
        functools.partial(_mla_in_kernel, tm=tm),
        grid=(TT // tm,),
        in_specs=[
            xp_spec,
            xs_spec,
            pl.BlockSpec((1, D_MODEL), lambda i: (0, 0)),
            _mod_spec(tm, 0),
            _mod_spec(tm, 1),
            pl.BlockSpec((D_MODEL, MLA_IN_PAD), lambda i: (0, 0)),
            pl.BlockSpec((1, MLA_Q_LORA), lambda i: (0, 0)),
            pl.BlockSpec((1, MLA_KV_LORA), lambda i: (0, 0)),
        ],
        out_specs=[
            pl.BlockSpec((tm, MLA_Q_LORA), lambda i: (i, 0)),
            pl.BlockSpec((tm, MLA_KV_LORA), lambda i: (i, 0)),
            pl.BlockSpec((tm, 128), lambda i: (i, 0)),
        ],
        out_shape=[
            jax.ShapeDtypeStruct((TT, MLA_Q_LORA), BF16),
            jax.ShapeDtypeStruct((TT, MLA_KV_LORA), F32),
            jax.ShapeDtypeStruct((TT, 128), F32),
        ],
        compiler_params=_cparams(("arbitrary",)),
        name="mla_in",
    )(x_prompt, x_latent, g, mod, mod, w_in, q_a_norm, kv_a_norm)


def _q_up_kernel(c_ref, w_ref, g_ref, cos_ref, sin_ref, q_ref, *, tm, heads):
    i = pl.program_id(0)
    cos, sin = _latent_tables(cos_ref, sin_ref, i >= NP // tm)
    o = _dot(c_ref[...], w_ref[...])
    g = g_ref[...] * (MLA_QK ** -0.5)
    for hh in range(heads):
        s = o[:, hh * 256:(hh + 1) * 256]
        r = lax.rsqrt(jnp.sum(s * s, axis=-1, keepdims=True) / MLA_QK + NORM_EPS)
        lo = s[:, :128] * r * g[:, :128]
        hi = _rope128(s[:, 128:] * r * g[:, 128:], cos, sin, MLA_ROPE // 4)
        q_ref[:, hh * 256:hh * 256 + 128] = lo.astype(BF16)
        q_ref[:, hh * 256 + 128:(hh + 1) * 256] = hi.astype(BF16)


def _pos_block(i, tm):
    return jnp.clip(i - NP // tm, 0, NS // tm - 1) % (DEC_SEQ // tm)


def _q_up_call(cq, w_q, q_norm_pad, cos, sin):
    tm, heads = 512, 8
    tn = heads * MLA_HEAD_PAD
    return pl.pallas_call(
        functools.partial(_q_up_kernel, tm=tm, heads=heads),
        grid=(TT // tm, MLA_HEADS // heads),
        in_specs=[
            pl.BlockSpec((tm, MLA_Q_LORA), lambda i, j: (i, 0)),
            pl.BlockSpec((MLA_Q_LORA, tn), lambda i, j: (0, j)),
            pl.BlockSpec((1, MLA_HEAD_PAD), lambda i, j: (0, 0)),
            pl.BlockSpec((tm, 128), lambda i, j: (_pos_block(i, tm), 0)),
            pl.BlockSpec((tm, 128), lambda i, j: (_pos_block(i, tm), 0)),
        ],
        out_specs=pl.BlockSpec((tm, tn), lambda i, j: (i, j)),
        out_shape=jax.ShapeDtypeStruct((TT, MLA_HEADS * MLA_HEAD_PAD), BF16),
        compiler_params=_cparams(("arbitrary", "arbitrary")),
        name="mla_q_up",
    )(cq, w_q, q_norm_pad, cos, sin)


def _kv_up_kernel(c_ref, kr_ref, w_ref, g_ref, cos_ref, sin_ref, k_ref, v_ref, *, tm, heads):
    i = pl.program_id(0)
    cos, sin = _latent_tables(cos_ref, sin_ref, (i >= NP // tm) & (i < TT // tm))
    kv = _dot(c_ref[...].astype(BF16), w_ref[...])
    g = g_ref[...]
    kr = kr_ref[...]
    ss_rope = jnp.sum(kr * kr, axis=-1, keepdims=True)
    base = _rope128(kr * g[:, 128:], cos, sin, MLA_ROPE // 4)
    for hh in range(heads):
        kn = kv[:, hh * 256:hh * 256 + 128]
        r = lax.rsqrt((jnp.sum(kn * kn, axis=-1, keepdims=True) + ss_rope) / MLA_QK + NORM_EPS)
        k_ref[:, hh * 256:hh * 256 + 128] = (kn * r * g[:, :128]).astype(BF16)
        k_ref[:, hh * 256 + 128:(hh + 1) * 256] = (base * r).astype(BF16)
        v_ref[:, hh * 128:(hh + 1) * 128] = kv[:, hh * 256 + 128:(hh + 1) * 256].astype(BF16)


def _kv_up_call(ckv_all, kr_all, w_kv, k_norm_pad, cos, sin):
    tm, heads = 512, 8
    rows = TT + NCTX
    return pl.pallas_call(
        functools.partial(_kv_up_kernel, tm=tm, heads=heads),
        grid=(rows // tm, MLA_HEADS // heads),
        in_specs=[
            pl.BlockSpec((tm, MLA_KV_LORA), lambda i, j: (i, 0)),
            pl.BlockSpec((tm, 128), lambda i, j: (i, 0)),
            pl.BlockSpec((MLA_KV_LORA, heads * 256), lambda i, j: (0, j)),
            pl.BlockSpec((1, MLA_HEAD_PAD), lambda i, j: (0, 0)),
            pl.BlockSpec((tm, 128), lambda i, j: (_pos_block(i, tm), 0)),
            pl.BlockSpec((tm, 128), lambda i, j: (_pos_block(i, tm), 0)),
        ],
        out_specs=[
            pl.BlockSpec((tm, heads * MLA_HEAD_PAD), lambda i, j: (i, j)),
            pl.BlockSpec((tm, heads * MLA_V), lambda i, j: (i, j)),
        ],
        out_shape=[
            jax.ShapeDtypeStruct((rows, MLA_HEADS * MLA_HEAD_PAD), BF16),
            jax.ShapeDtypeStruct((rows, MLA_HEADS * MLA_V), BF16),
        ],
        compiler_params=_cparams(("arbitrary", "arbitrary")),
        name="mla_kv_up",
    )(ckv_all, kr_all, w_kv, k_norm_pad, cos, sin)


def _mla_attn_prompt_kernel(q_ref, k_ref, v_ref, o_ref, *, heads):
    for hh in range(heads):
        q = q_ref[:, hh * 256:(hh + 1) * 256]
        k = k_ref[:, hh * 256:(hh + 1) * 256]
        s = _dot_nt(q, k)
        p = jnp.exp(s - jnp.max(s, axis=-1, keepdims=True))
        l = jnp.sum(p, axis=-1, keepdims=True)
        o = _dot(p.astype(BF16), v_ref[:, hh * 128:(hh + 1) * 128]) / l
        o_ref[:, hh * 128:(hh + 1) * 128] = o.astype(BF16)


def _mla_attn_prompt_call(q, k, v):
    heads = 8
    return pl.pallas_call(
        functools.partial(_mla_attn_prompt_kernel, heads=heads),
        grid=(BATCH, MLA_HEADS // heads),
        in_specs=[
            pl.BlockSpec((SEQ, heads * 256), lambda b, j: (b, j)),
            pl.BlockSpec((SEQ, heads * 256), lambda b, j: (b, j)),
            pl.BlockSpec((SEQ, heads * 128), lambda b, j: (b, j)),
        ],
        out_specs=pl.BlockSpec((SEQ, heads * 128), lambda b, j: (b, j)),
        out_shape=jax.ShapeDtypeStruct((NP, MLA_HEADS * MLA_V), BF16),
        compiler_params=_cparams(("arbitrary", "arbitrary")),
        name="mla_attn_prompt",
    )(q, k, v)


def _mla_attn_latent_kernel(q_ref, kl_ref, kc_ref, vl_ref, vc_ref, o_ref, *, heads):
    for hh in range(heads):
        q = q_ref[:, hh * 256:(hh + 1) * 256]
        sl = _dot_nt(q, kl_ref[:, hh * 256:(hh + 1) * 256])
        sc = _dot_nt(q, kc_ref[:, hh * 256:(hh + 1) * 256])
        m = jnp.maximum(jnp.max(sl, axis=-1, keepdims=True), jnp.max(sc, axis=-1, keepdims=True))
        pl_ = jnp.exp(sl - m)
        pc = jnp.exp(sc - m)
        l = jnp.sum(pl_, axis=-1, keepdims=True) + jnp.sum(pc, axis=-1, keepdims=True)
        o = _dot(pl_.astype(BF16), vl_ref[:, hh * 128:(hh + 1) * 128])
        o = o + _dot(pc.astype(BF16), vc_ref[:, hh * 128:(hh + 1) * 128])
        o_ref[:, hh * 128:(hh + 1) * 128] = (o / l).astype(BF16)


def _mla_attn_latent_call(q, k, v):
    heads, tq = 8, 512
    nq = DEC_SEQ // tq
    return pl.pallas_call(
        functools.partial(_mla_attn_latent_kernel, heads=heads),
        grid=(DEC_BATCH, MLA_HEADS // heads, nq),
        in_specs=[
            pl.BlockSpec((tq, heads * 256), lambda b, j, t: (NP // tq + b * nq + t, j)),
            pl.BlockSpec((DEC_SEQ, heads * 256), lambda b, j, t: (NP // DEC_SEQ + b, j)),
            pl.BlockSpec((PAST_LEN, heads * 256), lambda b, j, t: (TT // PAST_LEN + b, j)),
            pl.BlockSpec((DEC_SEQ, heads * 128), lambda b, j, t: (NP // DEC_SEQ + b, j)),
            pl.BlockSpec((PAST_LEN, heads * 128), lambda b, j, t: (TT // PAST_LEN + b, j)),
        ],
        out_specs=pl.BlockSpec((tq, heads * 128), lambda b, j, t: (b * nq + t, j)),
        out_shape=jax.ShapeDtypeStruct((NS, MLA_HEADS * MLA_V), BF16),
        compiler_params=_cparams(("arbitrary", "arbitrary", "arbitrary")),
        name="mla_attn_latent",
    )(q, k, k, v, v)


def _out_proj_kernel(ap_ref, as_ref, w_ref, xp_ref, xs_ref, gate_ref, o_ref, *, tm):
    def finish(a_ref, x_ref):
        o_ref[...] = x_ref[...] + gate_ref[...] * _dot(a_ref[...], w_ref[...])

    is_prompt = pl.program_id(0) < NP // tm
    pl.when(is_prompt)(lambda: finish(ap_ref, xp_ref))
    pl.when(jnp.logical_not(is_prompt))(lambda: finish(as_ref, xs_ref))


def _out_proj_call(a_prompt, a_latent, w, x_prompt, x_latent, x_split, mod):
    tm, tn = 512, 1024
    ap_spec, as_spec = _row_pair_specs(tm, D_MODEL, split=True)
    xp_spec, xs_spec = _row_pair_specs(tm, tn, split=x_split)
    return pl.pallas_call(
        functools.partial(_out_proj_kernel, tm=tm),
        grid=(TT // tm, D_MODEL // tn),
        in_specs=[
            ap_spec,
            as_spec,
            pl.BlockSpec((D_MODEL, tn), lambda i, j: (0, j)),
            xp_spec,
            xs_spec,
            _mod_spec(tm, 2, width=tn),
        ],
        out_specs=pl.BlockSpec((tm, tn), lambda i, j: (i, j)),
        out_shape=jax.ShapeDtypeStruct((TT, D_MODEL), F32),
        compiler_params=_cparams(("arbitrary", "arbitrary")),
        name="attn_out_proj",
    )(a_prompt, a_latent, w, x_prompt, x_latent, mod)


def _diff_qkv_kernel(h_ref, w_ref, qn_ref, kn_ref, cos_ref, sin_ref, q_ref, kp_ref, kl_ref, vp_ref, vl_ref,
                     *, tm, nq, sub):
    i = pl.program_id(0)
    j = pl.program_id(1)
    is_prompt = i < NP // tm
    cos, sin = _latent_tables(cos_ref, sin_ref, jnp.logical_not(is_prompt))
    n_sub = w_ref.shape[1] // sub

    def store_rows(prompt_ref, latent_ref, pieces):
        def fill(ref):
            for c0, y in pieces:
                ref[:, c0:c0 + y.shape[1]] = y
        pl.when(is_prompt)(lambda: fill(prompt_ref))
        pl.when(jnp.logical_not(is_prompt))(lambda: fill(latent_ref))

    def norm_rope(o, g, scale):
        return [_rope128(_rms(o[:, s * 128:(s + 1) * 128], g) * scale, cos, sin, DIFF_HEAD_DIM // 4)
                for s in range(sub // 128)]

    @pl.when(j < nq)
    def _():
        for t in range(n_sub):
            o = _dot(h_ref[...], w_ref[:, t * sub:(t + 1) * sub])
            for s, y in enumerate(norm_rope(o, qn_ref[...], DIFF_HEAD_DIM ** -0.5)):
                q_ref[:, t * sub + s * 128:t * sub + (s + 1) * 128] = y.astype(BF16)

    @pl.when((j >= nq) & (j < 2 * nq))
    def _():
        pieces = []
        for t in range(n_sub):
            o = _dot(h_ref[...], w_ref[:, t * sub:(t + 1) * sub])
            pieces += [(t * sub + s * 128, y) for s, y in enumerate(norm_rope(o, kn_ref[...], 1.0))]
        store_rows(kp_ref, kl_ref, pieces)

    @pl.when(j >= 2 * nq)
    def _():
        store_rows(vp_ref, vl_ref, [(0, _dot(h_ref[...], w_ref[...]))])


def _diff_qkv_call(h, w_qkv, q_norm, k_norm, cos, sin):
    tm, tn, sub = 512, 1024, 256
    nq = D_MODEL // tn
    npt = NP // tm

    def kv_specs(first):
        col = lambda j: jnp.clip(j - first, 0, nq - 1)
        return [pl.BlockSpec((tm, tn), lambda i, j: (jnp.minimum(i, npt - 1), jnp.where(i < npt, col(j), nq - 1))),
                pl.BlockSpec((tm, tn), lambda i, j: (jnp.maximum(i - npt, 0), jnp.where(i >= npt, col(j), 0)))]

    kv_shapes = [jax.ShapeDtypeStruct((NP, D_MODEL), F32), jax.ShapeDtypeStruct((NS, D_MODEL), F32)]
    return pl.pallas_call(
        functools.partial(_diff_qkv_kernel, tm=tm, nq=nq, sub=sub),
        grid=(TT // tm, 3 * nq),
        in_specs=[
            pl.BlockSpec((tm, D_MODEL), lambda i, j: (i, 0)),
            pl.BlockSpec((D_MODEL, tn), lambda i, j: (0, j)),
            pl.BlockSpec((1, DIFF_HEAD_DIM), lambda i, j: (0, 0)),
            pl.BlockSpec((1, DIFF_HEAD_DIM), lambda i, j: (0, 0)),
            pl.BlockSpec((tm, 128), lambda i, j: (_pos_block(i, tm), 0)),
            pl.BlockSpec((tm, 128), lambda i, j: (_pos_block(i, tm), 0)),
        ],
        out_specs=[pl.BlockSpec((tm, tn), lambda i, j: (i, jnp.minimum(j, nq - 1)))]
        + kv_specs(nq) + kv_specs(2 * nq),
        out_shape=[jax.ShapeDtypeStruct((TT, D_MODEL), BF16)] + kv_shapes + kv_shapes,
        compiler_params=_cparams(("arbitrary", "arbitrary")),
        name="diff_qkv",
    )(h, w_qkv, q_norm, k_norm, cos, sin)


def _diff_lambda(lq1_ref, lk1_ref, lq2_ref, lk2_ref, lam_init):
    e1 = jnp.exp(jnp.sum(lq1_ref[...] * lk1_ref[...], axis=-1, keepdims=True))
    e2 = jnp.exp(jnp.sum(lq2_ref[...] * lk2_ref[...], axis=-1, keepdims=True))
    return e1 - e2 + lam_init


def _softmax_rows(s):
    p = jnp.exp(s - jnp.max(s, axis=-1, keepdims=True))
    return p / jnp.sum(p, axis=-1, keepdims=True)


def _diff_attn_prompt_kernel(q_ref, k_ref, v_ref, lq1, lk1, lq2, lk2, on_ref, o_ref, *, heads, lam_init):
    lam = _diff_lambda(lq1, lk1, lq2, lk2, lam_init)
    for hh in range(heads):
        c0 = hh * 256
        p1 = _softmax_rows(_dot_nt(q_ref[:, c0:c0 + 128], k_ref[:, c0:c0 + 128].astype(BF16)))
        p2 = _softmax_rows(_dot_nt(q_ref[:, c0 + 128:c0 + 256], k_ref[:, c0 + 128:c0 + 256].astype(BF16)))
        a = (p1 - lam * p2).astype(BF16)
        o = _dot(a, v_ref[:, c0:c0 + 256].astype(BF16))
        o_ref[:, c0:c0 + 256] = (_rms(o, on_ref[...]) * (1.0 - lam_init)).astype(BF16)


def _vec_spec(n):
    return pl.BlockSpec((1, n), lambda *a: (0, 0))


def _diff_attn_prompt_call(q, k, v, lams, out_norm, lam_init):
    heads = 4
    w = heads * 256
    blk = pl.BlockSpec((SEQ, w), lambda b, j: (b, j))
    return pl.pallas_call(
        functools.partial(_diff_attn_prompt_kernel, heads=heads, lam_init=lam_init),
        grid=(BATCH, DIFF_HEADS // heads),
        in_specs=[blk, blk, blk] + [_vec_spec(DIFF_HEAD_DIM)] * 4 + [_vec_spec(DIFF_V_DIM)],
        out_specs=blk,
        out_shape=jax.ShapeDtypeStruct((NP, D_MODEL), BF16),
        compiler_params=_cparams(("arbitrary", "arbitrary")),
        name="diff_attn_prompt",
    )(q, k, v, *lams, out_norm)


def _softmax2(sa, sb):
    m = jnp.maximum(jnp.max(sa, axis=-1, keepdims=True), jnp.max(sb, axis=-1, keepdims=True))
    pa = jnp.exp(sa - m)
    pb = jnp.exp(sb - m)
    l = jnp.sum(pa, axis=-1, keepdims=True) + jnp.sum(pb, axis=-1, keepdims=True)
    return pa / l, pb / l


def _diff_attn_latent_kernel(q_ref, kl_ref, kc_ref, vl_ref, vc_ref, lq1, lk1, lq2, lk2, on_ref,
                             o_ref, *, heads, lam_init):
    lam = _diff_lambda(lq1, lk1, lq2, lk2, lam_init)
    for hh in range(heads):
        c0 = hh * 256
        q1 = q_ref[:, c0:c0 + 128]
        q2 = q_ref[:, c0 + 128:c0 + 256]
        p1l, p1c = _softmax2(_dot_nt(q1, kl_ref[:, c0:c0 + 128].astype(BF16)),
                             _dot_nt(q1, kc_ref[:, c0:c0 + 128].astype(BF16)))
        p2l, p2c = _softmax2(_dot_nt(q2, kl_ref[:, c0 + 128:c0 + 256].astype(BF16)),
                             _dot_nt(q2, kc_ref[:, c0 + 128:c0 + 256].astype(BF16)))
        o = _dot((p1l - lam * p2l).astype(BF16), vl_ref[:, c0:c0 + 256].astype(BF16))
        o = o + _dot((p1c - lam * p2c).astype(BF16), vc_ref[:, c0:c0 + 256].astype(BF16))
        o_ref[:, c0:c0 + 256] = (_rms(o, on_ref[...]) * (1.0 - lam_init)).astype(BF16)


def _diff_attn_latent_call(q, k, v, k_ctx, v_ctx, lams, out_norm, lam_init):
    heads, tq = 4, 256
    w = heads * 256
    nq = DEC_SEQ // tq
    lat = pl.BlockSpec((DEC_SEQ, w), lambda b, j, t: (b, j))
    ctx = pl.BlockSpec((PAST_LEN, w), lambda b, j, t: (b, j))
    return pl.pallas_call(
        functools.partial(_diff_attn_latent_kernel, heads=heads, lam_init=lam_init),
        grid=(DEC_BATCH, DIFF_HEADS // heads, nq),
        in_specs=[pl.BlockSpec((tq, w), lambda b, j, t: (NP // tq + b * nq + t, j)), lat, ctx, lat, ctx]
        + [_vec_spec(DIFF_HEAD_DIM)] * 4 + [_vec_spec(DIFF_V_DIM)],
        out_specs=pl.BlockSpec((tq, w), lambda b, j, t: (b * nq + t, j)),
        out_shape=jax.ShapeDtypeStruct((NS, D_MODEL), BF16),
        compiler_params=_cparams(("arbitrary", "arbitrary", "arbitrary")),
        name="diff_attn_latent",
    )(q, k, k_ctx, v, v_ctx, *lams, out_norm)


def _router_kernel(x_ref, g_ref, sh_ref, sc_ref, w_ref, b_ref, h_ref, ids_ref, gates_ref):
    h = _modulate(x_ref[...], g_ref[...], sh_ref[...], sc_ref[...])
    h_hi = h.astype(BF16)
    h_hi32 = h_hi.astype(F32)
    bits = lax.bitcast_convert_type(h_hi32, U32)
    h_ref[...] = (bits[:, :HALF_D] >> 16) | (bits[:, HALF_D:] & jnp.uint32(0xFFFF0000))

    h_lo = (h - h_hi32).astype(BF16)
    two = _dot(h_hi, w_ref[...])
    logits = two[:, :ROUTER_LANES] + two[:, ROUTER_LANES:] + _dot(h_lo, w_ref[:, :ROUTER_LANES]) + b_ref[...]

    lane = lax.broadcasted_iota(I32, logits.shape, 1).astype(F32)
    neg = -jnp.inf
    gl = jnp.where(lane < N_GROUPS, logits, neg)
    gmax = jnp.max(gl, axis=-1, keepdims=True)
    g_sel = jnp.min(jnp.where(gl == gmax, lane, ROUTER_LANES), axis=-1, keepdims=True)
    p_group = 1.0 / jnp.sum(jnp.exp(gl - gmax), axis=-1, keepdims=True)

    lo = N_GROUPS + g_sel * EXPERTS_PER_GROUP
    el = jnp.where(lane >= lo, jnp.where(lane < lo + EXPERTS_PER_GROUP, logits, neg), neg)
    v1 = jnp.max(el, axis=-1, keepdims=True)
    i1 = jnp.min(jnp.where(el == v1, lane, ROUTER_LANES), axis=-1, keepdims=True)
    el2 = jnp.where(lane == i1, neg, el)
    v2 = jnp.max(el2, axis=-1, keepdims=True)
    i2 = jnp.min(jnp.where(el2 == v2, lane, ROUTER_LANES), axis=-1, keepdims=True)
    e2 = jnp.exp(v2 - v1)
    w1 = p_group * (1.0 / (1.0 + e2))
    w2 = p_group * (e2 / (1.0 + e2))
    ids = jnp.where(lane == 0, i1 - N_GROUPS, jnp.where(lane == 1, i2 - N_GROUPS, 0.0))
    ids_ref[...] = ids.astype(I32)
    gates_ref[...] = jnp.where(lane == 0, w1, jnp.where(lane == 1, w2, 0.0))


def _router_call(x, g, mod, w_router2, b_router):
    tm = 256
    return pl.pallas_call(
        _router_kernel,
        grid=(TT // tm,),
        in_specs=[
            pl.BlockSpec((tm, D_MODEL), lambda i: (i, 0)),
            pl.BlockSpec((1, D_MODEL), lambda i: (0, 0)),
            _mod_spec(tm, 3),
            _mod_spec(tm, 4),
            pl.BlockSpec((D_MODEL, 2 * ROUTER_LANES), lambda i: (0, 0)),
            pl.BlockSpec((1, ROUTER_LANES), lambda i: (0, 0)),
        ],
        out_specs=[
            pl.BlockSpec((tm, HALF_D), lambda i: (i, 0)),
            pl.BlockSpec((tm, ROUTER_LANES), lambda i: (i, 0)),
            pl.BlockSpec((tm, ROUTER_LANES), lambda i: (i, 0)),
        ],
        out_shape=[
            jax.ShapeDtypeStruct((TT, HALF_D), U32),
            jax.ShapeDtypeStruct((TT, ROUTER_LANES), I32),
            jax.ShapeDtypeStruct((TT, ROUTER_LANES), F32),
        ],
        compiler_params=_cparams(("arbitrary",)),
        name="moe_router",
    )(x, g, mod, mod, w_router2, b_router)


def _moe_ffn_kernel(tok_ref, be_ref, nb_ref, h_hbm, wg_hbm, wu_hbm, wd_hbm, ys_hbm,
                    xrows, x_lo, x_hi, ring, gate_acc, up_acc, hid, stage,
                    row_sem, ring_sem, out_sem, *, layer):
    b = pl.program_id(0)
    nb = nb_ref[0]

    def row_copy(blk, r):
        return pltpu.make_async_copy(h_hbm.at[tok_ref[blk * MOE_ROWS + r]],
                                     xrows.at[pl.ds(r * ROW_TILES, ROW_TILES), :], row_sem)

    def piece_slot(blk, t, k):
        return (blk * (2 * N_PHASES) + 2 * t + k) % RING_SLOTS

    def gu_chunk(t, k):
        return (t + k * (GU_CHUNKS // 2)) % GU_CHUNKS

    def x_chunk(chunk):
        x_half = x_lo if chunk * PIECE_ROWS < HALF_D else x_hi
        k0 = (chunk * PIECE_ROWS) % HALF_D
        return x_half[:, k0:k0 + PIECE_ROWS]

    def piece_copy(blk, t, k):
        e, s = be_ref[blk], piece_slot(blk, t, k)
        if t < GU_CHUNKS:
            src = (wg_hbm, wu_hbm)[k].at[layer, e, pl.ds(gu_chunk(t, k) * PIECE_ROWS, PIECE_ROWS), :]
        else:
            src = wd_hbm.at[layer, e, pl.ds(k * PIECE_ROWS, PIECE_ROWS),
                            pl.ds((t - GU_CHUNKS) * PIECE_COLS, PIECE_COLS)]
        return pltpu.make_async_copy(src, ring.at[s], ring_sem.at[s])

    def request_phase(blk, t):
        for k in range(2):
            piece_copy(blk, t, k).start()

    def await_phase(blk, t):
        for k in range(2):
            piece_copy(blk, t, k).wait()

    def out_copy(blk, c, src_slot):
        return pltpu.make_async_copy(
            stage.at[src_slot], ys_hbm.at[pl.ds(blk * MOE_ROWS, MOE_ROWS), pl.ds(c * OUT_COLS, OUT_COLS)],
            out_sem.at[src_slot])

    def request_next_rows(first, count):
        for r in range(count):
            row_copy(b + 1, first + r).start()

    def request_ahead(t):
        ahead = t + LOOKAHEAD
        if ahead < N_PHASES:
            request_phase(b, ahead)
        else:
            request_phase(b + 1, ahead - N_PHASES)

    @pl.when(b == 0)
    def _():
        for t in range(LOOKAHEAD):
            request_phase(0, t)

        def start(r, carry):
            row_copy(0, r).start()
            return carry
        lax.fori_loop(0, MOE_ROWS, start, 0)

    @pl.when(b <= nb)
    def _():
        def wait(g, carry):
            for r in range(ROW_WAIT_GROUP):
                row_copy(b, g * ROW_WAIT_GROUP + r).wait()
            return carry
        lax.fori_loop(0, MOE_ROWS // ROW_WAIT_GROUP, wait, 0)

    @pl.when(b < nb)
    def _():
        for q in range(ROW_TILES):
            words = xrows[pl.ds(q, MOE_ROWS, stride=ROW_TILES), :]
            x_lo[:, q * 128:(q + 1) * 128] = lax.bitcast_convert_type(words << 16, F32).astype(BF16)
            x_hi[:, q * 128:(q + 1) * 128] = lax.bitcast_convert_type(
                words & jnp.uint32(0xFFFF0000), F32).astype(BF16)

        for t in range(GU_CHUNKS):
            request_ahead(t)
            request_next_rows(t * GU_PHASE_ROWS, GU_PHASE_ROWS)
            await_phase(b, t)
            gate = _dot(x_chunk(gu_chunk(t, 0)), ring[piece_slot(b, t, 0)].astype(BF16))
            up = _dot(x_chunk(gu_chunk(t, 1)), ring[piece_slot(b, t, 1)].astype(BF16))
            if t > 0:
                gate = gate_acc[...] + gate
                up = up_acc[...] + up
            if t + 1 < GU_CHUNKS:
                gate_acc[...] = gate
                up_acc[...] = up
            else:
                hid[...] = (gate * jax.nn.sigmoid(gate) * up).astype(BF16)

        for n in range(DN_PHASES):
            t = GU_CHUNKS + n
            request_ahead(t)
            request_next_rows(GU_CHUNKS * GU_PHASE_ROWS + n * DN_PHASE_ROWS, DN_PHASE_ROWS)
            await_phase(b, t)
            acc = (_dot(hid[:, :PIECE_ROWS], ring[piece_slot(b, t, 0)].astype(BF16))
                   + _dot(hid[:, PIECE_ROWS:], ring[piece_slot(b, t, 1)].astype(BF16)))
            for half in range(PIECE_COLS // OUT_COLS):
                c = n * (PIECE_COLS // OUT_COLS) + half
                slot = c % OUT_SLOTS
                if c >= OUT_SLOTS:
                    out_copy(b, c - OUT_SLOTS, slot).wait()
                else:
                    prev = OUT_CHUNKS - OUT_SLOTS + c
                    pl.when(b > 0)(lambda prev=prev, slot=slot: out_copy(b - 1, prev, slot).wait())
                stage[slot] = acc[:, half * OUT_COLS:(half + 1) * OUT_COLS]
                out_copy(b, c, slot).start()

    @pl.when(b >= nb)
    def _():
        @pl.when(b == nb)
        def _():
            for c in range(OUT_CHUNKS - OUT_SLOTS, OUT_CHUNKS):
                out_copy(b - 1, c, c % OUT_SLOTS).wait()
            for t in range(LOOKAHEAD):
                await_phase(b, t)

        stage[0] = jnp.zeros(stage.shape[1:], F32)
        for c in range(OUT_CHUNKS):
            out_copy(b, c, 0).start()
        for c in range(OUT_CHUNKS):
            out_copy(b, c, 0).wait()


def _moe_ffn_call(layer, slot_tok, block_e, n_blocks, h_packed, w_gate, w_up, w_down):
    any_spec = pl.BlockSpec(memory_space=pl.ANY)
    return pl.pallas_call(
        functools.partial(_moe_ffn_kernel, layer=layer),
        grid_spec=pltpu.PrefetchScalarGridSpec(
            num_scalar_prefetch=3,
            grid=(N_BLOCKS,),
            in_specs=[any_spec] * 4,
            out_specs=any_spec,
            scratch_shapes=[
                pltpu.VMEM((MOE_ROWS * ROW_TILES, 128), U32),
                pltpu.VMEM((MOE_ROWS, HALF_D), BF16),
                pltpu.VMEM((MOE_ROWS, HALF_D), BF16),
                pltpu.VMEM((RING_SLOTS, PIECE_ROWS, PIECE_COLS), F32),
                pltpu.VMEM((MOE_ROWS, EXPERT_HIDDEN), F32),
                pltpu.VMEM((MOE_ROWS, EXPERT_HIDDEN), F32),
                pltpu.VMEM((MOE_ROWS, EXPERT_HIDDEN), BF16),
                pltpu.VMEM((OUT_SLOTS, MOE_ROWS, OUT_COLS), F32),
                pltpu.SemaphoreType.DMA(()),
                pltpu.SemaphoreType.DMA((RING_SLOTS,)),
                pltpu.SemaphoreType.DMA((OUT_SLOTS,)),
            ],
        ),
        out_shape=jax.ShapeDtypeStruct((N_SLOTS, D_MODEL), F32),
        compiler_params=_cparams(("arbitrary",)),
        name="moe_experts",
    )(slot_tok, block_e, n_blocks, h_packed.reshape(TT, ROW_TILES, 128), w_gate, w_up, w_down)


def _combine_kernel(pos_ref, ys_hbm, x_ref, gates_ref, gate2_ref, *rest, tm, unroll, emit_next, split_out):
    if emit_next:
        ng_ref, nsh_ref, nsc_ref = rest[:3]
        rest = rest[3:]
    outs, (buf, sem) = rest[:-2], rest[-2:]
    i = pl.program_id(0)
    slot = i % 2

    def copies(tile, r, s):
        base = tile * (tm * TOP_K)
        return [pltpu.make_async_copy(ys_hbm.at[pl.ds(pos_ref[base + TOP_K * r + k], 1), :],
                                      buf.at[s, k, pl.ds(r, 1), :], sem.at[s]) for k in range(TOP_K)]

    def start_tile(tile, s):
        def body(t, c):
            for u in range(unroll):
                for cp in copies(tile, t * unroll + u, s):
                    cp.start()
            return c
        lax.fori_loop(0, tm // unroll, body, 0)

    def wait_tile(tile, s):
        def body(t, c):
            for u in range(unroll):
                for cp in copies(tile, t * unroll + u, s):
                    cp.wait()
            return c
        lax.fori_loop(0, tm // unroll, body, 0)

    pl.when(i == 0)(lambda: start_tile(0, 0))
    pl.when(i + 1 < pl.num_programs(0))(lambda: start_tile(i + 1, 1 - slot))
    wait_tile(i, slot)
    gates = gates_ref[...]
    y = gates[:, 0:1] * buf[slot, 0] + gates[:, 1:2] * buf[slot, 1]
    x = x_ref[...] + gate2_ref[...] * y
    if split_out:
        is_prompt = i < NP // tm

        @pl.when(is_prompt)
        def _():
            outs[0][...] = x

        @pl.when(jnp.logical_not(is_prompt))
        def _():
            outs[1][...] = x
    else:
        outs[0][...] = x
    if emit_next:
        outs[-1][...] = _modulate(x, ng_ref[...], nsh_ref[...], nsc_ref[...]).astype(BF16)


def _combine_call(pos, ys, x, gates, mod, next_norm=None, split_out=False):
    tm, unroll = 256, 8
    npt = NP // tm
    row = lambda i, pos: (i, 0)
    in_specs = [
        pl.BlockSpec(memory_space=pl.ANY),
        pl.BlockSpec((tm, D_MODEL), row),
        pl.BlockSpec((tm, ROUTER_LANES), row),
        pl.BlockSpec((None, 1, D_MODEL), lambda i, pos: (_cond_row(i * tm), 0, 5)),
    ]
    args = [ys, x, gates, mod]
    if next_norm is not None:
        g_next, mod_next = next_norm
        in_specs += [
            pl.BlockSpec((1, D_MODEL), lambda i, pos: (0, 0)),
            pl.BlockSpec((None, 1, D_MODEL), lambda i, pos: (_cond_row(i * tm), 0, 0)),
            pl.BlockSpec((None, 1, D_MODEL), lambda i, pos: (_cond_row(i * tm), 0, 1)),
        ]
        args += [g_next, mod_next, mod_next]
    if split_out:
        out_specs = [pl.BlockSpec((tm, D_MODEL), lambda i, pos: (jnp.minimum(i, npt - 1), 0)),
                     pl.BlockSpec((tm, D_MODEL), lambda i, pos: (jnp.maximum(i - npt, 0), 0))]
        out_shape = [jax.ShapeDtypeStruct((NP, D_MODEL), F32), jax.ShapeDtypeStruct((NS, D_MODEL), F32)]
    else:
        out_specs = [pl.BlockSpec((tm, D_MODEL), row)]
        out_shape = [jax.ShapeDtypeStruct((TT, D_MODEL), F32)]
    if next_norm is not None:
        out_specs.append(pl.BlockSpec((tm, D_MODEL), row))
        out_shape.append(jax.ShapeDtypeStruct((TT, D_MODEL), BF16))
    return pl.pallas_call(
        functools.partial(_combine_kernel, tm=tm, unroll=unroll, emit_next=next_norm is not None,
                          split_out=split_out),
        grid_spec=pltpu.PrefetchScalarGridSpec(
            num_scalar_prefetch=1,
            grid=(TT // tm,),
            in_specs=in_specs,
            out_specs=out_specs,
            scratch_shapes=[pltpu.VMEM((2, TOP_K, tm, D_MODEL), F32), pltpu.SemaphoreType.DMA((2,))],
        ),
        out_shape=out_shape,
        compiler_params=_cparams(("arbitrary",)),
        name="moe_combine",
    )(pos, *args)


def _dispatch_plan(ids):
    flat_e = ids[:, :TOP_K].reshape(N_ASSIGN)
    onehot = (flat_e[:, None] == jnp.arange(N_EXPERTS, dtype=I32)[None, :]).astype(I32)
    csum = jnp.cumsum(onehot, axis=0)
    rank = jnp.take_along_axis(csum, flat_e[:, None], axis=1)[:, 0] - 1
    counts = csum[-1]
    padded = (counts + MOE_ROWS - 1) // MOE_ROWS * MOE_ROWS
    pad_ends = jnp.cumsum(padded)
    pad_starts = pad_ends - padded
    dest = (pad_starts[flat_e] + rank).astype(I32)
    slot_tok = jnp.zeros((N_SLOTS,), I32).at[dest].set(jnp.arange(N_ASSIGN, dtype=I32) // TOP_K)
    n_blocks = (pad_ends[-1] // MOE_ROWS).astype(I32)
    blk = jnp.minimum(jnp.arange(N_BLOCKS, dtype=I32), n_blocks - 1)
    block_e = jnp.minimum(jnp.searchsorted(pad_ends, blk * MOE_ROWS, side="right"), N_EXPERTS - 1).astype(I32)
    return dest, slot_tok, block_e, n_blocks.reshape(1)


def _moe(layer, x, g, mod, w_router2, b_router, w_gate, w_up, w_down, **combine_kw):
    h_packed, ids, gates = _router_call(x, g, mod, w_router2, b_router)
    dest, slot_tok, block_e, n_blocks = _dispatch_plan(ids)
    ys = _moe_ffn_call(layer, slot_tok, block_e, n_blocks, h_packed, w_gate, w_up, w_down)
    return _combine_call(dest, ys, x, gates, mod, **combine_kw)


def _rope_tables(rot_dim):
    rows = DEC_SEQ // GRID_W
    nf = rot_dim // 4
    inv = jnp.exp(-math.log(ROPE_THETA) * jnp.arange(nf, dtype=F32) / nf)
    row_pos = jnp.repeat(jnp.arange(rows, dtype=F32), GRID_W)
    col_pos = jnp.tile(jnp.arange(GRID_W, dtype=F32), rows)
    ang = jnp.stack([row_pos[:, None] * inv, col_pos[:, None] * inv], axis=1)
    cos = jnp.broadcast_to(jnp.cos(ang)[:, :, None, :], (DEC_SEQ, 2, 2, nf)).reshape(DEC_SEQ, rot_dim)
    sgn = jnp.array([-1.0, 1.0], F32)[None, None, :, None]
    sin = (jnp.sin(ang)[:, :, None, :] * sgn).reshape(DEC_SEQ, rot_dim)
    pad = 128 - rot_dim
    if pad:
        cos = jnp.concatenate([cos, jnp.ones((DEC_SEQ, pad), F32)], axis=1)
        sin = jnp.concatenate([sin, jnp.zeros((DEC_SEQ, pad), F32)], axis=1)
    return cos, sin


def _pad_head_vec(v):
    return jnp.concatenate([v, jnp.zeros((MLA_HEAD_PAD - MLA_QK,), F32)]).reshape(1, MLA_HEAD_PAD)


def _router_weights(w_group, b_group, w_router, b_router):
    pad = ROUTER_LANES - N_GROUPS - N_EXPERTS
    w = jnp.concatenate([w_group, w_router, jnp.zeros((D_MODEL, pad), F32)], axis=1)
    w_hi = w.astype(BF16)
    w_lo = (w - w_hi.astype(F32)).astype(BF16)
    b = jnp.concatenate([b_group, b_router, jnp.zeros((pad,), F32)]).reshape(1, ROUTER_LANES)
    return jnp.concatenate([w_hi, w_lo], axis=1), b


def kernel(x_prompt, x_sample, cache_mla_ckv, cache_mla_krope, cache_diff_k, cache_diff_v, c, c_ctx, norm1_g, norm2_g, ada_w, ada_b, mla_w_in, mla_q_a_norm, mla_w_q_up, mla_kv_a_norm, mla_w_kv_up, mla_q_norm, mla_k_norm, mla_w_out, diff_w_qkv, diff_q_norm, diff_k_norm, diff_lambda_q1, diff_lambda_k1, diff_lambda_q2, diff_lambda_k2, diff_out_norm, diff_w_out, moe_w_group, moe_b_group, moe_w_router, moe_b_router, moe_w_gate, moe_w_up, moe_w_down):
    assert DEPTH == 2
    xp0 = x_prompt.reshape(NP, D_MODEL)
    xs0 = x_sample.reshape(NS, D_MODEL)
    cond = jnp.concatenate([c_ctx[None, :], c, jnp.zeros((N_COND - 1 - DEC_BATCH, D_MODEL), F32)], axis=0)
    mods = _ada_call(cond, ada_w, ada_b).reshape(DEPTH, N_COND, 1, 6 * D_MODEL)
    routers = [_router_weights(moe_w_group[i], moe_b_group[i], moe_w_router[i], moe_b_router[i])
               for i in range(DEPTH)]
    g2 = [norm2_g[i].reshape(1, D_MODEL) for i in range(DEPTH)]

    mod = mods[0]
    cos, sin = _rope_tables(MLA_ROPE)
    w_in = jnp.pad(mla_w_in[0], ((0, 0), (0, MLA_IN_PAD - MLA_IN))).astype(BF16)
    w_q = jnp.pad(mla_w_q_up[0].reshape(MLA_Q_LORA, MLA_HEADS, MLA_QK), ((0, 0), (0, 0), (0, MLA_HEAD_PAD - MLA_QK)))
    w_q = w_q.reshape(MLA_Q_LORA, MLA_HEADS * MLA_HEAD_PAD).astype(BF16)
    cq, ckv, kr = _mla_in_call(xp0, xs0, norm1_g[0].reshape(1, D_MODEL), mod, w_in,
                               mla_q_a_norm[0].reshape(1, -1), mla_kv_a_norm[0].reshape(1, -1))
    q = _q_up_call(cq, w_q, _pad_head_vec(mla_q_norm[0]), cos, sin)
    ckv_all = jnp.concatenate([ckv, cache_mla_ckv[:, 0].reshape(NCTX, MLA_KV_LORA)], axis=0)
    kr_ctx = jnp.pad(cache_mla_krope[:, 0].reshape(NCTX, MLA_ROPE), ((0, 0), (0, 128 - MLA_ROPE)))
    kr_all = jnp.concatenate([kr, kr_ctx], axis=0)
    k, v = _kv_up_call(ckv_all, kr_all, mla_w_kv_up[0].astype(BF16), _pad_head_vec(mla_k_norm[0]), cos, sin)
    o_p = _mla_attn_prompt_call(q, k, v)
    o_s = _mla_attn_latent_call(q, k, v)
    x = _out_proj_call(o_p, o_s, mla_w_out[0].astype(BF16), xp0, xs0, True, mod)
    new_ckv = ckv[:NP].reshape(BATCH, 1, SEQ, MLA_KV_LORA)
    new_krope = kr[:NP, :MLA_ROPE].reshape(BATCH, 1, SEQ, MLA_ROPE)
    x, h = _moe(0, x, g2[0], mod, *routers[0], moe_w_gate, moe_w_up, moe_w_down,
                next_norm=(norm1_g[1].reshape(1, D_MODEL), mods[1]))

    mod = mods[1]
    lam_init = 0.8 - 0.6 * math.exp(-0.3 * 1)
    cos, sin = _rope_tables(DIFF_HEAD_DIM)
    q, k_p, k_s, v_p, v_s = _diff_qkv_call(h, diff_w_qkv[0].astype(BF16), diff_q_norm[0].reshape(1, -1),
                                           diff_k_norm[0].reshape(1, -1), cos, sin)
    lams = [t[0].reshape(1, DIFF_HEAD_DIM) for t in
            (diff_lambda_q1, diff_lambda_k1, diff_lambda_q2, diff_lambda_k2)]
    out_norm = diff_out_norm[0].reshape(1, DIFF_V_DIM)
    o_p = _diff_attn_prompt_call(q, k_p, v_p, lams, out_norm, lam_init)
    o_s = _diff_attn_latent_call(q, k_s, v_s, cache_diff_k[:, 0].reshape(NCTX, D_MODEL),
                                 cache_diff_v[:, 0].reshape(NCTX, D_MODEL), lams, out_norm, lam_init)
    x = _out_proj_call(o_p, o_s, diff_w_out[0].astype(BF16), x, x, False, mod)
    new_k = k_p.reshape(BATCH, 1, SEQ, DIFF_HEADS, 2, DIFF_HEAD_DIM)
    new_v = v_p.reshape(BATCH, 1, SEQ, DIFF_HEADS, DIFF_V_DIM)
    y_p, y_s = _moe(1, x, g2[1], mod, *routers[1], moe_w_gate, moe_w_up, moe_w_down, split_out=True)

    return (y_p.reshape(BATCH, SEQ, D_MODEL), y_s.reshape(DEC_BATCH, DEC_SEQ, D_MODEL),
            new_ckv, new_krope, new_k, new_v)
```

```python
import functools
import math

import jax
import jax.numpy as jnp
from jax import lax
from jax.experimental import pallas as pl
from jax.experimental.pallas import tpu as pltpu

F32 = jnp.float32
BF16 = jnp.bfloat16
I32 = jnp.int32
U32 = jnp.uint32

D_MODEL = 4096
BATCH = 32
SEQ = 256
DEPTH = 2
DEC_BATCH = 4
DEC_SEQ = 1024
PAST_LEN = 256
GRID_W = 64
MLA_HEADS = 32
MLA_NOPE = 128
MLA_ROPE = 64
MLA_QK = MLA_NOPE + MLA_ROPE
MLA_V = 128
MLA_Q_LORA = 1024
MLA_KV_LORA = 512
MLA_IN = MLA_Q_LORA + MLA_KV_LORA + MLA_ROPE
MLA_IN_PAD = MLA_Q_LORA + MLA_KV_LORA + 128
MLA_HEAD_PAD = 256
DIFF_HEAD_DIM = 128
DIFF_HEADS = 16
DIFF_V_DIM = 256
N_GROUPS = 8
EXPERTS_PER_GROUP = 8
N_EXPERTS = 64
TOP_K = 2
EXPERT_HIDDEN = 1024
ROPE_THETA = 10000.0
NORM_EPS = 1e-6

NP = BATCH * SEQ
NS = DEC_BATCH * DEC_SEQ
TT = NP + NS
NCTX = DEC_BATCH * PAST_LEN
N_COND = 8
N_ASSIGN = TT * TOP_K
ROUTER_LANES = 128
HALF_D = D_MODEL // 2
ROW_TILES = HALF_D // 128

MOE_ROWS = 512
PIECE_ROWS = 512
PIECE_COLS = EXPERT_HIDDEN
GU_CHUNKS = D_MODEL // PIECE_ROWS
DN_PHASES = D_MODEL // PIECE_COLS
N_PHASES = GU_CHUNKS + DN_PHASES
LOOKAHEAD = 4
RING_SLOTS = 2 * (LOOKAHEAD + 1)
OUT_COLS = 512
OUT_CHUNKS = D_MODEL // OUT_COLS
OUT_SLOTS = 4
GU_PHASE_ROWS = 48
DN_PHASE_ROWS = (MOE_ROWS - GU_CHUNKS * GU_PHASE_ROWS) // DN_PHASES
ROW_WAIT_GROUP = 64
N_BLOCKS = N_ASSIGN // MOE_ROWS + N_EXPERTS + 1
N_SLOTS = N_BLOCKS * MOE_ROWS

VMEM_LIMIT = 56 * 1024 * 1024


def _cparams(sem):
    return pltpu.CompilerParams(dimension_semantics=sem, vmem_limit_bytes=VMEM_LIMIT)


def _cond_row(row0):
    return jnp.where(row0 < NP, 0, 1 + (row0 - NP) // DEC_SEQ)


def _rms(x, g):
    return x * lax.rsqrt(jnp.mean(x * x, axis=-1, keepdims=True) + NORM_EPS) * g


def _modulate(x, g, shift, scale):
    return _rms(x, g) * (1.0 + scale) + shift


def _dot(a, b):
    return jnp.dot(a, b, preferred_element_type=F32)


def _dot_nt(a, b):
    return lax.dot_general(a, b, (((1,), (1,)), ((), ())), preferred_element_type=F32)


def _rope128(x, cos, sin_signed, half):
    lane = lax.broadcasted_iota(I32, x.shape, 1)
    first_half = ((lane // half) % 2) == 0
    partner = jnp.where(first_half, pltpu.roll(x, 128 - half, 1), pltpu.roll(x, half, 1))
    return x * cos + partner * sin_signed


def _latent_tables(cos_ref, sin_ref, is_latent):
    m = is_latent.astype(F32)
    return 1.0 + m * (cos_ref[...] - 1.0), m * sin_ref[...]


def _row_pair_specs(tm, tn, split):
    npt = NP // tm
    off = npt if split else 0
    last = D_MODEL // tn - 1

    def prompt_map(i, *a):
        return (jnp.minimum(i, npt - 1), jnp.where(i < npt, a[0], last) if last else 0)

    def latent_map(i, *a):
        return (jnp.maximum(i, npt) - off, jnp.where(i >= npt, a[0], 0) if last else 0)

    return pl.BlockSpec((tm, tn), prompt_map), pl.BlockSpec((tm, tn), latent_map)


def _ada_kernel(c_ref, w_ref, b_ref, o_ref):
    c = c_ref[...]
    a = (c * jax.nn.sigmoid(c)).astype(BF16)
    o_ref[...] = _dot(a, w_ref[...].astype(BF16)) + b_ref[...]


def _ada_call(cond, ada_w, ada_b):
    tn = 512
    n = 6 * D_MODEL
    return pl.pallas_call(
        _ada_kernel,
        grid=(DEPTH, n // tn),
        in_specs=[
            pl.BlockSpec((N_COND, D_MODEL), lambda l, j: (0, 0)),
            pl.BlockSpec((None, D_MODEL, tn), lambda l, j: (l, 0, j)),
            pl.BlockSpec((None, 1, tn), lambda l, j: (l, 0, j)),
        ],
        out_specs=pl.BlockSpec((None, N_COND, tn), lambda l, j: (l, 0, j)),
        out_shape=jax.ShapeDtypeStruct((DEPTH, N_COND, n), F32),
        compiler_params=_cparams(("arbitrary", "arbitrary")),
        name="ada_table",
    )(cond, ada_w, ada_b.reshape(DEPTH, 1, n))


def _mod_spec(tm, chunk, width=D_MODEL):
    per = D_MODEL // width
    return pl.BlockSpec((None, 1, width),
                        lambda i, *a: (_cond_row(i * tm), 0, chunk * per + (a[0] if per > 1 else 0)))


def _mla_in_kernel(xp_ref, xs_ref, g_ref, sh_ref, sc_ref, w_ref, qg_ref, kvg_ref, cq_ref, ckv_ref, kr_ref, *, tm):
    def run(x_ref):
        h = _modulate(x_ref[...], g_ref[...], sh_ref[...], sc_ref[...]).astype(BF16)
        p = _dot(h, w_ref[...])
        cq_ref[...] = _rms(p[:, :MLA_Q_LORA], qg_ref[...]).astype(BF16)
        ckv_ref[...] = _rms(p[:, MLA_Q_LORA:MLA_Q_LORA + MLA_KV_LORA], kvg_ref[...])
        kr_ref[...] = p[:, MLA_Q_LORA + MLA_KV_LORA:]

    is_prompt = pl.program_id(0) < NP // tm
    pl.when(is_prompt)(lambda: run(xp_ref))
    pl.when(jnp.logical_not(is_prompt))(lambda: run(xs_ref))


def _mla_in_call(x_prompt, x_latent, g, mod, w_in, q_a_norm, kv_a_norm):
    tm = 256
    xp_spec, xs_spec = _row_pair_specs(tm, D_MODEL, split=True)
    return pl.pallas_call(
        functools.partial(_mla_in_kernel, tm=tm),
        grid=(TT // tm,),
        in_specs=[
            xp_spec,
            xs_spec,
            pl.BlockSpec((1, D_MODEL), lambda i: (0, 0)),
            _mod_spec(tm, 0),
            _mod_spec(tm, 1),
            pl.BlockSpec((D_MODEL, MLA_IN_PAD), lambda i: (0, 0)),
            pl.BlockSpec((1, MLA_Q_LORA), lambda i: (0, 0)),
            pl.BlockSpec((1, MLA_KV_LORA), lambda i: (0, 0)),
        ],
        out_specs=[
            pl.BlockSpec((tm, MLA_Q_LORA), lambda i: (i, 0)),
            pl.BlockSpec((tm, MLA_KV_LORA), lambda i: (i, 0)),
            pl.BlockSpec((tm, 128), lambda i: (i, 0)),
        ],
        out_shape=[
            jax.ShapeDtypeStruct((TT, MLA_Q_LORA), BF16),
            jax.ShapeDtypeStruct((TT, MLA_KV_LORA), F32),
            jax.ShapeDtypeStruct((TT, 128), F32),
        ],
        compiler_params=_cparams(("arbitrary",)),
        name="mla_in",
    )(x_prompt, x_latent, g, mod, mod, w_in, q_a_norm, kv_a_norm)


def _q_up_kernel(c_ref, w_ref, g_ref, cos_ref, sin_ref, q_ref, *, tm, heads):
    i = pl.program_id(0)
    cos, sin = _latent_tables(cos_ref, sin_ref, i >= NP // tm)
    o = _dot(c_ref[...], w_ref[...])
    g = g_ref[...] * (MLA_QK ** -0.5)
    for hh in range(heads):
        s = o[:, hh * 256:(hh + 1) * 256]
        r = lax.rsqrt(jnp.sum(s * s, axis=-1, keepdims=True) / MLA_QK + NORM_EPS)
        lo = s[:, :128] * r * g[:, :128]
        hi = _rope128(s[:, 128:] * r * g[:, 128:], cos, sin, MLA_ROPE // 4)
        q_ref[:, hh * 256:hh * 256 + 128] = lo.astype(BF16)
        q_ref[:, hh * 256 + 128:(hh + 1) * 256] = hi.astype(BF16)


def _pos_block(i, tm):
    return jnp.clip(i - NP // tm, 0, NS // tm - 1) % (DEC_SEQ // tm)


def _q_up_call(cq, w_q, q_norm_pad, cos, sin):
    tm, heads = 512, 16
    tn = heads * MLA_HEAD_PAD
    return pl.pallas_call(
        functools.partial(_q_up_kernel, tm=tm, heads=heads),
        grid=(TT // tm, MLA_HEADS // heads),
        in_specs=[
            pl.BlockSpec((tm, MLA_Q_LORA), lambda i, j: (i, 0)),
            pl.BlockSpec((MLA_Q_LORA, tn), lambda i, j: (0, j)),
            pl.BlockSpec((1, MLA_HEAD_PAD), lambda i, j: (0, 0)),
            pl.BlockSpec((tm, 128), lambda i, j: (_pos_block(i, tm), 0)),
            pl.BlockSpec((tm, 128), lambda i, j: (_pos_block(i, tm), 0)),
        ],
        out_specs=pl.BlockSpec((tm, tn), lambda i, j: (i, j)),
        out_shape=jax.ShapeDtypeStruct((TT, MLA_HEADS * MLA_HEAD_PAD), BF16),
        compiler_params=_cparams(("arbitrary", "arbitrary")),
        name="mla_q_up",
    )(cq, w_q, q_norm_pad, cos, sin)


def _kv_up_kernel(c_ref, kr_ref, w_ref, g_ref, cos_ref, sin_ref, k_ref, v_ref, *, tm, heads):
    i = pl.program_id(0)
    cos, sin = _latent_tables(cos_ref, sin_ref, (i >= NP // tm) & (i < TT // tm))
    kv = _dot(c_ref[...].astype(BF16), w_ref[...])
    g = g_ref[...]
    kr = kr_ref[...]
    ss_rope = jnp.sum(kr * kr, axis=-1, keepdims=True)
    base = _rope128(kr * g[:, 128:], cos, sin, MLA_ROPE // 4)
    for hh in range(heads):
        kn = kv[:, hh * 256:hh * 256 + 128]
        r = lax.rsqrt((jnp.sum(kn * kn, axis=-1, keepdims=True) + ss_rope) / MLA_QK + NORM_EPS)
        k_ref[:, hh * 256:hh * 256 + 128] = (kn * r * g[:, :128]).astype(BF16)
        k_ref[:, hh * 256 + 128:(hh + 1) * 256] = (base * r).astype(BF16)
        v_ref[:, hh * 128:(hh + 1) * 128] = kv[:, hh * 256 + 128:(hh + 1) * 256].astype(BF16)


def _kv_up_call(ckv_all, kr_all, w_kv, k_norm_pad, cos, sin):
    tm, heads = 512, 16
    rows = TT + NCTX
    return pl.pallas_call(
        functools.partial(_kv_up_kernel, tm=tm, heads=heads),
        grid=(rows // tm, MLA_HEADS // heads),
        in_specs=[
            pl.BlockSpec((tm, MLA_KV_LORA), lambda i, j: (i, 0)),
            pl.BlockSpec((tm, 128), lambda i, j: (i, 0)),
            pl.BlockSpec((MLA_KV_LORA, heads * 256), lambda i, j: (0, j)),
            pl.BlockSpec((1, MLA_HEAD_PAD), lambda i, j: (0, 0)),
            pl.BlockSpec((tm, 128), lambda i, j: (_pos_block(i, tm), 0)),
            pl.BlockSpec((tm, 128), lambda i, j: (_pos_block(i, tm), 0)),
        ],
        out_specs=[
            pl.BlockSpec((tm, heads * MLA_HEAD_PAD), lambda i, j: (i, j)),
            pl.BlockSpec((tm, heads * MLA_V), lambda i, j: (i, j)),
        ],
        out_shape=[
            jax.ShapeDtypeStruct((rows, MLA_HEADS * MLA_HEAD_PAD), BF16),
            jax.ShapeDtypeStruct((rows, MLA_HEADS * MLA_V), BF16),
        ],
        compiler_params=_cparams(("arbitrary", "arbitrary")),
        name="mla_kv_up",
    )(ckv_all, kr_all, w_kv, k_norm_pad, cos, sin)


def _mla_attn_prompt_kernel(q_ref, k_ref, v_ref, o_ref, *, heads):
    for hh in range(heads):
        q = q_ref[:, hh * 256:(hh + 1) * 256]
        k = k_ref[:, hh * 256:(hh + 1) * 256]
        s = _dot_nt(q, k)
        p = jnp.exp(s - jnp.max(s, axis=-1, keepdims=True))
        l = jnp.sum(p, axis=-1, keepdims=True)
        o = _dot(p.astype(BF16), v_ref[:, hh * 128:(hh + 1) * 128]) / l
        o_ref[:, hh * 128:(hh + 1) * 128] = o.astype(BF16)


def _mla_attn_prompt_call(q, k, v):
    heads = 8
    return pl.pallas_call(
        functools.partial(_mla_attn_prompt_kernel, heads=heads),
        grid=(BATCH, MLA_HEADS // heads),
        in_specs=[
            pl.BlockSpec((SEQ, heads * 256), lambda b, j: (b, j)),
            pl.BlockSpec((SEQ, heads * 256), lambda b, j: (b, j)),
            pl.BlockSpec((SEQ, heads * 128), lambda b, j: (b, j)),
        ],
        out_specs=pl.BlockSpec((SEQ, heads * 128), lambda b, j: (b, j)),
        out_shape=jax.ShapeDtypeStruct((NP, MLA_HEADS * MLA_V), BF16),
        compiler_params=_cparams(("arbitrary", "arbitrary")),
        name="mla_attn_prompt",
    )(q, k, v)


def _mla_attn_latent_kernel(q_ref, kl_ref, kc_ref, vl_ref, vc_ref, o_ref, *, heads):
    for hh in range(heads):
        q = q_ref[:, hh * 256:(hh + 1) * 256]
        sl = _dot_nt(q, kl_ref[:, hh * 256:(hh + 1) * 256])
        sc = _dot_nt(q, kc_ref[:, hh * 256:(hh + 1) * 256])
        m = jnp.maximum(jnp.max(sl, axis=-1, keepdims=True), jnp.max(sc, axis=-1, keepdims=True))
        pl_ = jnp.exp(sl - m)
        pc = jnp.exp(sc - m)
        l = jnp.sum(pl_, axis=-1, keepdims=True) + jnp.sum(pc, axis=-1, keepdims=True)
        o = _dot(pl_.astype(BF16), vl_ref[:, hh * 128:(hh + 1) * 128])
        o = o + _dot(pc.astype(BF16), vc_ref[:, hh * 128:(hh + 1) * 128])
        o_ref[:, hh * 128:(hh + 1) * 128] = (o / l).astype(BF16)


def _mla_attn_latent_call(q, k, v):
    heads, tq = 8, 512
    nq = DEC_SEQ // tq
    return pl.pallas_call(
        functools.partial(_mla_attn_latent_kernel, heads=heads),
        grid=(DEC_BATCH, MLA_HEADS // heads, nq),
        in_specs=[
            pl.BlockSpec((tq, heads * 256), lambda b, j, t: (NP // tq + b * nq + t, j)),
            pl.BlockSpec((DEC_SEQ, heads * 256), lambda b, j, t: (NP // DEC_SEQ + b, j)),
            pl.BlockSpec((PAST_LEN, heads * 256), lambda b, j, t: (TT // PAST_LEN + b, j)),
            pl.BlockSpec((DEC_SEQ, heads * 128), lambda b, j, t: (NP // DEC_SEQ + b, j)),
            pl.BlockSpec((PAST_LEN, heads * 128), lambda b, j, t: (TT // PAST_LEN + b, j)),
        ],
        out_specs=pl.BlockSpec((tq, heads * 128), lambda b, j, t: (b * nq + t, j)),
        out_shape=jax.ShapeDtypeStruct((NS, MLA_HEADS * MLA_V), BF16),
        compiler_params=_cparams(("arbitrary", "arbitrary", "arbitrary")),
        name="mla_attn_latent",
    )(q, k, k, v, v)


def _out_proj_kernel(ap_ref, as_ref, w_ref, xp_ref, xs_ref, gate_ref, o_ref, *, tm):
    def finish(a_ref, x_ref):
        o_ref[...] = x_ref[...] + gate_ref[...] * _dot(a_ref[...], w_ref[...])

    is_prompt = pl.program_id(0) < NP // tm
    pl.when(is_prompt)(lambda: finish(ap_ref, xp_ref))
    pl.when(jnp.logical_not(is_prompt))(lambda: finish(as_ref, xs_ref))


def _out_proj_call(a_prompt, a_latent, w, x_prompt, x_latent, x_split, mod):
    tm, tn = 512, 1024
    ap_spec, as_spec = _row_pair_specs(tm, D_MODEL, split=True)
    xp_spec, xs_spec = _row_pair_specs(tm, tn, split=x_split)
    return pl.pallas_call(
        functools.partial(_out_proj_kernel, tm=tm),
        grid=(TT // tm, D_MODEL // tn),
        in_specs=[
            ap_spec,
            as_spec,
            pl.BlockSpec((D_MODEL, tn), lambda i, j: (0, j)),
            xp_spec,
            xs_spec,
            _mod_spec(tm, 2, width=tn),
        ],
        out_specs=pl.BlockSpec((tm, tn), lambda i, j: (i, j)),
        out_shape=jax.ShapeDtypeStruct((TT, D_MODEL), F32),
        compiler_params=_cparams(("arbitrary", "arbitrary")),
        name="attn_out_proj",
    )(a_prompt, a_latent, w, x_prompt, x_latent, mod)


def _diff_qkv_kernel(h_ref, w_ref, qn_ref, kn_ref, cos_ref, sin_ref, q_ref, kp_ref, kl_ref, vp_ref, vl_ref,
                     *, tm, nq, sub):
    i = pl.program_id(0)
    j = pl.program_id(1)
    is_prompt = i < NP // tm
    cos, sin = _latent_tables(cos_ref, sin_ref, jnp.logical_not(is_prompt))
    n_sub = w_ref.shape[1] // sub

    def store_rows(prompt_ref, latent_ref, pieces):
        def fill(ref):
            for c0, y in pieces:
                ref[:, c0:c0 + y.shape[1]] = y
        pl.when(is_prompt)(lambda: fill(prompt_ref))
        pl.when(jnp.logical_not(is_prompt))(lambda: fill(latent_ref))

    def norm_rope(o, g, scale):
        return [_rope128(_rms(o[:, s * 128:(s + 1) * 128], g) * scale, cos, sin, DIFF_HEAD_DIM // 4)
                for s in range(sub // 128)]

    @pl.when(j < nq)
    def _():
        for t in range(n_sub):
            o = _dot(h_ref[...], w_ref[:, t * sub:(t + 1) * sub])
            for s, y in enumerate(norm_rope(o, qn_ref[...], DIFF_HEAD_DIM ** -0.5)):
                q_ref[:, t * sub + s * 128:t * sub + (s + 1) * 128] = y.astype(BF16)

    @pl.when((j >= nq) & (j < 2 * nq))
    def _():
        pieces = []
        for t in range(n_sub):
            o = _dot(h_ref[...], w_ref[:, t * sub:(t + 1) * sub])
            pieces += [(t * sub + s * 128, y) for s, y in enumerate(norm_rope(o, kn_ref[...], 1.0))]
        store_rows(kp_ref, kl_ref, pieces)

    @pl.when(j >= 2 * nq)
    def _():
        store_rows(vp_ref, vl_ref, [(0, _dot(h_ref[...], w_ref[...]))])


def _diff_qkv_call(h, w_qkv, q_norm, k_norm, cos, sin):
    tm, tn, sub = 512, 1024, 256
    nq = D_MODEL // tn
    npt = NP // tm

    def kv_specs(first):
        col = lambda j: jnp.clip(j - first, 0, nq - 1)
        return [pl.BlockSpec((tm, tn), lambda i, j: (jnp.minimum(i, npt - 1), jnp.where(i < npt, col(j), nq - 1))),
                pl.BlockSpec((tm, tn), lambda i, j: (jnp.maximum(i - npt, 0), jnp.where(i >= npt, col(j), 0)))]

    kv_shapes = [jax.ShapeDtypeStruct((NP, D_MODEL), F32), jax.ShapeDtypeStruct((NS, D_MODEL), F32)]
    return pl.pallas_call(
        functools.partial(_diff_qkv_kernel, tm=tm, nq=nq, sub=sub),
        grid=(TT // tm, 3 * nq),
        in_specs=[
            pl.BlockSpec((tm, D_MODEL), lambda i, j: (i, 0)),
            pl.BlockSpec((D_MODEL, tn), lambda i, j: (0, j)),
            pl.BlockSpec((1, DIFF_HEAD_DIM), lambda i, j: (0, 0)),
            pl.BlockSpec((1, DIFF_HEAD_DIM), lambda i, j: (0, 0)),
            pl.BlockSpec((tm, 128), lambda i, j: (_pos_block(i, tm), 0)),
            pl.BlockSpec((tm, 128), lambda i, j: (_pos_block(i, tm), 0)),
        ],
        out_specs=[pl.BlockSpec((tm, tn), lambda i, j: (i, jnp.minimum(j, nq - 1)))]
        + kv_specs(nq) + kv_specs(2 * nq),
        out_shape=[jax.ShapeDtypeStruct((TT, D_MODEL), BF16)] + kv_shapes + kv_shapes,
        compiler_params=_cparams(("arbitrary", "arbitrary")),
        name="diff_qkv",
    )(h, w_qkv, q_norm, k_norm, cos, sin)


def _diff_lambda(lq1_ref, lk1_ref, lq2_ref, lk2_ref, lam_init):
    e1 = jnp.exp(jnp.sum(lq1_ref[...] * lk1_ref[...], axis=-1, keepdims=True))
    e2 = jnp.exp(jnp.sum(lq2_ref[...] * lk2_ref[...], axis=-1, keepdims=True))
    return e1 - e2 + lam_init


def _softmax_rows(s):
    p = jnp.exp(s - jnp.max(s, axis=-1, keepdims=True))
    return p / jnp.sum(p, axis=-1, keepdims=True)


def _diff_attn_prompt_kernel(q_ref, k_ref, v_ref, lq1, lk1, lq2, lk2, on_ref, o_ref, *, heads, lam_init):
    lam = _diff_lambda(lq1, lk1, lq2, lk2, lam_init)
    for hh in range(heads):
        c0 = hh * 256
        p1 = _softmax_rows(_dot_nt(q_ref[:, c0:c0 + 128], k_ref[:, c0:c0 + 128].astype(BF16)))
        p2 = _softmax_rows(_dot_nt(q_ref[:, c0 + 128:c0 + 256], k_ref[:, c0 + 128:c0 + 256].astype(BF16)))
        a = (p1 - lam * p2).astype(BF16)
        o = _dot(a, v_ref[:, c0:c0 + 256].astype(BF16))
        o_ref[:, c0:c0 + 256] = (_rms(o, on_ref[...]) * (1.0 - lam_init)).astype(BF16)


def _vec_spec(n):
    return pl.BlockSpec((1, n), lambda *a: (0, 0))


def _diff_attn_prompt_call(q, k, v, lams, out_norm, lam_init):
    heads = 4
    w = heads * 256
    blk = pl.BlockSpec((SEQ, w), lambda b, j: (b, j))
    return pl.pallas_call(
        functools.partial(_diff_attn_prompt_kernel, heads=heads, lam_init=lam_init),
        grid=(BATCH, DIFF_HEADS // heads),
        in_specs=[blk, blk, blk] + [_vec_spec(DIFF_HEAD_DIM)] * 4 + [_vec_spec(DIFF_V_DIM)],
        out_specs=blk,
        out_shape=jax.ShapeDtypeStruct((NP, D_MODEL), BF16),
        compiler_params=_cparams(("arbitrary", "arbitrary")),
        name="diff_attn_prompt",
    )(q, k, v, *lams, out_norm)


def _softmax2(sa, sb):
    m = jnp.maximum(jnp.max(sa, axis=-1, keepdims=True), jnp.max(sb, axis=-1, keepdims=True))
    pa = jnp.exp(sa - m)
    pb = jnp.exp(sb - m)
    l = jnp.sum(pa, axis=-1, keepdims=True) + jnp.sum(pb, axis=-1, keepdims=True)
    return pa / l, pb / l


def _diff_attn_latent_kernel(q_ref, kl_ref, kc_ref, vl_ref, vc_ref, lq1, lk1, lq2, lk2, on_ref,
                             o_ref, *, heads, lam_init):
    lam = _diff_lambda(lq1, lk1, lq2, lk2, lam_init)
    for hh in range(heads):
        c0 = hh * 256
        q1 = q_ref[:, c0:c0 + 128]
        q2 = q_ref[:, c0 + 128:c0 + 256]
        p1l, p1c = _softmax2(_dot_nt(q1, kl_ref[:, c0:c0 + 128].astype(BF16)),
                             _dot_nt(q1, kc_ref[:, c0:c0 + 128].astype(BF16)))
        p2l, p2c = _softmax2(_dot_nt(q2, kl_ref[:, c0 + 128:c0 + 256].astype(BF16)),
                             _dot_nt(q2, kc_ref[:, c0 + 128:c0 + 256].astype(BF16)))
        o = _dot((p1l - lam * p2l).astype(BF16), vl_ref[:, c0:c0 + 256].astype(BF16))
        o = o + _dot((p1c - lam * p2c).astype(BF16), vc_ref[:, c0:c0 + 256].astype(BF16))
        o_ref[:, c0:c0 + 256] = (_rms(o, on_ref[...]) * (1.0 - lam_init)).astype(BF16)


def _diff_attn_latent_call(q, k, v, k_ctx, v_ctx, lams, out_norm, lam_init):
    heads, tq = 4, 256
    w = heads * 256
    nq = DEC_SEQ // tq
    lat = pl.BlockSpec((DEC_SEQ, w), lambda b, j, t: (b, j))
    ctx = pl.BlockSpec((PAST_LEN, w), lambda b, j, t: (b, j))
    return pl.pallas_call(
        functools.partial(_diff_attn_latent_kernel, heads=heads, lam_init=lam_init),
        grid=(DEC_BATCH, DIFF_HEADS // heads, nq),
        in_specs=[pl.BlockSpec((tq, w), lambda b, j, t: (NP // tq + b * nq + t, j)), lat, ctx, lat, ctx]
        + [_vec_spec(DIFF_HEAD_DIM)] * 4 + [_vec_spec(DIFF_V_DIM)],
        out_specs=pl.BlockSpec((tq, w), lambda b, j, t: (b * nq + t, j)),
        out_shape=jax.ShapeDtypeStruct((NS, D_MODEL), BF16),
        compiler_params=_cparams(("arbitrary", "arbitrary", "arbitrary")),
        name="diff_attn_latent",
    )(q, k, k_ctx, v, v_ctx, *lams, out_norm)


def _router_kernel(x_ref, g_ref, sh_ref, sc_ref, w_ref, b_ref, h_ref, ids_ref, gates_ref):
    h = _modulate(x_ref[...], g_ref[...], sh_ref[...], sc_ref[...])
    h_hi = h.astype(BF16)
    h_hi32 = h_hi.astype(F32)
    bits = lax.bitcast_convert_type(h_hi32, U32)
    h_ref[...] = (bits[:, :HALF_D] >> 16) | (bits[:, HALF_D:] & jnp.uint32(0xFFFF0000))

    h_lo = (h - h_hi32).astype(BF16)
    two = _dot(h_hi, w_ref[...])
    logits = two[:, :ROUTER_LANES] + two[:, ROUTER_LANES:] + _dot(h_lo, w_ref[:, :ROUTER_LANES]) + b_ref[...]

    lane = lax.broadcasted_iota(I32, logits.shape, 1).astype(F32)
    neg = -jnp.inf
    gl = jnp.where(lane < N_GROUPS, logits, neg)
    gmax = jnp.max(gl, axis=-1, keepdims=True)
    g_sel = jnp.min(jnp.where(gl == gmax, lane, ROUTER_LANES), axis=-1, keepdims=True)
    p_group = 1.0 / jnp.sum(jnp.exp(gl - gmax), axis=-1, keepdims=True)

    lo = N_GROUPS + g_sel * EXPERTS_PER_GROUP
    el = jnp.where(lane >= lo, jnp.where(lane < lo + EXPERTS_PER_GROUP, logits, neg), neg)
    v1 = jnp.max(el, axis=-1, keepdims=True)
    i1 = jnp.min(jnp.where(el == v1, lane, ROUTER_LANES), axis=-1, keepdims=True)
    el2 = jnp.where(lane == i1, neg, el)
    v2 = jnp.max(el2, axis=-1, keepdims=True)
    i2 = jnp.min(jnp.where(el2 == v2, lane, ROUTER_LANES), axis=-1, keepdims=True)
    e2 = jnp.exp(v2 - v1)
    w1 = p_group * (1.0 / (1.0 + e2))
    w2 = p_group * (e2 / (1.0 + e2))
    ids = jnp.where(lane == 0, i1 - N_GROUPS, jnp.where(lane == 1, i2 - N_GROUPS, 0.0))
    ids_ref[...] = ids.astype(I32)
    gates_ref[...] = jnp.where(lane == 0, w1, jnp.where(lane == 1, w2, 0.0))


def _router_call(x, g, mod, w_router2, b_router):
    tm = 256
    return pl.pallas_call(
        _router_kernel,
        grid=(TT // tm,),
        in_specs=[
            pl.BlockSpec((tm, D_MODEL), lambda i: (i, 0)),
            pl.BlockSpec((1, D_MODEL), lambda i: (0, 0)),
            _mod_spec(tm, 3),
            _mod_spec(tm, 4),
            pl.BlockSpec((D_MODEL, 2 * ROUTER_LANES), lambda i: (0, 0)),
            pl.BlockSpec((1, ROUTER_LANES), lambda i: (0, 0)),
        ],
        out_specs=[
            pl.BlockSpec((tm, HALF_D), lambda i: (i, 0)),
            pl.BlockSpec((tm, ROUTER_LANES), lambda i: (i, 0)),
            pl.BlockSpec((tm, ROUTER_LANES), lambda i: (i, 0)),
        ],
        out_shape=[
            jax.ShapeDtypeStruct((TT, HALF_D), U32),
            jax.ShapeDtypeStruct((TT, ROUTER_LANES), I32),
            jax.ShapeDtypeStruct((TT, ROUTER_LANES), F32),
        ],
        compiler_params=_cparams(("arbitrary",)),
        name="moe_router",
    )(x, g, mod, mod, w_router2, b_router)


def _moe_ffn_kernel(tok_ref, be_ref, nb_ref, h_hbm, wg_hbm, wu_hbm, wd_hbm, ys_hbm,
                    xrows, x_lo, x_hi, ring, gate_acc, up_acc, hid, stage,
                    row_sem, ring_sem, out_sem, *, layer):
    b = pl.program_id(0)
    nb = nb_ref[0]

    def row_copy(blk, r):
        return pltpu.make_async_copy(h_hbm.at[tok_ref[blk * MOE_ROWS + r]],
                                     xrows.at[pl.ds(r * ROW_TILES, ROW_TILES), :], row_sem)

    def piece_slot(blk, t, k):
        return (blk * (2 * N_PHASES) + 2 * t + k) % RING_SLOTS

    def gu_chunk(t, k):
        return (t + k * (GU_CHUNKS // 2)) % GU_CHUNKS

    def x_chunk(chunk):
        x_half = x_lo if chunk * PIECE_ROWS < HALF_D else x_hi
        k0 = (chunk * PIECE_ROWS) % HALF_D
        return x_half[:, k0:k0 + PIECE_ROWS]

    def piece_copy(blk, t, k):
        e, s = be_ref[blk], piece_slot(blk, t, k)
        if t < GU_CHUNKS:
            src = (wg_hbm, wu_hbm)[k].at[layer, e, pl.ds(gu_chunk(t, k) * PIECE_ROWS, PIECE_ROWS), :]
        else:
            src = wd_hbm.at[layer, e, pl.ds(k * PIECE_ROWS, PIECE_ROWS),
                            pl.ds((t - GU_CHUNKS) * PIECE_COLS, PIECE_COLS)]
        return pltpu.make_async_copy(src, ring.at[s], ring_sem.at[s])

    def request_phase(blk, t):
        for k in range(2):
            piece_copy(blk, t, k).start()

    def await_phase(blk, t):
        for k in range(2):
            piece_copy(blk, t, k).wait()

    def out_copy(blk, c, src_slot):
        return pltpu.make_async_copy(
            stage.at[src_slot], ys_hbm.at[pl.ds(blk * MOE_ROWS, MOE_ROWS), pl.ds(c * OUT_COLS, OUT_COLS)],
            out_sem.at[src_slot])

    def request_next_rows(first, count):
        for r in range(count):
            row_copy(b + 1, first + r).start()

    def request_ahead(t):
        ahead = t + LOOKAHEAD
        if ahead < N_PHASES:
            request_phase(b, ahead)
        else:
            request_phase(b + 1, ahead - N_PHASES)

    @pl.when(b == 0)
    def _():
        for t in range(LOOKAHEAD):
            request_phase(0, t)

        def start(r, carry):
            row_copy(0, r).start()
            return carry
        lax.fori_loop(0, MOE_ROWS, start, 0)

    @pl.when(b <= nb)
    def _():
        def wait(g, carry):
            for r in range(ROW_WAIT_GROUP):
                row_copy(b, g * ROW_WAIT_GROUP + r).wait()
            return carry
        lax.fori_loop(0, MOE_ROWS // ROW_WAIT_GROUP, wait, 0)

    @pl.when(b < nb)
    def _():
        for q in range(ROW_TILES):
            words = xrows[pl.ds(q, MOE_ROWS, stride=ROW_TILES), :]
            x_lo[:, q * 128:(q + 1) * 128] = lax.bitcast_convert_type(words << 16, F32).astype(BF16)
            x_hi[:, q * 128:(q + 1) * 128] = lax.bitcast_convert_type(
                words & jnp.uint32(0xFFFF0000), F32).astype(BF16)

        for t in range(GU_CHUNKS):
            request_ahead(t)
            request_next_rows(t * GU_PHASE_ROWS, GU_PHASE_ROWS)
            await_phase(b, t)
            gate = _dot(x_chunk(gu_chunk(t, 0)), ring[piece_slot(b, t, 0)].astype(BF16))
            up = _dot(x_chunk(gu_chunk(t, 1)), ring[piece_slot(b, t, 1)].astype(BF16))
            if t > 0:
                gate = gate_acc[...] + gate
                up = up_acc[...] + up
            if t + 1 < GU_CHUNKS:
                gate_acc[...] = gate
                up_acc[...] = up
            else:
                hid[...] = (gate * jax.nn.sigmoid(gate) * up).astype(BF16)

        for n in range(DN_PHASES):
            t = GU_CHUNKS + n
            request_ahead(t)
            request_next_rows(GU_CHUNKS * GU_PHASE_ROWS + n * DN_PHASE_ROWS, DN_PHASE_ROWS)
            await_phase(b, t)
            acc = (_dot(hid[:, :PIECE_ROWS], ring[piece_slot(b, t, 0)].astype(BF16))
                   + _dot(hid[:, PIECE_ROWS:], ring[piece_slot(b, t, 1)].astype(BF16)))
            for half in range(PIECE_COLS // OUT_COLS):
                c = n * (PIECE_COLS // OUT_COLS) + half
                slot = c % OUT_SLOTS
                if c >= OUT_SLOTS:
                    out_copy(b, c - OUT_SLOTS, slot).wait()
                else:
                    prev = OUT_CHUNKS - OUT_SLOTS + c
                    pl.when(b > 0)(lambda prev=prev, slot=slot: out_copy(b - 1, prev, slot).wait())
                stage[slot] = acc[:, half * OUT_COLS:(half + 1) * OUT_COLS]
                out_copy(b, c, slot).start()

    @pl.when(b >= nb)
    def _():
        @pl.when(b == nb)
        def _():
            for c in range(OUT_CHUNKS - OUT_SLOTS, OUT_CHUNKS):
                out_copy(b - 1, c, c % OUT_SLOTS).wait()
            for t in range(LOOKAHEAD):
                await_phase(b, t)

        stage[0] = jnp.zeros(stage.shape[1:], F32)
        for c in range(OUT_CHUNKS):
            out_copy(b, c, 0).start()
        for c in range(OUT_CHUNKS):
            out_copy(b, c, 0).wait()


def _moe_ffn_call(layer, slot_tok, block_e, n_blocks, h_packed, w_gate, w_up, w_down):
    any_spec = pl.BlockSpec(memory_space=pl.ANY)
    return pl.pallas_call(
        functools.partial(_moe_ffn_kernel, layer=layer),
        grid_spec=pltpu.PrefetchScalarGridSpec(
            num_scalar_prefetch=3,
            grid=(N_BLOCKS,),
            in_specs=[any_spec] * 4,
            out_specs=any_spec,
            scratch_shapes=[
                pltpu.VMEM((MOE_ROWS * ROW_TILES, 128), U32),
                pltpu.VMEM((MOE_ROWS, HALF_D), BF16),
                pltpu.VMEM((MOE_ROWS, HALF_D), BF16),
                pltpu.VMEM((RING_SLOTS, PIECE_ROWS, PIECE_COLS), F32),
                pltpu.VMEM((MOE_ROWS, EXPERT_HIDDEN), F32),
                pltpu.VMEM((MOE_ROWS, EXPERT_HIDDEN), F32),
                pltpu.VMEM((MOE_ROWS, EXPERT_HIDDEN), BF16),
                pltpu.VMEM((OUT_SLOTS, MOE_ROWS, OUT_COLS), F32),
                pltpu.SemaphoreType.DMA(()),
                pltpu.SemaphoreType.DMA((RING_SLOTS,)),
                pltpu.SemaphoreType.DMA((OUT_SLOTS,)),
            ],
        ),
        out_shape=jax.ShapeDtypeStruct((N_SLOTS, D_MODEL), F32),
        compiler_params=_cparams(("arbitrary",)),
        name="moe_experts",
    )(slot_tok, block_e, n_blocks, h_packed.reshape(TT, ROW_TILES, 128), w_gate, w_up, w_down)


def _combine_kernel(pos_ref, ys_hbm, x_ref, gates_ref, gate2_ref, *rest, tm, unroll, emit_next, split_out):
    if emit_next:
        ng_ref, nsh_ref, nsc_ref = rest[:3]
        rest = rest[3:]
    outs, (buf, sem) = rest[:-2], rest[-2:]
    i = pl.program_id(0)
    slot = i % 2

    def copies(tile, r, s):
        base = tile * (tm * TOP_K)
        return [pltpu.make_async_copy(ys_hbm.at[pl.ds(pos_ref[base + TOP_K * r + k], 1), :],
                                      buf.at[s, k, pl.ds(r, 1), :], sem.at[s]) for k in range(TOP_K)]

    def start_tile(tile, s):
        def body(t, c):
            for u in range(unroll):
                for cp in copies(tile, t * unroll + u, s):
                    cp.start()
            return c
        lax.fori_loop(0, tm // unroll, body, 0)

    def wait_tile(tile, s):
        def body(t, c):
            for u in range(unroll):
                for cp in copies(tile, t * unroll + u, s):
                    cp.wait()
            return c
        lax.fori_loop(0, tm // unroll, body, 0)

    pl.when(i == 0)(lambda: start_tile(0, 0))
    pl.when(i + 1 < pl.num_programs(0))(lambda: start_tile(i + 1, 1 - slot))
    wait_tile(i, slot)
    gates = gates_ref[...]
    y = gates[:, 0:1] * buf[slot, 0] + gates[:, 1:2] * buf[slot, 1]
    x = x_ref[...] + gate2_ref[...] * y
    if split_out:
        is_prompt = i < NP // tm

        @pl.when(is_prompt)
        def _():
            outs[0][...] = x

        @pl.when(jnp.logical_not(is_prompt))
        def _():
            outs[1][...] = x
    else:
        outs[0][...] = x
    if emit_next:
        outs[-1][...] = _modulate(x, ng_ref[...], nsh_ref[...], nsc_ref[...]).astype(BF16)


def _combine_call(pos, ys, x, gates, mod, next_norm=None, split_out=False):
    tm, unroll = 256, 8
    npt = NP // tm
    row = lambda i, pos: (i, 0)
    in_specs = [
        pl.BlockSpec(memory_space=pl.ANY),
        pl.BlockSpec((tm, D_MODEL), row),
        pl.BlockSpec((tm, ROUTER_LANES), row),
        pl.BlockSpec((None, 1, D_MODEL), lambda i, pos: (_cond_row(i * tm), 0, 5)),
    ]
    args = [ys, x, gates, mod]
    if next_norm is not None:
        g_next, mod_next = next_norm
        in_specs += [
            pl.BlockSpec((1, D_MODEL), lambda i, pos: (0, 0)),
            pl.BlockSpec((None, 1, D_MODEL), lambda i, pos: (_cond_row(i * tm), 0, 0)),
            pl.BlockSpec((None, 1, D_MODEL), lambda i, pos: (_cond_row(i * tm), 0, 1)),
        ]
        args += [g_next, mod_next, mod_next]
    if split_out:
        out_specs = [pl.BlockSpec((tm, D_MODEL), lambda i, pos: (jnp.minimum(i, npt - 1), 0)),
                     pl.BlockSpec((tm, D_MODEL), lambda i, pos: (jnp.maximum(i - npt, 0), 0))]
        out_shape = [jax.ShapeDtypeStruct((NP, D_MODEL), F32), jax.ShapeDtypeStruct((NS, D_MODEL), F32)]
    else:
        out_specs = [pl.BlockSpec((tm, D_MODEL), row)]
        out_shape = [jax.ShapeDtypeStruct((TT, D_MODEL), F32)]
    if next_norm is not None:
        out_specs.append(pl.BlockSpec((tm, D_MODEL), row))
        out_shape.append(jax.ShapeDtypeStruct((TT, D_MODEL), BF16))
    return pl.pallas_call(
        functools.partial(_combine_kernel, tm=tm, unroll=unroll, emit_next=next_norm is not None,
                          split_out=split_out),
        grid_spec=pltpu.PrefetchScalarGridSpec(
            num_scalar_prefetch=1,
            grid=(TT // tm,),
            in_specs=in_specs,
            out_specs=out_specs,
            scratch_shapes=[pltpu.VMEM((2, TOP_K, tm, D_MODEL), F32), pltpu.SemaphoreType.DMA((2,))],
        ),
        out_shape=out_shape,
        compiler_params=_cparams(("arbitrary",)),
        name="moe_combine",
    )(pos, *args)


def _dispatch_plan(ids):
    flat_e = ids[:, :TOP_K].reshape(N_ASSIGN)
    onehot = (flat_e[:, None] == jnp.arange(N_EXPERTS, dtype=I32)[None, :]).astype(I32)
    csum = jnp.cumsum(onehot, axis=0)
    rank = jnp.take_along_axis(csum, flat_e[:, None], axis=1)[:, 0] - 1
    counts = csum[-1]
    padded = (counts + MOE_ROWS - 1) // MOE_ROWS * MOE_ROWS
    pad_ends = jnp.cumsum(padded)
    pad_starts = pad_ends - padded
    dest = (pad_starts[flat_e] + rank).astype(I32)
    slot_tok = jnp.zeros((N_SLOTS,), I32).at[dest].set(jnp.arange(N_ASSIGN, dtype=I32) // TOP_K)
    n_blocks = (pad_ends[-1] // MOE_ROWS).astype(I32)
    blk = jnp.minimum(jnp.arange(N_BLOCKS, dtype=I32), n_blocks - 1)
    block_e = jnp.minimum(jnp.searchsorted(pad_ends, blk * MOE_ROWS, side="right"), N_EXPERTS - 1).astype(I32)
    return dest, slot_tok, block_e, n_blocks.reshape(1)


def _moe(layer, x, g, mod, w_router2, b_router, w_gate, w_up, w_down, **combine_kw):
    h_packed, ids, gates = _router_call(x, g, mod, w_router2, b_router)
    dest, slot_tok, block_e, n_blocks = _dispatch_plan(ids)
    ys = _moe_ffn_call(layer, slot_tok, block_e, n_blocks, h_packed, w_gate, w_up, w_down)
    return _combine_call(dest, ys, x, gates, mod, **combine_kw)


def _rope_tables(rot_dim):
    rows = DEC_SEQ // GRID_W
    nf = rot_dim // 4
    inv = jnp.exp(-math.log(ROPE_THETA) * jnp.arange(nf, dtype=F32) / nf)
    row_pos = jnp.repeat(jnp.arange(rows, dtype=F32), GRID_W)
    col_pos = jnp.tile(jnp.arange(GRID_W, dtype=F32), rows)
    ang = jnp.stack([row_pos[:, None] * inv, col_pos[:, None] * inv], axis=1)
    cos = jnp.broadcast_to(jnp.cos(ang)[:, :, None, :], (DEC_SEQ, 2, 2, nf)).reshape(DEC_SEQ, rot_dim)
    sgn = jnp.array([-1.0, 1.0], F32)[None, None, :, None]
    sin = (jnp.sin(ang)[:, :, None, :] * sgn).reshape(DEC_SEQ, rot_dim)
    pad = 128 - rot_dim
    if pad:
        cos = jnp.concatenate([cos, jnp.ones((DEC_SEQ, pad), F32)], axis=1)
        sin = jnp.concatenate([sin, jnp.zeros((DEC_SEQ, pad), F32)], axis=1)
    return cos, sin


def _pad_head_vec(v):
    return jnp.concatenate([v, jnp.zeros((MLA_HEAD_PAD - MLA_QK,), F32)]).reshape(1, MLA_HEAD_PAD)


def _router_weights(w_group, b_group, w_router, b_router):
    pad = ROUTER_LANES - N_GROUPS - N_EXPERTS
    w = jnp.concatenate([w_group, w_router, jnp.zeros((D_MODEL, pad), F32)], axis=1)
    w_hi = w.astype(BF16)
    w_lo = (w - w_hi.astype(F32)).astype(BF16)
    b = jnp.concatenate([b_group, b_router, jnp.zeros((pad,), F32)]).reshape(1, ROUTER_LANES)
    return jnp.concatenate([w_hi, w_lo], axis=1), b


def kernel(x_prompt, x_sample, cache_mla_ckv, cache_mla_krope, cache_diff_k, cache_diff_v, c, c_ctx, norm1_g, norm2_g, ada_w, ada_b, mla_w_in, mla_q_a_norm, mla_w_q_up, mla_kv_a_norm, mla_w_kv_up, mla_q_norm, mla_k_norm, mla_w_out, diff_w_qkv, diff_q_norm, diff_k_norm, diff_lambda_q1, diff_lambda_k1, diff_lambda_q2, diff_lambda_k2, diff_out_norm, diff_w_out, moe_w_group, moe_b_group, moe_w_router, moe_b_router, moe_w_gate, moe_w_up, moe_w_down):
    assert DEPTH == 2
    xp0 = x_prompt.reshape(NP, D_MODEL)
    xs0 = x_sample.reshape(NS, D_MODEL)
    cond = jnp.concatenate([c_ctx[None, :], c, jnp.zeros((N_COND - 1 - DEC_BATCH, D_MODEL), F32)], axis=0)
    mods = _ada_call(cond, ada_w, ada_b).reshape(DEPTH, N_COND, 1, 6 * D_MODEL)
    routers = [_router_weights(moe_w_group[i], moe_b_group[i], moe_w_router[i], moe_b_router[i])
               for i in range(DEPTH)]
    g2 = [norm2_g[i].reshape(1, D_MODEL) for i in range(DEPTH)]

    mod = mods[0]
    cos, sin = _rope_tables(MLA_ROPE)
    w_in = jnp.pad(mla_w_in[0], ((0, 0), (0, MLA_IN_PAD - MLA_IN))).astype(BF16)
    w_q = jnp.pad(mla_w_q_up[0].reshape(MLA_Q_LORA, MLA_HEADS, MLA_QK), ((0, 0), (0, 0), (0, MLA_HEAD_PAD - MLA_QK)))
    w_q = w_q.reshape(MLA_Q_LORA, MLA_HEADS * MLA_HEAD_PAD).astype(BF16)
    cq, ckv, kr = _mla_in_call(xp0, xs0, norm1_g[0].reshape(1, D_MODEL), mod, w_in,
                               mla_q_a_norm[0].reshape(1, -1), mla_kv_a_norm[0].reshape(1, -1))
    q = _q_up_call(cq, w_q, _pad_head_vec(mla_q_norm[0]), cos, sin)
    ckv_all = jnp.concatenate([ckv, cache_mla_ckv[:, 0].reshape(NCTX, MLA_KV_LORA)], axis=0)
    kr_ctx = jnp.pad(cache_mla_krope[:, 0].reshape(NCTX, MLA_ROPE), ((0, 0), (0, 128 - MLA_ROPE)))
    kr_all = jnp.concatenate([kr, kr_ctx], axis=0)
    k, v = _kv_up_call(ckv_all, kr_all, mla_w_kv_up[0].astype(BF16), _pad_head_vec(mla_k_norm[0]), cos, sin)
    o_p = _mla_attn_prompt_call(q, k, v)
    o_s = _mla_attn_latent_call(q, k, v)
    x = _out_proj_call(o_p, o_s, mla_w_out[0].astype(BF16), xp0, xs0, True, mod)
    new_ckv = ckv[:NP].reshape(BATCH, 1, SEQ, MLA_KV_LORA)
    new_krope = kr[:NP, :MLA_ROPE].reshape(BATCH, 1, SEQ, MLA_ROPE)
    x, h = _moe(0, x, g2[0], mod, *routers[0], moe_w_gate, moe_w_up, moe_w_down,
                next_norm=(norm1_g[1].reshape(1, D_MODEL), mods[1]))

    mod = mods[1]
    lam_init = 0.8 - 0.6 * math.exp(-0.3 * 1)
    cos, sin = _rope_tables(DIFF_HEAD_DIM)
    q, k_p, k_s, v_p, v_s = _diff_qkv_call(h, diff_w_qkv[0].astype(BF16), diff_q_norm[0].reshape(1, -1),
                                           diff_k_norm[0].reshape(1, -1), cos, sin)
    lams = [t[0].reshape(1, DIFF_HEAD_DIM) for t in
            (diff_lambda_q1, diff_lambda_k1, diff_lambda_q2, diff_lambda_k2)]
    out_norm = diff_out_norm[0].reshape(1, DIFF_V_DIM)
    o_p = _diff_attn_prompt_call(q, k_p, v_p, lams, out_norm, lam_init)
    o_s = _diff_attn_latent_call(q, k_s, v_s, cache_diff_k[:, 0].reshape(NCTX, D_MODEL),
                                 cache_diff_v[:, 0].reshape(NCTX, D_MODEL), lams, out_norm, lam_init)
    x = _out_proj_call(o_p, o_s, diff_w_out[0].astype(BF16), x, x, False, mod)
    new_k = k_p.reshape(BATCH, 1, SEQ, DIFF_HEADS, 2, DIFF_HEAD_DIM)
    new_v = v_p.reshape(BATCH, 1, SEQ, DIFF_HEADS, DIFF_V_DIM)
    y_p, y_s = _moe(1, x, g2[1], mod, *routers[1], moe_w_gate, moe_w_up, moe_w_down, split_out=True)

    return (y_p.reshape(BATCH, SEQ, D_MODEL), y_s.reshape(DEC_BATCH, DEC_SEQ, D_MODEL),
            new_ckv, new_krope, new_k, new_v)
```
